```python
import jax, jax.numpy as jnp
from jax import lax
import numpy as np

D_MODEL = 2048
BATCH = 1
SEQ = 16384
DEPTH = 2

N_MIXERS = 2
N_MLA_LAYERS = (DEPTH + 1) // 2
N_SB_LAYERS = DEPTH // 2

MLA_HEADS = 16
Q_LORA_RANK = 512
KV_LORA_RANK = 512
QK_NOPE_DIM = 128
QK_ROPE_DIM = 64
V_HEAD_DIM = 128
ROPE_THETA = 10000.0

SB_HEADS = 16
SB_HEAD_DIM = 128

D_FF = ((8 * D_MODEL + 3 * 256 - 1) // (3 * 256)) * 256

Q_BLOCK = 128
RMS_EPS = 1e-6

kernel_name = "hybrid_mla_stickbreaking_trunk"


def _rmsnorm(x, g):
    xf = x.astype(jnp.float32)
    y = xf * lax.rsqrt(jnp.mean(xf * xf, axis=-1, keepdims=True) + RMS_EPS)
    return (y * g.astype(jnp.float32)).astype(x.dtype)


def _rope_tables(positions):
    inv_freq = ROPE_THETA ** (-jnp.arange(0, QK_ROPE_DIM, 2, dtype=jnp.float32) / QK_ROPE_DIM)
    ang = positions.astype(jnp.float32)[..., None] * inv_freq
    return jnp.cos(ang), jnp.sin(ang)


def _apply_rope(x, cos, sin):
    xf = x.astype(jnp.float32)
    x1, x2 = jnp.split(xf, 2, axis=-1)
    out = jnp.concatenate([x1 * cos - x2 * sin, x2 * cos + x1 * sin], axis=-1)
    return out.astype(x.dtype)


def _sweep_query_blocks(block_fn, q_parts):
    b, s = q_parts[0].shape[:2]
    n_blk = s // Q_BLOCK
    blocked = tuple(jnp.moveaxis(q.reshape((b, n_blk, Q_BLOCK) + q.shape[2:]), 1, 0) for q in q_parts)
    out = lax.map(block_fn, blocked + (jnp.arange(n_blk, dtype=jnp.int32),))
    out = jnp.moveaxis(out, 0, 1)
    return out.reshape(b, s, -1)


def _mla_mixer(h, positions, w_in, q_norm, kv_norm, w_uq, w_ukv, w_o):
    b, s, _ = h.shape
    proj = h @ w_in
    c_q = _rmsnorm(proj[..., :Q_LORA_RANK], q_norm)
    c_kv = _rmsnorm(proj[..., Q_LORA_RANK:Q_LORA_RANK + KV_LORA_RANK], kv_norm)
    k_rope = proj[..., Q_LORA_RANK + KV_LORA_RANK:]
    q = (c_q @ w_uq).reshape(b, s, MLA_HEADS, QK_NOPE_DIM + QK_ROPE_DIM)
    q_nope, q_rope = q[..., :QK_NOPE_DIM], q[..., QK_NOPE_DIM:]
    kv = (c_kv @ w_ukv).reshape(b, s, MLA_HEADS, QK_NOPE_DIM + V_HEAD_DIM)
    k_nope, v = kv[..., :QK_NOPE_DIM], kv[..., QK_NOPE_DIM:]
    cos, sin = _rope_tables(positions)
    q_rope = _apply_rope(q_rope, cos[:, :, None, :], sin[:, :, None, :])
    k_rope = _apply_rope(k_rope, cos, sin)
    scale = (QK_NOPE_DIM + QK_ROPE_DIM) ** -0.5
    k_idx = jnp.arange(s, dtype=jnp.int32)

    def block(args):
        qn, qr, blk = args
        q_idx = blk * Q_BLOCK + jnp.arange(Q_BLOCK, dtype=jnp.int32)
        sc = (jnp.einsum('bqhd,bkhd->bhqk', qn, k_nope)
              + jnp.einsum('bqhr,bkr->bhqk', qr, k_rope)).astype(jnp.float32) * scale
        causal = k_idx[None, :] <= q_idx[:, None]
        p = jax.nn.softmax(jnp.where(causal, sc, -jnp.inf), axis=-1)
        return jnp.einsum('bhqk,bkhd->bqhd', p.astype(v.dtype), v)

    o = _sweep_query_blocks(block, (q_nope, q_rope))
    return o @ w_o


def _stick_breaking_mixer(h, w_in, w_o):
    b, s, _ = h.shape
    qkv = (h @ w_in).reshape(b, s, 3, SB_HEADS, SB_HEAD_DIM)
    q, k, v = qkv[:, :, 0], qkv[:, :, 1], qkv[:, :, 2]
    scale = SB_HEAD_DIM ** -0.5
    k_idx = jnp.arange(s, dtype=jnp.int32)

    def block(args):
        qb, blk = args
        q_idx = blk * Q_BLOCK + jnp.arange(Q_BLOCK, dtype=jnp.int32)
        z = jnp.einsum('bqhd,bkhd->bhqk', qb, k).astype(jnp.float32) * scale
        strict = k_idx[None, :] < q_idx[:, None]
        log_beta = jax.nn.log_sigmoid(z)
        log_one_minus = jnp.where(strict, jax.nn.log_sigmoid(-z), 0.0)
        after = lax.cumsum(log_one_minus, axis=log_one_minus.ndim - 1, reverse=True) - log_one_minus
        a = jnp.where(strict, jnp.exp(log_beta + after), 0.0)
        return jnp.einsum('bhqk,bkhd->bqhd', a.astype(v.dtype), v)

    o = _sweep_query_blocks(block, (q,))
    return o @ w_o


def _swiglu(h, w_gate_up, w_down):
    gu = h @ w_gate_up
    g, u = gu[..., :D_FF], gu[..., D_FF:]
    return (jax.nn.silu(g) * u) @ w_down


def setup_inputs(seed: int = 0) -> dict:
    key = jax.random.key(seed)
    ks = jax.random.split(key, 16)

    def dense(k, shape):
        return jax.random.normal(k, shape, jnp.float32) * (shape[-2] ** -0.5)

    def gain(k, shape):
        return 1.0 + 0.02 * jax.random.normal(k, shape, jnp.float32)

    x = jax.random.normal(ks[0], (BATCH, SEQ, D_MODEL), jnp.float32)
    positions = jnp.broadcast_to(jnp.arange(SEQ, dtype=jnp.int32), (BATCH, SEQ))
    return {
        "x": x,
        "positions": positions,
        "attn_norm": gain(ks[1], (DEPTH, D_MODEL)),
        "mla_w_in": dense(ks[2], (N_MLA_LAYERS, D_MODEL, Q_LORA_RANK + KV_LORA_RANK + QK_ROPE_DIM)),
        "mla_q_norm": gain(ks[3], (N_MLA_LAYERS, Q_LORA_RANK)),
        "mla_kv_norm": gain(ks[4], (N_MLA_LAYERS, KV_LORA_RANK)),
        "mla_w_uq": dense(ks[5], (N_MLA_LAYERS, Q_LORA_RANK, MLA_HEADS * (QK_NOPE_DIM + QK_ROPE_DIM))),
        "mla_w_ukv": dense(ks[6], (N_MLA_LAYERS, KV_LORA_RANK, MLA_HEADS * (QK_NOPE_DIM + V_HEAD_DIM))),
        "mla_w_o": dense(ks[7], (N_MLA_LAYERS, MLA_HEADS * V_HEAD_DIM, D_MODEL)),
        "sb_w_in": dense(ks[8], (N_SB_LAYERS, D_MODEL, 3 * SB_HEADS * SB_HEAD_DIM)),
        "sb_w_o": dense(ks[9], (N_SB_LAYERS, SB_HEADS * SB_HEAD_DIM, D_MODEL)),
        "ffn_norm": gain(ks[10], (DEPTH, D_MODEL)),
        "ffn_w_gate_up": dense(ks[11], (DEPTH, D_MODEL, 2 * D_FF)),
        "ffn_w_down": dense(ks[12], (DEPTH, D_FF, D_MODEL)),
        "final_norm": gain(ks[13], (D_MODEL,)),
    }


def reference(x, positions, attn_norm, mla_w_in, mla_q_norm, mla_kv_norm, mla_w_uq, mla_w_ukv,
              mla_w_o, sb_w_in, sb_w_o, ffn_norm, ffn_w_gate_up, ffn_w_down, final_norm):
    h = x
    for i in range(DEPTH):
        a = _rmsnorm(h, attn_norm[i])
        j = i // N_MIXERS
        if i % N_MIXERS == 0:
            mix = _mla_mixer(a, positions, mla_w_in[j], mla_q_norm[j], mla_kv_norm[j],
                             mla_w_uq[j], mla_w_ukv[j], mla_w_o[j])
        else:
            mix = _stick_breaking_mixer(a, sb_w_in[j], sb_w_o[j])
        h = h + mix
        f = _rmsnorm(h, ffn_norm[i])
        h = h + _swiglu(f, ffn_w_gate_up[i], ffn_w_down[i])
    return _rmsnorm(h, final_norm)
```

```python
import functools

import jax
import jax.numpy as jnp
from jax import lax
from jax.experimental import pallas as pl
from jax.experimental.pallas import tpu as pltpu

D_MODEL = 2048
MLA_HEADS = 16
Q_LORA_RANK = 512
KV_LORA_RANK = 512
QK_NOPE_DIM = 128
QK_ROPE_DIM = 64
V_HEAD_DIM = 128
ROPE_THETA = 10000.0
SB_HEADS = 16
SB_HEAD_DIM = 128
RMS_EPS = 1e-6

LANES = 128
QK_PAD_DIM = 2 * LANES
VMEM_LIMIT_BYTES = 56 * 1024 * 1024

F32 = jnp.float32
BF16 = jnp.bfloat16


def _params(n_axes):
    return pltpu.CompilerParams(
        dimension_semantics=("arbitrary",) * n_axes, vmem_limit_bytes=VMEM_LIMIT_BYTES)


def _dot(a, b):
    return jnp.dot(a, b, preferred_element_type=F32)


def _dot_nt(a, b):
    return lax.dot_general(a, b, (((1,), (1,)), ((), ())), preferred_element_type=F32)


def _rms(x, g):
    return x * lax.rsqrt(jnp.mean(x * x, axis=-1, keepdims=True) + RMS_EPS) * g


def _rope(t, c, s1, s2):
    half = QK_ROPE_DIM // 2
    return t * c + pltpu.roll(t, half, 1) * s1 + pltpu.roll(t, LANES - half, 1) * s2


def _rope_tables_kernel(pos_ref, freq_ref, c_ref, s1_ref, s2_ref):
    half = QK_ROPE_DIM // 2
    ang = pos_ref[...].astype(F32) * freq_ref[...]
    lane = lax.broadcasted_iota(jnp.int32, ang.shape, 1)
    cos, sin = jnp.cos(ang), jnp.sin(ang)
    c_ref[...] = jnp.where(lane < 2 * half, cos, 0.0)
    s1_ref[...] = jnp.where((lane >= half) & (lane < 2 * half), sin, 0.0)
    s2_ref[...] = jnp.where(lane < half, -sin, 0.0)


def _rope_tables(positions, tm=2048):
    s = positions.shape[0]
    inv_freq = ROPE_THETA ** (-jnp.arange(0, QK_ROPE_DIM, 2, dtype=F32) / QK_ROPE_DIM)
    freq = jnp.concatenate([inv_freq, inv_freq, jnp.zeros((LANES - QK_ROPE_DIM,), F32)])[None, :]
    out = jax.ShapeDtypeStruct((s, LANES), F32)
    row = pl.BlockSpec((tm, LANES), lambda i: (i, 0))
    return pl.pallas_call(
        _rope_tables_kernel,
        grid=(s // tm,),
        in_specs=[pl.BlockSpec((tm, 1), lambda i: (i, 0)), pl.BlockSpec((1, LANES), lambda i: (0, 0))],
        out_specs=[row, row, row],
        out_shape=[out, out, out],
        compiler_params=_params(1),
        name="rope_tables",
    )(positions, freq)


def _norm_kernel(x_ref, g_ref, o_ref):
    o_ref[...] = _rms(x_ref[...], g_ref[...]).astype(o_ref.dtype)


def _rmsnorm(x, g, out_dtype, tm=512):
    s, d = x.shape
    return pl.pallas_call(
        _norm_kernel,
        grid=(s // tm,),
        in_specs=[pl.BlockSpec((tm, d), lambda i: (i, 0)), pl.BlockSpec((1, d), lambda i: (0, 0))],
        out_specs=pl.BlockSpec((tm, d), lambda i: (i, 0)),
        out_shape=jax.ShapeDtypeStruct((s, d), out_dtype),
        compiler_params=_params(1),
        name="rmsnorm",
    )(x, g[None, :])


def _mla_in_kernel(a_ref, w_ref, qn_ref, kvn_ref, c_ref, s1_ref, s2_ref, cq_ref, ckv_ref, kr_ref):
    proj = _dot(a_ref[...], w_ref[...])
    cq_ref[...] = _rms(proj[:, :Q_LORA_RANK], qn_ref[...]).astype(BF16)
    ckv_ref[...] = _rms(proj[:, Q_LORA_RANK:Q_LORA_RANK + KV_LORA_RANK], kvn_ref[...]).astype(BF16)
    kr = proj[:, Q_LORA_RANK + KV_LORA_RANK:]
    kr_ref[...] = _rope(kr, c_ref[...], s1_ref[...], s2_ref[...]).astype(BF16)


def _mla_in(a, w_pad, q_norm, kv_norm, tabs, tm=512):
    s, d = a.shape
    n = w_pad.shape[1]
    row = lambda w: pl.BlockSpec((tm, w), lambda i: (i, 0))
    const = lambda r, w: pl.BlockSpec((r, w), lambda i: (0, 0))
    return pl.pallas_call(
        _mla_in_kernel,
        grid=(s // tm,),
        in_specs=[row(d), const(d, n), const(1, Q_LORA_RANK), const(1, KV_LORA_RANK),
                  row(LANES), row(LANES), row(LANES)],
        out_specs=[row(Q_LORA_RANK), row(KV_LORA_RANK), row(LANES)],
        out_shape=[jax.ShapeDtypeStruct((s, Q_LORA_RANK), BF16),
                   jax.ShapeDtypeStruct((s, KV_LORA_RANK), BF16),
                   jax.ShapeDtypeStruct((s, LANES), BF16)],
        compiler_params=_params(1),
        name="mla_in",
    )(a, w_pad, q_norm[None, :], kv_norm[None, :], *tabs)


def _mla_q_kernel(cq_ref, w_ref, c_ref, s1_ref, s2_ref, q_ref, *, heads, scale):
    acc = _dot(cq_ref[...], w_ref[...])
    c, s1, s2 = c_ref[...], s1_ref[...], s2_ref[...]
    for h in range(heads):
        lo = h * QK_PAD_DIM
        q_ref[:, lo:lo + LANES] = (acc[:, lo:lo + LANES] * scale).astype(BF16)
        rp = _rope(acc[:, lo + LANES:lo + QK_PAD_DIM], c, s1, s2)
        q_ref[:, lo + LANES:lo + QK_PAD_DIM] = (rp * scale).astype(BF16)


def _mla_q(cq, w_uq_pad, tabs, scale, tm=1024, heads_per_tile=4):
    s, r = cq.shape
    n = w_uq_pad.shape[1]
    tn = heads_per_tile * QK_PAD_DIM
    row = pl.BlockSpec((tm, LANES), lambda i, j: (i, 0))
    return pl.pallas_call(
        functools.partial(_mla_q_kernel, heads=heads_per_tile, scale=scale),
        grid=(s // tm, n // tn),
        in_specs=[pl.BlockSpec((tm, r), lambda i, j: (i, 0)), pl.BlockSpec((r, tn), lambda i, j: (0, j)),
                  row, row, row],
        out_specs=pl.BlockSpec((tm, tn), lambda i, j: (i, j)),
        out_shape=jax.ShapeDtypeStruct((s, n), BF16),
        compiler_params=_params(2),
        name="mla_q",
    )(cq, w_uq_pad, *tabs)


def _mla_kv_kernel(ckv_ref, wk_ref, wv_ref, kr_ref, k_ref, v_ref, *, heads):
    ckv = ckv_ref[...]
    kn = _dot(ckv, wk_ref[...])
    v_ref[...] = _dot(ckv, wv_ref[...]).astype(BF16)
    kr = kr_ref[...]
    for h in range(heads):
        k_ref[:, h * QK_PAD_DIM:h * QK_PAD_DIM + LANES] = kn[:, h * LANES:(h + 1) * LANES].astype(BF16)
        k_ref[:, h * QK_PAD_DIM + LANES:(h + 1) * QK_PAD_DIM] = kr


def _mla_kv(ckv, w_uk, w_uv, kr, tm=1024, heads_per_tile=4):
    s, r = ckv.shape
    n = w_uk.shape[1]
    tn = heads_per_tile * LANES
    return pl.pallas_call(
        functools.partial(_mla_kv_kernel, heads=heads_per_tile),
        grid=(s // tm, n // tn),
        in_specs=[pl.BlockSpec((tm, r), lambda i, j: (i, 0)),
                  pl.BlockSpec((r, tn), lambda i, j: (0, j)),
                  pl.BlockSpec((r, tn), lambda i, j: (0, j)),
                  pl.BlockSpec((tm, LANES), lambda i, j: (i, 0))],
        out_specs=[pl.BlockSpec((tm, 2 * tn), lambda i, j: (i, j)),
                   pl.BlockSpec((tm, tn), lambda i, j: (i, j))],
        out_shape=[jax.ShapeDtypeStruct((s, 2 * n), BF16), jax.ShapeDtypeStruct((s, n), BF16)],
        compiler_params=_params(2),
        name="mla_kv",
    )(ckv, w_uk, w_uv, kr)


def _mla_attn_kernel(q_ref, k_ref, v_ref, o_ref, m_sc, l_sc, acc_sc, *, tq, tk):
    i = pl.program_id(1)
    q = q_ref[...]
    m_sc[...] = jnp.full(m_sc.shape, -jnp.inf, F32)
    l_sc[...] = jnp.zeros(l_sc.shape, F32)
    acc_sc[...] = jnp.zeros(acc_sc.shape, F32)
    rep = tk // LANES

    def step(j, masked):
        start = pl.multiple_of(j * tk, tk)
        s = _dot_nt(q, k_ref[pl.ds(start, tk), :])
        if masked:
            row = i * tq + lax.broadcasted_iota(jnp.int32, (tq, tk), 0)
            col = j * tk + lax.broadcasted_iota(jnp.int32, (tq, tk), 1)
            s = jnp.where(col <= row, s, -jnp.inf)
        m_prev = m_sc[...]
        m_next = jnp.maximum(m_prev, jnp.max(s, axis=1, keepdims=True))
        p = jnp.exp(s - jnp.tile(m_next, (1, rep)))
        alpha = jnp.exp(m_prev - m_next)
        l_sc[...] = alpha * l_sc[...] + jnp.sum(p, axis=1, keepdims=True)
        acc_sc[...] = alpha * acc_sc[...] + _dot(p.astype(BF16), v_ref[pl.ds(start, tk), :])
        m_sc[...] = m_next

    def body(j, carry):
        step(j, False)
        return carry

    lax.fori_loop(0, i * (tq // tk), body, 0)
    for d in range(tq // tk):
        step(i * (tq // tk) + d, True)
    o_ref[...] = (acc_sc[...] / l_sc[...]).astype(o_ref.dtype)


def _mla_attn(q, k, v, tq=512, tk=512):
    s = q.shape[0]
    h = MLA_HEADS
    return pl.pallas_call(
        functools.partial(_mla_attn_kernel, tq=tq, tk=tk),
        grid=(h, s // tq),
        in_specs=[pl.BlockSpec((tq, QK_PAD_DIM), lambda hh, i: (i, hh)),
                  pl.BlockSpec((s, QK_PAD_DIM), lambda hh, i: (0, hh)),
                  pl.BlockSpec((s, V_HEAD_DIM), lambda hh, i: (0, hh))],
        out_specs=pl.BlockSpec((tq, V_HEAD_DIM), lambda hh, i: (i, hh)),
        out_shape=jax.ShapeDtypeStruct((s, h * V_HEAD_DIM), BF16),
        scratch_shapes=[pltpu.VMEM((tq, LANES), F32), pltpu.VMEM((tq, LANES), F32),
                        pltpu.VMEM((tq, V_HEAD_DIM), F32)],
        compiler_params=_params(2),
        name="mla_attn",
    )(q, k, v)


def _sb_attn_kernel(q_ref, k_ref, v_ref, o_ref, carry_sc, acc_sc, *, tq, tk):
    i = pl.program_id(1)
    q = q_ref[...]
    carry_sc[...] = jnp.zeros(carry_sc.shape, F32)
    acc_sc[...] = jnp.zeros(acc_sc.shape, F32)
    rep = tk // LANES
    tri = (lax.broadcasted_iota(jnp.int32, (tk, tk), 0)
           > lax.broadcasted_iota(jnp.int32, (tk, tk), 1)).astype(BF16)

    def step(j, masked):
        start = pl.multiple_of(j * tk, tk)
        z = _dot_nt(q, k_ref[pl.ds(start, tk), :])
        soft = jnp.log1p(jnp.exp(-jnp.abs(z)))
        log_om = -(jnp.maximum(z, 0.0) + soft)
        log_beta = jnp.minimum(z, 0.0) - soft
        if masked:
            row = i * tq + lax.broadcasted_iota(jnp.int32, (tq, tk), 0)
            col = j * tk + lax.broadcasted_iota(jnp.int32, (tq, tk), 1)
            strict = col < row
            log_om = jnp.where(strict, log_om, 0.0)
        after = _dot(log_om.astype(BF16), tri) + jnp.tile(carry_sc[...], (1, rep))
        a = jnp.exp(log_beta + after)
        if masked:
            a = jnp.where(strict, a, 0.0)
        acc_sc[...] += _dot(a.astype(BF16), v_ref[pl.ds(start, tk), :])
        carry_sc[...] += jnp.sum(log_om, axis=1, keepdims=True)

    n_diag = tq // tk
    last = (i + 1) * n_diag - 1
    for d in range(n_diag):
        step(last - d, True)

    def body(t, carry):
        step(i * n_diag - 1 - t, False)
        return carry

    lax.fori_loop(0, i * n_diag, body, 0)
    o_ref[...] = acc_sc[...].astype(o_ref.dtype)


def _sb_attn(qkv, tq=512, tk=256):
    s = qkv.shape[0]
    h, d = SB_HEADS, SB_HEAD_DIM
    return pl.pallas_call(
        functools.partial(_sb_attn_kernel, tq=tq, tk=tk),
        grid=(h, s // tq),
        in_specs=[pl.BlockSpec((tq, d), lambda hh, i: (i, hh)),
                  pl.BlockSpec((s, d), lambda hh, i: (0, h + hh)),
                  pl.BlockSpec((s, d), lambda hh, i: (0, 2 * h + hh))],
        out_specs=pl.BlockSpec((tq, d), lambda hh, i: (i, hh)),
        out_shape=jax.ShapeDtypeStruct((s, h * d), BF16),
        scratch_shapes=[pltpu.VMEM((tq, LANES), F32), pltpu.VMEM((tq, d), F32)],
        compiler_params=_params(2),
        name="sb_attn",
    )(qkv, qkv, qkv)


def _matmul_scale_kernel(a_ref, w_ref, cs_ref, o_ref):
    o_ref[...] = (_dot(a_ref[...], w_ref[...]) * cs_ref[...]).astype(o_ref.dtype)


def _matmul_scale(a, w, col_scale, tm=1024, tn=1024):
    s, k = a.shape
    n = w.shape[1]
    return pl.pallas_call(
        _matmul_scale_kernel,
        grid=(s // tm, n // tn),
        in_specs=[pl.BlockSpec((tm, k), lambda i, j: (i, 0)), pl.BlockSpec((k, tn), lambda i, j: (0, j)),
                  pl.BlockSpec((1, tn), lambda i, j: (0, j))],
        out_specs=pl.BlockSpec((tm, tn), lambda i, j: (i, j)),
        out_shape=jax.ShapeDtypeStruct((s, n), BF16),
        compiler_params=_params(2),
        name="matmul_scale",
    )(a, w, col_scale[None, :])


def _matmul_residual_kernel(a_ref, w_ref, r_ref, o_ref):
    o_ref[...] = r_ref[...] + _dot(a_ref[...], w_ref[...])


def _matmul_residual(a, w, res, tm=512, tn=512):
    s, k = a.shape
    n = w.shape[1]
    return pl.pallas_call(
        _matmul_residual_kernel,
        grid=(s // tm, n // tn),
        in_specs=[pl.BlockSpec((tm, k), lambda i, j: (i, 0)), pl.BlockSpec((k, tn), lambda i, j: (0, j)),
                  pl.BlockSpec((tm, tn), lambda i, j: (i, j))],
        out_specs=pl.BlockSpec((tm, tn), lambda i, j: (i, j)),
        out_shape=jax.ShapeDtypeStruct((s, n), F32),
        compiler_params=_params(2),
        name="matmul_residual",
    )(a, w, res)


def _ffn_up_kernel(a_ref, wg_ref, wu_ref, o_ref):
    a = a_ref[...]
    g = _dot(a, wg_ref[...])
    u = _dot(a, wu_ref[...])
    o_ref[...] = (g * jax.nn.sigmoid(g) * u).astype(o_ref.dtype)


def _ffn_up(a, w_gate_up, tm=1024, tn=512):
    s, k = a.shape
    d_ff = w_gate_up.shape[1] // 2
    nj = d_ff // tn
    return pl.pallas_call(
        _ffn_up_kernel,
        grid=(s // tm, nj),
        in_specs=[pl.BlockSpec((tm, k), lambda i, j: (i, 0)),
                  pl.BlockSpec((k, tn), lambda i, j: (0, j)),
                  pl.BlockSpec((k, tn), lambda i, j: (0, nj + j))],
        out_specs=pl.BlockSpec((tm, tn), lambda i, j: (i, j)),
        out_shape=jax.ShapeDtypeStruct((s, d_ff), BF16),
        compiler_params=_params(2),
        name="ffn_up",
    )(a, w_gate_up, w_gate_up)


def _ffn(h, norm_g, w_gate_up, w_down):
    f = _rmsnorm(h, norm_g, BF16)
    act = _ffn_up(f, w_gate_up.astype(BF16))
    return _matmul_residual(act, w_down.astype(BF16), h)


def kernel(x, positions, attn_norm, mla_w_in, mla_q_norm, mla_kv_norm, mla_w_uq, mla_w_ukv, mla_w_o,
           sb_w_in, sb_w_o, ffn_norm, ffn_w_gate_up, ffn_w_down, final_norm):
    b, s, d = x.shape
    assert b == 1 and d == D_MODEL
    h = x.reshape(s, d)

    tabs = _rope_tables(positions.reshape(s, 1))
    w_in_pad = jnp.pad(mla_w_in[0], ((0, 0), (0, LANES - QK_ROPE_DIM))).astype(BF16)
    qk_dim = QK_NOPE_DIM + QK_ROPE_DIM
    w_uq_pad = jnp.pad(mla_w_uq[0].reshape(Q_LORA_RANK, MLA_HEADS, qk_dim),
                       ((0, 0), (0, 0), (0, QK_PAD_DIM - qk_dim)))
    w_uq_pad = w_uq_pad.reshape(Q_LORA_RANK, MLA_HEADS * QK_PAD_DIM).astype(BF16)
    w_ukv = mla_w_ukv[0].reshape(KV_LORA_RANK, MLA_HEADS, QK_NOPE_DIM + V_HEAD_DIM)
    w_uk = w_ukv[:, :, :QK_NOPE_DIM].reshape(KV_LORA_RANK, MLA_HEADS * QK_NOPE_DIM).astype(BF16)
    w_uv = w_ukv[:, :, QK_NOPE_DIM:].reshape(KV_LORA_RANK, MLA_HEADS * V_HEAD_DIM).astype(BF16)

    a = _rmsnorm(h, attn_norm[0], BF16)
    cq, ckv, kr = _mla_in(a, w_in_pad, mla_q_norm[0], mla_kv_norm[0], tabs)
    q = _mla_q(cq, w_uq_pad, tabs, qk_dim ** -0.5)
    k, v = _mla_kv(ckv, w_uk, w_uv, kr)
    o = _mla_attn(q, k, v)
    h = _matmul_residual(o, mla_w_o[0].astype(BF16), h)
    h = _ffn(h, ffn_norm[0], ffn_w_gate_up[0], ffn_w_down[0])

    a = _rmsnorm(h, attn_norm[1], BF16)
    n_q = SB_HEADS * SB_HEAD_DIM
    col_scale = jnp.concatenate([jnp.full((n_q,), SB_HEAD_DIM ** -0.5, F32), jnp.ones((2 * n_q,), F32)])
    qkv = _matmul_scale(a, sb_w_in[0].astype(BF16), col_scale)
    o = _sb_attn(qkv)
    h = _matmul_residual(o, sb_w_o[0].astype(BF16), h)
    h = _ffn(h, ffn_norm[1], ffn_w_gate_up[1], ffn_w_down[1])

    return _rmsnorm(h, final_norm, F32).reshape(b, s, d)
```

```python
import functools

import jax
import jax.numpy as jnp
from jax import lax
from jax.experimental import pallas as pl
from jax.experimental.pallas import tpu as pltpu

D_MODEL = 2048
MLA_HEADS = 16
Q_LORA_RANK = 512
KV_LORA_RANK = 512
QK_NOPE_DIM = 128
QK_ROPE_DIM = 64
V_HEAD_DIM = 128
ROPE_THETA = 10000.0
SB_HEADS = 16
SB_HEAD_DIM = 128
RMS_EPS = 1e-6

LANES = 128
QK_PAD_DIM = 2 * LANES
VMEM_LIMIT_BYTES = 56 * 1024 * 1024

LOG2_E = 1.4426950408889634
SB_LOG_WEIGHT_FLOOR = -120.0

F32 = jnp.float32
BF16 = jnp.bfloat16


def _params(n_axes):
    return pltpu.CompilerParams(
        dimension_semantics=("arbitrary",) * n_axes, vmem_limit_bytes=VMEM_LIMIT_BYTES)


def _dot(a, b):
    return jnp.dot(a, b, preferred_element_type=F32)


def _dot_nt(a, b):
    return lax.dot_general(a, b, (((1,), (1,)), ((), ())), preferred_element_type=F32)


def _rms(x, g):
    return x * lax.rsqrt(jnp.mean(x * x, axis=-1, keepdims=True) + RMS_EPS) * g


def _rope(t, c, s1, s2):
    half = QK_ROPE_DIM // 2
    return t * c + pltpu.roll(t, half, 1) * s1 + pltpu.roll(t, LANES - half, 1) * s2


def _rope_tables_kernel(pos_ref, freq_ref, c_ref, s1_ref, s2_ref):
    half = QK_ROPE_DIM // 2
    ang = pos_ref[...].astype(F32) * freq_ref[...]
    lane = lax.broadcasted_iota(jnp.int32, ang.shape, 1)
    cos, sin = jnp.cos(ang), jnp.sin(ang)
    c_ref[...] = jnp.where(lane < 2 * half, cos, 0.0)
    s1_ref[...] = jnp.where((lane >= half) & (lane < 2 * half), sin, 0.0)
    s2_ref[...] = jnp.where(lane < half, -sin, 0.0)


def _rope_tables(positions, tm=2048):
    s = positions.shape[0]
    inv_freq = ROPE_THETA ** (-jnp.arange(0, QK_ROPE_DIM, 2, dtype=F32) / QK_ROPE_DIM)
    freq = jnp.concatenate([inv_freq, inv_freq, jnp.zeros((LANES - QK_ROPE_DIM,), F32)])[None, :]
    out = jax.ShapeDtypeStruct((s, LANES), F32)
    row = pl.BlockSpec((tm, LANES), lambda i: (i, 0))
    return pl.pallas_call(
        _rope_tables_kernel,
        grid=(s // tm,),
        in_specs=[pl.BlockSpec((tm, 1), lambda i: (i, 0)), pl.BlockSpec((1, LANES), lambda i: (0, 0))],
        out_specs=[row, row, row],
        out_shape=[out, out, out],
        compiler_params=_params(1),
        name="rope_tables",
    )(positions, freq)


def _norm_kernel(x_ref, g_ref, o_ref):
    o_ref[...] = _rms(x_ref[...], g_ref[...]).astype(o_ref.dtype)


def _rmsnorm(x, g, out_dtype, tm=512):
    s, d = x.shape
    return pl.pallas_call(
        _norm_kernel,
        grid=(s // tm,),
        in_specs=[pl.BlockSpec((tm, d), lambda i: (i, 0)), pl.BlockSpec((1, d), lambda i: (0, 0))],
        out_specs=pl.BlockSpec((tm, d), lambda i: (i, 0)),
        out_shape=jax.ShapeDtypeStruct((s, d), out_dtype),
        compiler_params=_params(1),
        name="rmsnorm",
    )(x, g[None, :])


def _mla_in_kernel(a_ref, w_ref, qn_ref, kvn_ref, c_ref, s1_ref, s2_ref, cq_ref, ckv_ref, kr_ref):
    proj = _dot(a_ref[...], w_ref[...])
    cq_ref[...] = _rms(proj[:, :Q_LORA_RANK], qn_ref[...]).astype(BF16)
    ckv_ref[...] = _rms(proj[:, Q_LORA_RANK:Q_LORA_RANK + KV_LORA_RANK], kvn_ref[...]).astype(BF16)
    kr = proj[:, Q_LORA_RANK + KV_LORA_RANK:]
    kr_ref[...] = _rope(kr, c_ref[...], s1_ref[...], s2_ref[...]).astype(BF16)


def _mla_in(a, w_pad, q_norm, kv_norm, tabs, tm=512):
    s, d = a.shape
    n = w_pad.shape[1]
    row = lambda w: pl.BlockSpec((tm, w), lambda i: (i, 0))
    const = lambda r, w: pl.BlockSpec((r, w), lambda i: (0, 0))
    return pl.pallas_call(
        _mla_in_kernel,
        grid=(s // tm,),
        in_specs=[row(d), const(d, n), const(1, Q_LORA_RANK), const(1, KV_LORA_RANK),
                  row(LANES), row(LANES), row(LANES)],
        out_specs=[row(Q_LORA_RANK), row(KV_LORA_RANK), row(LANES)],
        out_shape=[jax.ShapeDtypeStruct((s, Q_LORA_RANK), BF16),
                   jax.ShapeDtypeStruct((s, KV_LORA_RANK), BF16),
                   jax.ShapeDtypeStruct((s, LANES), BF16)],
        compiler_params=_params(1),
        name="mla_in",
    )(a, w_pad, q_norm[None, :], kv_norm[None, :], *tabs)


def _mla_q_kernel(cq_ref, w_ref, c_ref, s1_ref, s2_ref, q_ref, *, heads, scale):
    acc = _dot(cq_ref[...], w_ref[...])
    c, s1, s2 = c_ref[...], s1_ref[...], s2_ref[...]
    for h in range(heads):
        lo = h * QK_PAD_DIM
        q_ref[:, lo:lo + LANES] = (acc[:, lo:lo + LANES] * scale).astype(BF16)
        rp = _rope(acc[:, lo + LANES:lo + QK_PAD_DIM], c, s1, s2)
        q_ref[:, lo + LANES:lo + QK_PAD_DIM] = (rp * scale).astype(BF16)


def _mla_q(cq, w_uq_pad, tabs, scale, tm=1024, heads_per_tile=4):
    s, r = cq.shape
    n = w_uq_pad.shape[1]
    tn = heads_per_tile * QK_PAD_DIM
    row = pl.BlockSpec((tm, LANES), lambda i, j: (i, 0))
    return pl.pallas_call(
        functools.partial(_mla_q_kernel, heads=heads_per_tile, scale=scale),
        grid=(s // tm, n // tn),
        in_specs=[pl.BlockSpec((tm, r), lambda i, j: (i, 0)), pl.BlockSpec((r, tn), lambda i, j: (0, j)),
                  row, row, row],
        out_specs=pl.BlockSpec((tm, tn), lambda i, j: (i, j)),
        out_shape=jax.ShapeDtypeStruct((s, n), BF16),
        compiler_params=_params(2),
        name="mla_q",
    )(cq, w_uq_pad, *tabs)


def _mla_kv_kernel(ckv_ref, wk_ref, wv_ref, kr_ref, k_ref, v_ref, *, heads):
    ckv = ckv_ref[...]
    kn = _dot(ckv, wk_ref[...])
    v_ref[...] = _dot(ckv, wv_ref[...]).astype(BF16)
    kr = kr_ref[...]
    for h in range(heads):
        k_ref[:, h * QK_PAD_DIM:h * QK_PAD_DIM + LANES] = kn[:, h * LANES:(h + 1) * LANES].astype(BF16)
        k_ref[:, h * QK_PAD_DIM + LANES:(h + 1) * QK_PAD_DIM] = kr


def _mla_kv(ckv, w_uk, w_uv, kr, tm=1024, heads_per_tile=4):
    s, r = ckv.shape
    n = w_uk.shape[1]
    tn = heads_per_tile * LANES
    return pl.pallas_call(
        functools.partial(_mla_kv_kernel, heads=heads_per_tile),
        grid=(s // tm, n // tn),
        in_specs=[pl.BlockSpec((tm, r), lambda i, j: (i, 0)),
                  pl.BlockSpec((r, tn), lambda i, j: (0, j)),
                  pl.BlockSpec((r, tn), lambda i, j: (0, j)),
                  pl.BlockSpec((tm, LANES), lambda i, j: (i, 0))],
        out_specs=[pl.BlockSpec((tm, 2 * tn), lambda i, j: (i, j)),
                   pl.BlockSpec((tm, tn), lambda i, j: (i, j))],
        out_shape=[jax.ShapeDtypeStruct((s, 2 * n), BF16), jax.ShapeDtypeStruct((s, n), BF16)],
        compiler_params=_params(2),
        name="mla_kv",
    )(ckv, w_uk, w_uv, kr)


def _mla_attn_kernel(q_ref, k_ref, v_ref, o_ref, m_sc, l_sc, acc_sc, *, tq, tk, chains):
    i = pl.program_id(1)
    m_sc[...] = jnp.full(m_sc.shape, -jnp.inf, F32)
    l_sc[...] = jnp.zeros(l_sc.shape, F32)
    acc_sc[...] = jnp.zeros(acc_sc.shape, F32)
    rep = tk // LANES
    tc = tq // chains

    def step(j, diag):
        start = pl.multiple_of(j * tk, tk)
        k = k_ref[pl.ds(start, tk), :]
        v = v_ref[pl.ds(start, tk), :]
        for c in range(chains):
            if diag is not None and (c + 1) * tc <= diag * tk:
                continue
            rows = pl.ds(c * tc, tc)
            s = _dot_nt(q_ref[rows, :], k)
            if diag is not None and (diag + 1) * tk - 1 > c * tc:
                row = c * tc + lax.broadcasted_iota(jnp.int32, (tc, tk), 0)
                col = diag * tk + lax.broadcasted_iota(jnp.int32, (tc, tk), 1)
                s = jnp.where(col <= row, s, -jnp.inf)
            m_prev = m_sc[rows, :]
            m_next = jnp.maximum(m_prev, jnp.max(s, axis=1, keepdims=True))
            p = jnp.exp2(s - jnp.tile(m_next, (1, rep)))
            alpha = jnp.exp2(m_prev - m_next)
            l_sc[rows, :] = alpha * l_sc[rows, :] + jnp.sum(p, axis=1, keepdims=True)
            acc_sc[rows, :] = alpha * acc_sc[rows, :] + _dot(p.astype(BF16), v)
            m_sc[rows, :] = m_next

    def body(j, carry):
        step(j, None)
        return carry

    lax.fori_loop(0, i * (tq // tk), body, 0)
    for d in range(tq // tk):
        step(i * (tq // tk) + d, d)
    o_ref[...] = (acc_sc[...] / l_sc[...]).astype(o_ref.dtype)


def _mla_attn(q, k, v, tq=1024, tk=512, chains=2):
    s = q.shape[0]
    h = MLA_HEADS
    return pl.pallas_call(
        functools.partial(_mla_attn_kernel, tq=tq, tk=tk, chains=chains),
        grid=(h, s // tq),
        in_specs=[pl.BlockSpec((tq, QK_PAD_DIM), lambda hh, i: (i, hh)),
                  pl.BlockSpec((s, QK_PAD_DIM), lambda hh, i: (0, hh)),
                  pl.BlockSpec((s, V_HEAD_DIM), lambda hh, i: (0, hh))],
        out_specs=pl.BlockSpec((tq, V_HEAD_DIM), lambda hh, i: (i, hh)),
        out_shape=jax.ShapeDtypeStruct((s, h * V_HEAD_DIM), BF16),
        scratch_shapes=[pltpu.VMEM((tq, LANES), F32), pltpu.VMEM((tq, LANES), F32),
                        pltpu.VMEM((tq, V_HEAD_DIM), F32)],
        compiler_params=_params(2),
        name="mla_attn",
    )(q, k, v)


def _sb_attn_kernel(q_ref, k_ref, v_ref, o_ref, carry_sc, acc_sc, *, t):
    n_tiles = q_ref.shape[0] // t
    rep = t // LANES
    row = lax.broadcasted_iota(jnp.int32, (t, t), 0)
    col = lax.broadcasted_iota(jnp.int32, (t, t), 1)
    strict = col < row
    tri = (row > col).astype(BF16)

    def step(q, j, masked):
        start = pl.multiple_of(j * t, t)
        z = _dot_nt(q, k_ref[pl.ds(start, t), :])
        soft = jnp.log1p(jnp.exp(-jnp.abs(z)))
        log_om = -(jnp.maximum(z, 0.0) + soft)
        log_beta = jnp.minimum(z, 0.0) - soft
        if masked:
            log_om = jnp.where(strict, log_om, 0.0)
        after = _dot(log_om.astype(BF16), tri) + jnp.tile(carry_sc[...], (1, rep))
        a = jnp.exp(log_beta + after)
        if masked:
            a = jnp.where(strict, a, 0.0)
        acc_sc[...] += _dot(a.astype(BF16), v_ref[pl.ds(start, t), :])
        carry_sc[...] += jnp.sum(log_om, axis=1, keepdims=True)

    def q_tile(i, _):
        rows = pl.ds(pl.multiple_of(i * t, t), t)
        q = q_ref[rows, :]
        carry_sc[...] = jnp.zeros(carry_sc.shape, F32)
        acc_sc[...] = jnp.zeros(acc_sc.shape, F32)
        step(q, i, True)

        def more(state):
            j, carry_max = state
            return (j >= 0) & (carry_max > SB_LOG_WEIGHT_FLOOR)

        def body(state):
            j, _ = state
            step(q, j, False)
            return j - 1, jnp.max(carry_sc[...])

        lax.while_loop(more, body, (i - 1, jnp.float32(0.0)))
        o_ref[rows, :] = acc_sc[...].astype(o_ref.dtype)
        return 0

    lax.fori_loop(0, n_tiles, q_tile, 0)


def _sb_attn(qkv, t=256):
    s = qkv.shape[0]
    h, d = SB_HEADS, SB_HEAD_DIM
    return pl.pallas_call(
        functools.partial(_sb_attn_kernel, t=t),
        grid=(h,),
        in_specs=[pl.BlockSpec((s, d), lambda hh: (0, hh)),
                  pl.BlockSpec((s, d), lambda hh: (0, h + hh)),
                  pl.BlockSpec((s, d), lambda hh: (0, 2 * h + hh))],
        out_specs=pl.BlockSpec((s, d), lambda hh: (0, hh)),
        out_shape=jax.ShapeDtypeStruct((s, h * d), BF16),
        scratch_shapes=[pltpu.VMEM((t, LANES), F32), pltpu.VMEM((t, d), F32)],
        compiler_params=_params(1),
        name="sb_attn",
    )(qkv, qkv, qkv)


def _matmul_scale_kernel(a_ref, w_ref, cs_ref, o_ref):
    o_ref[...] = (_dot(a_ref[...], w_ref[...]) * cs_ref[...]).astype(o_ref.dtype)


def _matmul_scale(a, w, col_scale, tm=1024, tn=1024):
    s, k = a.shape
    n = w.shape[1]
    return pl.pallas_call(
        _matmul_scale_kernel,
        grid=(s // tm, n // tn),
        in_specs=[pl.BlockSpec((tm, k), lambda i, j: (i, 0)), pl.BlockSpec((k, tn), lambda i, j: (0, j)),
                  pl.BlockSpec((1, tn), lambda i, j: (0, j))],
        out_specs=pl.BlockSpec((tm, tn), lambda i, j: (i, j)),
        out_shape=jax.ShapeDtypeStruct((s, n), BF16),
        compiler_params=_params(2),
        name="matmul_scale",
    )(a, w, col_scale[None, :])


def _matmul_residual_kernel(a_ref, w_ref, r_ref, o_ref):
    o_ref[...] = r_ref[...] + _dot(a_ref[...], w_ref[...])


def _matmul_residual(a, w, res, tm=512, tn=512):
    s, k = a.shape
    n = w.shape[1]
    return pl.pallas_call(
        _matmul_residual_kernel,
        grid=(s // tm, n // tn),
        in_specs=[pl.BlockSpec((tm, k), lambda i, j: (i, 0)), pl.BlockSpec((k, tn), lambda i, j: (0, j)),
                  pl.BlockSpec((tm, tn), lambda i, j: (i, j))],
        out_specs=pl.BlockSpec((tm, tn), lambda i, j: (i, j)),
        out_shape=jax.ShapeDtypeStruct((s, n), F32),
        compiler_params=_params(2),
        name="matmul_residual",
    )(a, w, res)


def _ffn_up_kernel(a_ref, wg_ref, wu_ref, o_ref):
    a = a_ref[...]
    g = _dot(a, wg_ref[...])
    u = _dot(a, wu_ref[...])
    o_ref[...] = (g * jax.nn.sigmoid(g) * u).astype(o_ref.dtype)


def _ffn_up(a, w_gate_up, tm=1024, tn=512):
    s, k = a.shape
    d_ff = w_gate_up.shape[1] // 2
    nj = d_ff // tn
    return pl.pallas_call(
        _ffn_up_kernel,
        grid=(s // tm, nj),
        in_specs=[pl.BlockSpec((tm, k), lambda i, j: (i, 0)),
                  pl.BlockSpec((k, tn), lambda i, j: (0, j)),
                  pl.BlockSpec((k, tn), lambda i, j: (0, nj + j))],
        out_specs=pl.BlockSpec((tm, tn), lambda i, j: (i, j)),
        out_shape=jax.ShapeDtypeStruct((s, d_ff), BF16),
        compiler_params=_params(2),
        name="ffn_up",
    )(a, w_gate_up, w_gate_up)


def _ffn(h, norm_g, w_gate_up, w_down):
    f = _rmsnorm(h, norm_g, BF16)
    act = _ffn_up(f, w_gate_up.astype(BF16))
    return _matmul_residual(act, w_down.astype(BF16), h)


def kernel(x, positions, attn_norm, mla_w_in, mla_q_norm, mla_kv_norm, mla_w_uq, mla_w_ukv, mla_w_o,
           sb_w_in, sb_w_o, ffn_norm, ffn_w_gate_up, ffn_w_down, final_norm):
    b, s, d = x.shape
    assert b == 1 and d == D_MODEL
    h = x.reshape(s, d)

    tabs = _rope_tables(positions.reshape(s, 1))
    w_in_pad = jnp.pad(mla_w_in[0], ((0, 0), (0, LANES - QK_ROPE_DIM))).astype(BF16)
    qk_dim = QK_NOPE_DIM + QK_ROPE_DIM
    w_uq_pad = jnp.pad(mla_w_uq[0].reshape(Q_LORA_RANK, MLA_HEADS, qk_dim),
                       ((0, 0), (0, 0), (0, QK_PAD_DIM - qk_dim)))
    w_uq_pad = w_uq_pad.reshape(Q_LORA_RANK, MLA_HEADS * QK_PAD_DIM).astype(BF16)
    w_ukv = mla_w_ukv[0].reshape(KV_LORA_RANK, MLA_HEADS, QK_NOPE_DIM + V_HEAD_DIM)
    w_uk = w_ukv[:, :, :QK_NOPE_DIM].reshape(KV_LORA_RANK, MLA_HEADS * QK_NOPE_DIM).astype(BF16)
    w_uv = w_ukv[:, :, QK_NOPE_DIM:].reshape(KV_LORA_RANK, MLA_HEADS * V_HEAD_DIM).astype(BF16)

    a = _rmsnorm(h, attn_norm[0], BF16)
    cq, ckv, kr = _mla_in(a, w_in_pad, mla_q_norm[0], mla_kv_norm[0], tabs)
    q = _mla_q(cq, w_uq_pad, tabs, qk_dim ** -0.5 * LOG2_E)
    k, v = _mla_kv(ckv, w_uk, w_uv, kr)
    o = _mla_attn(q, k, v)
    h = _matmul_residual(o, mla_w_o[0].astype(BF16), h)
    h = _ffn(h, ffn_norm[0], ffn_w_gate_up[0], ffn_w_down[0])

    a = _rmsnorm(h, attn_norm[1], BF16)
    n_q = SB_HEADS * SB_HEAD_DIM
    col_scale = jnp.concatenate([jnp.full((n_q,), SB_HEAD_DIM ** -0.5, F32), jnp.ones((2 * n_q,), F32)])
    qkv = _matmul_scale(a, sb_w_in[0].astype(BF16), col_scale)
    o = _sb_attn(qkv)
    h = _matmul_residual(o, sb_w_o[0].astype(BF16), h)
    h = _ffn(h, ffn_norm[1], ffn_w_gate_up[1], ffn_w_down[1])

    return _rmsnorm(h, final_norm, F32).reshape(b, s, d)
```

```python
import functools

import jax
import jax.numpy as jnp
from jax import lax
from jax.experimental import pallas as pl
from jax.experimental.pallas import tpu as pltpu

D_MODEL = 2048
MLA_HEADS = 16
Q_LORA_RANK = 512
KV_LORA_RANK = 512
QK_NOPE_DIM = 128
QK_ROPE_DIM = 64
V_HEAD_DIM = 128
ROPE_THETA = 10000.0
SB_HEADS = 16
SB_HEAD_DIM = 128
RMS_EPS = 1e-6

LANES = 128
QK_PAD_DIM = 2 * LANES
VMEM_LIMIT_BYTES = 56 * 1024 * 1024

LOG2_E = 1.4426950408889634
SB_LOG2_WEIGHT_FLOOR = -120.0 * LOG2_E

F32 = jnp.float32
BF16 = jnp.bfloat16


def _params(n_axes):
    return pltpu.CompilerParams(
        dimension_semantics=("arbitrary",) * n_axes, vmem_limit_bytes=VMEM_LIMIT_BYTES)


def _dot(a, b):
    return jnp.dot(a, b, preferred_element_type=F32)


def _dot_nt(a, b):
    return lax.dot_general(a, b, (((1,), (1,)), ((), ())), preferred_element_type=F32)


def _rms(x, g):
    return x * lax.rsqrt(jnp.mean(x * x, axis=-1, keepdims=True) + RMS_EPS) * g


def _rope(t, c, s1, s2):
    half = QK_ROPE_DIM // 2
    return t * c + pltpu.roll(t, half, 1) * s1 + pltpu.roll(t, LANES - half, 1) * s2


def _rope_tables_kernel(pos_ref, freq_ref, c_ref, s1_ref, s2_ref):
    half = QK_ROPE_DIM // 2
    ang = pos_ref[...].astype(F32) * freq_ref[...]
    lane = lax.broadcasted_iota(jnp.int32, ang.shape, 1)
    cos, sin = jnp.cos(ang), jnp.sin(ang)
    c_ref[...] = jnp.where(lane < 2 * half, cos, 0.0)
    s1_ref[...] = jnp.where((lane >= half) & (lane < 2 * half), sin, 0.0)
    s2_ref[...] = jnp.where(lane < half, -sin, 0.0)


def _rope_tables(positions, tm=2048):
    s = positions.shape[0]
    inv_freq = ROPE_THETA ** (-jnp.arange(0, QK_ROPE_DIM, 2, dtype=F32) / QK_ROPE_DIM)
    freq = jnp.concatenate([inv_freq, inv_freq, jnp.zeros((LANES - QK_ROPE_DIM,), F32)])[None, :]
    out = jax.ShapeDtypeStruct((s, LANES), F32)
    row = pl.BlockSpec((tm, LANES), lambda i: (i, 0))
    return pl.pallas_call(
        _rope_tables_kernel,
        grid=(s // tm,),
        in_specs=[pl.BlockSpec((tm, 1), lambda i: (i, 0)), pl.BlockSpec((1, LANES), lambda i: (0, 0))],
        out_specs=[row, row, row],
        out_shape=[out, out, out],
        compiler_params=_params(1),
        name="rope_tables",
    )(positions, freq)


def _norm_kernel(x_ref, g_ref, o_ref):
    o_ref[...] = _rms(x_ref[...], g_ref[...]).astype(o_ref.dtype)


def _rmsnorm(x, g, out_dtype, tm=512):
    s, d = x.shape
    return pl.pallas_call(
        _norm_kernel,
        grid=(s // tm,),
        in_specs=[pl.BlockSpec((tm, d), lambda i: (i, 0)), pl.BlockSpec((1, d), lambda i: (0, 0))],
        out_specs=pl.BlockSpec((tm, d), lambda i: (i, 0)),
        out_shape=jax.ShapeDtypeStruct((s, d), out_dtype),
        compiler_params=_params(1),
        name="rmsnorm",
    )(x, g[None, :])


def _mla_in_kernel(h_ref, g_ref, w_ref, qn_ref, kvn_ref, c_ref, s1_ref, s2_ref, cq_ref, ckv_ref, kr_ref):
    a = _rms(h_ref[...], g_ref[...]).astype(BF16)
    proj = _dot(a, w_ref[...])
    cq_ref[...] = _rms(proj[:, :Q_LORA_RANK], qn_ref[...]).astype(BF16)
    ckv_ref[...] = _rms(proj[:, Q_LORA_RANK:Q_LORA_RANK + KV_LORA_RANK], kvn_ref[...]).astype(BF16)
    kr = proj[:, Q_LORA_RANK + KV_LORA_RANK:]
    kr_ref[...] = _rope(kr, c_ref[...], s1_ref[...], s2_ref[...]).astype(BF16)


def _mla_in(h, norm_g, w_pad, q_norm, kv_norm, tabs, tm=512):
    s, d = h.shape
    n = w_pad.shape[1]
    row = lambda w: pl.BlockSpec((tm, w), lambda i: (i, 0))
    const = lambda r, w: pl.BlockSpec((r, w), lambda i: (0, 0))
    return pl.pallas_call(
        _mla_in_kernel,
        grid=(s // tm,),
        in_specs=[row(d), const(1, d), const(d, n), const(1, Q_LORA_RANK), const(1, KV_LORA_RANK),
                  row(LANES), row(LANES), row(LANES)],
        out_specs=[row(Q_LORA_RANK), row(KV_LORA_RANK), row(LANES)],
        out_shape=[jax.ShapeDtypeStruct((s, Q_LORA_RANK), BF16),
                   jax.ShapeDtypeStruct((s, KV_LORA_RANK), BF16),
                   jax.ShapeDtypeStruct((s, LANES), BF16)],
        compiler_params=_params(1),
        name="mla_in",
    )(h, norm_g[None, :], w_pad, q_norm[None, :], kv_norm[None, :], *tabs)


def _mla_q_kernel(cq_ref, w_ref, c_ref, s1_ref, s2_ref, q_ref, *, heads, scale):
    acc = _dot(cq_ref[...], w_ref[...])
    c, s1, s2 = c_ref[...], s1_ref[...], s2_ref[...]
    for h in range(heads):
        lo = h * QK_PAD_DIM
        q_ref[:, lo:lo + LANES] = (acc[:, lo:lo + LANES] * scale).astype(BF16)
        rp = _rope(acc[:, lo + LANES:lo + QK_PAD_DIM], c, s1, s2)
        q_ref[:, lo + LANES:lo + QK_PAD_DIM] = (rp * scale).astype(BF16)


def _mla_q(cq, w_uq_pad, tabs, scale, tm=1024, heads_per_tile=4):
    s, r = cq.shape
    n = w_uq_pad.shape[1]
    tn = heads_per_tile * QK_PAD_DIM
    row = pl.BlockSpec((tm, LANES), lambda i, j: (i, 0))
    return pl.pallas_call(
        functools.partial(_mla_q_kernel, heads=heads_per_tile, scale=scale),
        grid=(s // tm, n // tn),
        in_specs=[pl.BlockSpec((tm, r), lambda i, j: (i, 0)), pl.BlockSpec((r, tn), lambda i, j: (0, j)),
                  row, row, row],
        out_specs=pl.BlockSpec((tm, tn), lambda i, j: (i, j)),
        out_shape=jax.ShapeDtypeStruct((s, n), BF16),
        compiler_params=_params(2),
        name="mla_q",
    )(cq, w_uq_pad, *tabs)


def _mla_kv_kernel(ckv_ref, wk_ref, wv_ref, kr_ref, k_ref, v_ref, *, heads):
    ckv = ckv_ref[...]
    kn = _dot(ckv, wk_ref[...])
    v_ref[...] = _dot(ckv, wv_ref[...]).astype(BF16)
    kr = kr_ref[...]
    for h in range(heads):
        k_ref[:, h * QK_PAD_DIM:h * QK_PAD_DIM + LANES] = kn[:, h * LANES:(h + 1) * LANES].astype(BF16)
        k_ref[:, h * QK_PAD_DIM + LANES:(h + 1) * QK_PAD_DIM] = kr


def _mla_kv(ckv, w_uk, w_uv, kr, tm=1024, heads_per_tile=4):
    s, r = ckv.shape
    n = w_uk.shape[1]
    tn = heads_per_tile * LANES
    return pl.pallas_call(
        functools.partial(_mla_kv_kernel, heads=heads_per_tile),
        grid=(s // tm, n // tn),
        in_specs=[pl.BlockSpec((tm, r), lambda i, j: (i, 0)),
                  pl.BlockSpec((r, tn), lambda i, j: (0, j)),
                  pl.BlockSpec((r, tn), lambda i, j: (0, j)),
                  pl.BlockSpec((tm, LANES), lambda i, j: (i, 0))],
        out_specs=[pl.BlockSpec((tm, 2 * tn), lambda i, j: (i, j)),
                   pl.BlockSpec((tm, tn), lambda i, j: (i, j))],
        out_shape=[jax.ShapeDtypeStruct((s, 2 * n), BF16), jax.ShapeDtypeStruct((s, n), BF16)],
        compiler_params=_params(2),
        name="mla_kv",
    )(ckv, w_uk, w_uv, kr)


def _mla_attn_kernel(q_ref, k_ref, v_ref, o_ref, m_sc, acc_sc, s_even, s_odd, *, t, unroll):
    i = pl.program_id(1)
    m_sc[...] = jnp.full(m_sc.shape, -jnp.inf, F32)
    acc_sc[...] = jnp.zeros(acc_sc.shape, F32)
    rep = t // LANES
    q = q_ref[...]

    def scores(j):
        return _dot_nt(q, k_ref[pl.ds(pl.multiple_of(j * t, t), t), :])

    def softmax_pv(s_ref, j, diagonal):
        def load():
            s = s_ref[...]
            if diagonal:
                row = lax.broadcasted_iota(jnp.int32, (t, t), 0)
                col = lax.broadcasted_iota(jnp.int32, (t, t), 1)
                s = jnp.where(col <= row, s, -jnp.inf)
            return s
        m_prev = m_sc[...]
        m_next = jnp.maximum(m_prev, jnp.max(load(), axis=1, keepdims=True))
        m_sc[...] = m_next
        p = jnp.exp2(load() - jnp.tile(m_sc[...], (1, rep))).astype(BF16)
        alpha = jnp.exp2(m_prev - m_next)
        v = v_ref[pl.ds(pl.multiple_of(j * t, t), t), :]
        v_ones = jnp.concatenate([v, jnp.ones((t, LANES), BF16)], axis=1)
        acc_sc[...] = jnp.tile(alpha, (1, 2)) * acc_sc[...] + _dot(p, v_ones)

    s_even[...] = scores(0)

    def pairs(first, n_pairs):
        for d in range(n_pairs):
            j = first + 2 * d
            s_odd[...] = scores(j + 1)
            softmax_pv(s_even, j, False)
            s_even[...] = scores(j + 2)
            softmax_pv(s_odd, j + 1, False)

    def group(g, carry):
        pairs(g * (2 * unroll), unroll)
        return carry

    def pair(p, carry):
        pairs(p * 2, 1)
        return carry

    n_groups = i // (2 * unroll)
    lax.fori_loop(0, n_groups, group, 0)
    lax.fori_loop(n_groups * unroll, i // 2, pair, 0)

    @pl.when(i % 2 == 0)
    def _():
        softmax_pv(s_even, i, True)

    @pl.when(i % 2 == 1)
    def _():
        s_odd[...] = scores(i)
        softmax_pv(s_even, i - 1, False)
        softmax_pv(s_odd, i, True)

    o_ref[...] = (acc_sc[:, :V_HEAD_DIM] / acc_sc[:, V_HEAD_DIM:]).astype(o_ref.dtype)


def _mla_attn(q, k, v, t=512, unroll=4):
    s = q.shape[0]
    h = MLA_HEADS
    return pl.pallas_call(
        functools.partial(_mla_attn_kernel, t=t, unroll=unroll),
        grid=(h, s // t),
        in_specs=[pl.BlockSpec((t, QK_PAD_DIM), lambda hh, i: (i, hh)),
                  pl.BlockSpec((s, QK_PAD_DIM), lambda hh, i: (0, hh)),
                  pl.BlockSpec((s, V_HEAD_DIM), lambda hh, i: (0, hh))],
        out_specs=pl.BlockSpec((t, V_HEAD_DIM), lambda hh, i: (i, hh)),
        out_shape=jax.ShapeDtypeStruct((s, h * V_HEAD_DIM), BF16),
        scratch_shapes=[pltpu.VMEM((t, LANES), F32),
                        pltpu.VMEM((t, V_HEAD_DIM + LANES), F32),
                        pltpu.VMEM((t, t), F32), pltpu.VMEM((t, t), F32)],
        compiler_params=_params(2),
        name="mla_attn",
    )(q, k, v)


def _sb_attn_kernel(q_ref, k_ref, v_ref, o_ref, tri_sc, carry_sc, acc_sc, *, t, chains):
    n_tiles = q_ref.shape[0] // t
    w = 2 * t
    tri_sc[...] = (lax.broadcasted_iota(jnp.int32, (w, w), 0)
                   > lax.broadcasted_iota(jnp.int32, (w, w), 1)).astype(BF16)
    row = lax.broadcasted_iota(jnp.int32, (t, w), 0)
    col = lax.broadcasted_iota(jnp.int32, (t, w), 1)

    def log_probs(q, start, width):
        z = _dot_nt(q, k_ref[pl.ds(start, width), :])
        soft = jnp.log2(1.0 + jnp.exp2(-jnp.abs(z)))
        log_om = -(jnp.maximum(z, 0.0) + soft)
        log_beta = jnp.minimum(z, 0.0) - soft
        return log_om, log_beta

    def window(c, i):
        q = q_ref[pl.ds(pl.multiple_of(i * t, t), t), :]
        first = jnp.maximum(i - 1, 0)
        start = pl.multiple_of(first * t, t)
        valid = col < row + (i - first) * t
        log_om, log_beta = log_probs(q, start, w)
        log_om = jnp.where(valid, log_om, 0.0)
        after = _dot(log_om.astype(BF16), tri_sc[...])
        a = jnp.where(valid, jnp.exp2(log_beta + after), 0.0)
        acc_sc[c] = _dot(a.astype(BF16), v_ref[pl.ds(start, w), :])
        carry_sc[c] = jnp.broadcast_to(jnp.sum(log_om, axis=1, keepdims=True), (t, LANES))

    def earlier_tile(c, q, j):
        start = pl.multiple_of(j * t, t)
        log_om, log_beta = log_probs(q, start, t)
        after = _dot(log_om.astype(BF16), tri_sc[:t, :t]) + jnp.tile(carry_sc[c], (1, t // LANES))
        a = jnp.exp2(log_beta + after)
        acc_sc[c] +=_dot(a.astype(BF16), v_ref[pl.ds(start, t), :])
        carry_sc[c] += jnp.sum(log_om, axis=1, keepdims=True)

    def earlier_tiles(c, i):
        rows = pl.ds(pl.multiple_of(i * t, t), t)
        q = q_ref[rows, :]

        def more(state):
            j, carry_max = state
            return (j >= 0) & (carry_max > SB_LOG2_WEIGHT_FLOOR)

        def body(state):
            j, _ = state
            earlier_tile(c, q, j)
            return j - 1, jnp.max(carry_sc[c])

        lax.while_loop(more, body, (i - 2, jnp.max(carry_sc[c])))
        o_ref[rows, :] = acc_sc[c].astype(o_ref.dtype)

    def q_tiles(p, _):
        for c in range(chains):
            window(c, p * chains + c)
        for c in range(chains):
            earlier_tiles(c, p * chains + c)
        return 0

    lax.fori_loop(0, n_tiles // chains, q_tiles, 0)


def _sb_attn(qkv, t=256, chains=2):
    s = qkv.shape[0]
    h, d = SB_HEADS, SB_HEAD_DIM
    assert s % (t * chains) == 0 and s >= 2 * t
    return pl.pallas_call(
        functools.partial(_sb_attn_kernel, t=t, chains=chains),
        grid=(h,),
        in_specs=[pl.BlockSpec((s, d), lambda hh: (0, hh)),
                  pl.BlockSpec((s, d), lambda hh: (0, h + hh)),
                  pl.BlockSpec((s, d), lambda hh: (0, 2 * h + hh))],
        out_specs=pl.BlockSpec((s, d), lambda hh: (0, hh)),
        out_shape=jax.ShapeDtypeStruct((s, h * d), BF16),
        scratch_shapes=[pltpu.VMEM((2 * t, 2 * t), BF16), pltpu.VMEM((chains, t, LANES), F32),
                        pltpu.VMEM((chains, t, d), F32)],
        compiler_params=_params(1),
        name="sb_attn",
    )(qkv, qkv, qkv)


def _normalize_rows_once(h_ref, g_ref, a_sc):
    @pl.when(pl.program_id(1) == 0)
    def _():
        a_sc[...] = _rms(h_ref[...], g_ref[...]).astype(BF16)


def _norm_matmul_scale_kernel(h_ref, g_ref, w_ref, cs_ref, o_ref, a_sc):
    _normalize_rows_once(h_ref, g_ref, a_sc)
    o_ref[...] = (_dot(a_sc[...], w_ref[...]) * cs_ref[...]).astype(o_ref.dtype)


def _norm_matmul_scale(h, norm_g, w, col_scale, tm=1024, tn=1024):
    s, k = h.shape
    n = w.shape[1]
    return pl.pallas_call(
        _norm_matmul_scale_kernel,
        grid=(s // tm, n // tn),
        in_specs=[pl.BlockSpec((tm, k), lambda i, j: (i, 0)), pl.BlockSpec((1, k), lambda i, j: (0, 0)),
                  pl.BlockSpec((k, tn), lambda i, j: (0, j)), pl.BlockSpec((1, tn), lambda i, j: (0, j))],
        out_specs=pl.BlockSpec((tm, tn), lambda i, j: (i, j)),
        out_shape=jax.ShapeDtypeStruct((s, n), BF16),
        scratch_shapes=[pltpu.VMEM((tm, k), BF16)],
        compiler_params=_params(2),
        name="norm_matmul_scale",
    )(h, norm_g[None, :], w, col_scale[None, :])


def _matmul_residual_kernel(a_ref, w_ref, r_ref, o_ref):
    o_ref[...] = r_ref[...] + _dot(a_ref[...], w_ref[...])


def _matmul_residual(a, w, res, tm, tn):
    s, k = a.shape
    n = w.shape[1]
    return pl.pallas_call(
        _matmul_residual_kernel,
        grid=(s // tm, n // tn),
        in_specs=[pl.BlockSpec((tm, k), lambda i, j: (i, 0)), pl.BlockSpec((k, tn), lambda i, j: (0, j)),
                  pl.BlockSpec((tm, tn), lambda i, j: (i, j))],
        out_specs=pl.BlockSpec((tm, tn), lambda i, j: (i, j)),
        out_shape=jax.ShapeDtypeStruct((s, n), F32),
        compiler_params=_params(2),
        name="matmul_residual",
    )(a, w, res)


def _ffn_up_kernel(h_ref, g_ref, wg_ref, wu_ref, o_ref, a_sc):
    _normalize_rows_once(h_ref, g_ref, a_sc)
    a = a_sc[...]
    g = _dot(a, wg_ref[...])
    u = _dot(a, wu_ref[...])
    o_ref[...] = (g * jax.nn.sigmoid(g) * u).astype(o_ref.dtype)


def _ffn_up(h, norm_g, w_gate_up, tm=1024, tn=512):
    s, k = h.shape
    d_ff = w_gate_up.shape[1] // 2
    nj = d_ff // tn
    return pl.pallas_call(
        _ffn_up_kernel,
        grid=(s // tm, nj),
        in_specs=[pl.BlockSpec((tm, k), lambda i, j: (i, 0)), pl.BlockSpec((1, k), lambda i, j: (0, 0)),
                  pl.BlockSpec((k, tn), lambda i, j: (0, j)),
                  pl.BlockSpec((k, tn), lambda i, j: (0, nj + j))],
        out_specs=pl.BlockSpec((tm, tn), lambda i, j: (i, j)),
        out_shape=jax.ShapeDtypeStruct((s, d_ff), BF16),
        scratch_shapes=[pltpu.VMEM((tm, k), BF16)],
        compiler_params=_params(2),
        name="ffn_up",
    )(h, norm_g[None, :], w_gate_up, w_gate_up)


def _ffn(h, norm_g, w_gate_up, w_down):
    act = _ffn_up(h, norm_g, w_gate_up.astype(BF16))
    return _matmul_residual(act, w_down.astype(BF16), h, tm=1024, tn=512)


def kernel(x, positions, attn_norm, mla_w_in, mla_q_norm, mla_kv_norm, mla_w_uq, mla_w_ukv, mla_w_o,
           sb_w_in, sb_w_o, ffn_norm, ffn_w_gate_up, ffn_w_down, final_norm):
    b, s, d = x.shape
    assert b == 1 and d == D_MODEL
    h = x.reshape(s, d)

    tabs = _rope_tables(positions.reshape(s, 1))
    w_in_pad = jnp.pad(mla_w_in[0], ((0, 0), (0, LANES - QK_ROPE_DIM))).astype(BF16)
    qk_dim = QK_NOPE_DIM + QK_ROPE_DIM
    w_uq_pad = jnp.pad(mla_w_uq[0].reshape(Q_LORA_RANK, MLA_HEADS, qk_dim),
                       ((0, 0), (0, 0), (0, QK_PAD_DIM - qk_dim)))
    w_uq_pad = w_uq_pad.reshape(Q_LORA_RANK, MLA_HEADS * QK_PAD_DIM).astype(BF16)
    w_ukv = mla_w_ukv[0].reshape(KV_LORA_RANK, MLA_HEADS, QK_NOPE_DIM + V_HEAD_DIM)
    w_uk = w_ukv[:, :, :QK_NOPE_DIM].reshape(KV_LORA_RANK, MLA_HEADS * QK_NOPE_DIM).astype(BF16)
    w_uv = w_ukv[:, :, QK_NOPE_DIM:].reshape(KV_LORA_RANK, MLA_HEADS * V_HEAD_DIM).astype(BF16)

    cq, ckv, kr = _mla_in(h, attn_norm[0], w_in_pad, mla_q_norm[0], mla_kv_norm[0], tabs)
    q = _mla_q(cq, w_uq_pad, tabs, qk_dim ** -0.5 * LOG2_E)
    k, v = _mla_kv(ckv, w_uk, w_uv, kr)
    o = _mla_attn(q, k, v)
    h = _matmul_residual(o, mla_w_o[0].astype(BF16), h, tm=1024, tn=1024)
    h = _ffn(h, ffn_norm[0], ffn_w_gate_up[0], ffn_w_down[0])

    n_q = SB_HEADS * SB_HEAD_DIM
    col_scale = jnp.concatenate([jnp.full((n_q,), SB_HEAD_DIM ** -0.5 * LOG2_E, F32), jnp.ones((2 * n_q,), F32)])
    qkv = _norm_matmul_scale(h, attn_norm[1], sb_w_in[0].astype(BF16), col_scale)
    o = _sb_attn(qkv)
    h = _matmul_residual(o, sb_w_o[0].astype(BF16), h, tm=1024, tn=1024)
    h = _ffn(h, ffn_norm[1], ffn_w_gate_up[1], ffn_w_down[1])

    return _rmsnorm(h, final_norm, F32).reshape(b, s, d)
```

```python
import functools

import jax
import jax.numpy as jnp
from jax import lax
from jax.experimental import pallas as pl
from jax.experimental.pallas import tpu as pltpu

D_MODEL = 2048
MLA_HEADS = 16
Q_LORA_RANK = 512
KV_LORA_RANK = 512
QK_NOPE_DIM = 128
QK_ROPE_DIM = 64
V_HEAD_DIM = 128
ROPE_THETA = 10000.0
SB_HEADS = 16
SB_HEAD_DIM = 128
RMS_EPS = 1e-6

LANES = 128
QK_PAD_DIM = 2 * LANES
VMEM_LIMIT_BYTES = 56 * 1024 * 1024

LOG2_E = 1.4426950408889634
SB_LOG2_WEIGHT_FLOOR = -120.0 * LOG2_E

F32 = jnp.float32
BF16 = jnp.bfloat16


def _params(n_axes):
    return pltpu.CompilerParams(
        dimension_semantics=("arbitrary",) * n_axes, vmem_limit_bytes=VMEM_LIMIT_BYTES)


def _dot(a, b):
    return jnp.dot(a, b, preferred_element_type=F32)


def _dot_nt(a, b):
    return lax.dot_general(a, b, (((1,), (1,)), ((), ())), preferred_element_type=F32)


def _rms(x, g):
    return x * lax.rsqrt(jnp.mean(x * x, axis=-1, keepdims=True) + RMS_EPS) * g


def _rope(t, c, s1, s2):
    half = QK_ROPE_DIM // 2
    return t * c + pltpu.roll(t, half, 1) * s1 + pltpu.roll(t, LANES - half, 1) * s2


def _rope_tables_kernel(pos_ref, freq_ref, c_ref, s1_ref, s2_ref):
    half = QK_ROPE_DIM // 2
    ang = pos_ref[...].astype(F32) * freq_ref[...]
    lane = lax.broadcasted_iota(jnp.int32, ang.shape, 1)
    cos, sin = jnp.cos(ang), jnp.sin(ang)
    c_ref[...] = jnp.where(lane < 2 * half, cos, 0.0)
    s1_ref[...] = jnp.where((lane >= half) & (lane < 2 * half), sin, 0.0)
    s2_ref[...] = jnp.where(lane < half, -sin, 0.0)


def _rope_tables(positions, tm=2048):
    s = positions.shape[0]
    inv_freq = ROPE_THETA ** (-jnp.arange(0, QK_ROPE_DIM, 2, dtype=F32) / QK_ROPE_DIM)
    freq = jnp.concatenate([inv_freq, inv_freq, jnp.zeros((LANES - QK_ROPE_DIM,), F32)])[None, :]
    out = jax.ShapeDtypeStruct((s, LANES), F32)
    row = pl.BlockSpec((tm, LANES), lambda i: (i, 0))
    return pl.pallas_call(
        _rope_tables_kernel,
        grid=(s // tm,),
        in_specs=[pl.BlockSpec((tm, 1), lambda i: (i, 0)), pl.BlockSpec((1, LANES), lambda i: (0, 0))],
        out_specs=[row, row, row],
        out_shape=[out, out, out],
        compiler_params=_params(1),
        name="rope_tables",
    )(positions, freq)


def _norm_kernel(x_ref, g_ref, o_ref):
    o_ref[...] = _rms(x_ref[...], g_ref[...]).astype(o_ref.dtype)


def _rmsnorm(x, g, out_dtype, tm=512):
    s, d = x.shape
    return pl.pallas_call(
        _norm_kernel,
        grid=(s // tm,),
        in_specs=[pl.BlockSpec((tm, d), lambda i: (i, 0)), pl.BlockSpec((1, d), lambda i: (0, 0))],
        out_specs=pl.BlockSpec((tm, d), lambda i: (i, 0)),
        out_shape=jax.ShapeDtypeStruct((s, d), out_dtype),
        compiler_params=_params(1),
        name="rmsnorm",
    )(x, g[None, :])


def _mla_in_kernel(h_ref, g_ref, w_ref, qn_ref, kvn_ref, c_ref, s1_ref, s2_ref, cq_ref, ckv_ref, kr_ref):
    a = _rms(h_ref[...], g_ref[...]).astype(BF16)
    proj = _dot(a, w_ref[...])
    cq_ref[...] = _rms(proj[:, :Q_LORA_RANK], qn_ref[...]).astype(BF16)
    ckv_ref[...] = _rms(proj[:, Q_LORA_RANK:Q_LORA_RANK + KV_LORA_RANK], kvn_ref[...]).astype(BF16)
    kr = proj[:, Q_LORA_RANK + KV_LORA_RANK:]
    kr_ref[...] = _rope(kr, c_ref[...], s1_ref[...], s2_ref[...]).astype(BF16)


def _mla_in(h, norm_g, w_pad, q_norm, kv_norm, tabs, tm=1024):
    s, d = h.shape
    n = w_pad.shape[1]
    row = lambda w: pl.BlockSpec((tm, w), lambda i: (i, 0))
    const = lambda r, w: pl.BlockSpec((r, w), lambda i: (0, 0))
    return pl.pallas_call(
        _mla_in_kernel,
        grid=(s // tm,),
        in_specs=[row(d), const(1, d), const(d, n), const(1, Q_LORA_RANK), const(1, KV_LORA_RANK),
                  row(LANES), row(LANES), row(LANES)],
        out_specs=[row(Q_LORA_RANK), row(KV_LORA_RANK), row(LANES)],
        out_shape=[jax.ShapeDtypeStruct((s, Q_LORA_RANK), BF16),
                   jax.ShapeDtypeStruct((s, KV_LORA_RANK), BF16),
                   jax.ShapeDtypeStruct((s, LANES), BF16)],
        compiler_params=_params(1),
        name="mla_in",
    )(h, norm_g[None, :], w_pad, q_norm[None, :], kv_norm[None, :], *tabs)


def _mla_q_kernel(cq_ref, w_ref, c_ref, s1_ref, s2_ref, q_ref, *, heads, scale):
    acc = _dot(cq_ref[...], w_ref[...])
    c, s1, s2 = c_ref[...], s1_ref[...], s2_ref[...]
    for h in range(heads):
        lo = h * QK_PAD_DIM
        q_ref[:, lo:lo + LANES] = (acc[:, lo:lo + LANES] * scale).astype(BF16)
        rp = _rope(acc[:, lo + LANES:lo + QK_PAD_DIM], c, s1, s2)
        q_ref[:, lo + LANES:lo + QK_PAD_DIM] = (rp * scale).astype(BF16)


def _mla_q(cq, w_uq_pad, tabs, scale, tm=1024, heads_per_tile=8):
    s, r = cq.shape
    n = w_uq_pad.shape[1]
    tn = heads_per_tile * QK_PAD_DIM
    row = pl.BlockSpec((tm, LANES), lambda i, j: (i, 0))
    return pl.pallas_call(
        functools.partial(_mla_q_kernel, heads=heads_per_tile, scale=scale),
        grid=(s // tm, n // tn),
        in_specs=[pl.BlockSpec((tm, r), lambda i, j: (i, 0)), pl.BlockSpec((r, tn), lambda i, j: (0, j)),
                  row, row, row],
        out_specs=pl.BlockSpec((tm, tn), lambda i, j: (i, j)),
        out_shape=jax.ShapeDtypeStruct((s, n), BF16),
        compiler_params=_params(2),
        name="mla_q",
    )(cq, w_uq_pad, *tabs)


def _mla_kv_kernel(ckv_ref, wk_ref, wv_ref, kr_ref, k_ref, v_ref, *, heads):
    ckv = ckv_ref[...]
    kn = _dot(ckv, wk_ref[...])
    v_ref[...] = _dot(ckv, wv_ref[...]).astype(BF16)
    kr = kr_ref[...]
    for h in range(heads):
        k_ref[:, h * QK_PAD_DIM:h * QK_PAD_DIM + LANES] = kn[:, h * LANES:(h + 1) * LANES].astype(BF16)
        k_ref[:, h * QK_PAD_DIM + LANES:(h + 1) * QK_PAD_DIM] = kr


def _mla_kv(ckv, w_uk, w_uv, kr, tm=1024, heads_per_tile=8):
    s, r = ckv.shape
    n = w_uk.shape[1]
    tn = heads_per_tile * LANES
    return pl.pallas_call(
        functools.partial(_mla_kv_kernel, heads=heads_per_tile),
        grid=(s // tm, n // tn),
        in_specs=[pl.BlockSpec((tm, r), lambda i, j: (i, 0)),
                  pl.BlockSpec((r, tn), lambda i, j: (0, j)),
                  pl.BlockSpec((r, tn), lambda i, j: (0, j)),
                  pl.BlockSpec((tm, LANES), lambda i, j: (i, 0))],
        out_specs=[pl.BlockSpec((tm, 2 * tn), lambda i, j: (i, j)),
                   pl.BlockSpec((tm, tn), lambda i, j: (i, j))],
        out_shape=[jax.ShapeDtypeStruct((s, 2 * n), BF16), jax.ShapeDtypeStruct((s, n), BF16)],
        compiler_params=_params(2),
        name="mla_kv",
    )(ckv, w_uk, w_uv, kr)


def _mla_attn_kernel(q_ref, k_ref, v_ref, o_ref, m_sc, acc_sc, s_even, s_odd, *, t, unrolls):
    i = pl.program_id(1)
    m_sc[...] = jnp.full(m_sc.shape, -jnp.inf, F32)
    acc_sc[...] = jnp.zeros(acc_sc.shape, F32)
    rep = t // LANES
    q = q_ref[...]

    def scores(j):
        return _dot_nt(q, k_ref[pl.ds(pl.multiple_of(j * t, t), t), :])

    def softmax_pv(s_ref, j, diagonal):
        def load():
            s = s_ref[...]
            if diagonal:
                row = lax.broadcasted_iota(jnp.int32, (t, t), 0)
                col = lax.broadcasted_iota(jnp.int32, (t, t), 1)
                s = jnp.where(col <= row, s, -jnp.inf)
            return s
        m_prev = m_sc[...]
        m_next = jnp.maximum(m_prev, jnp.max(load(), axis=1, keepdims=True))
        m_sc[...] = m_next
        p = jnp.exp2(load() - jnp.tile(m_sc[...], (1, rep))).astype(BF16)
        alpha = jnp.exp2(m_prev - m_next)
        v = v_ref[pl.ds(pl.multiple_of(j * t, t), t), :]
        v_ones = jnp.concatenate([v, jnp.ones((t, LANES), BF16)], axis=1)
        acc_sc[...] = jnp.tile(alpha, (1, 2)) * acc_sc[...] + _dot(p, v_ones)

    s_even[...] = scores(0)

    def pairs(first, n_pairs):
        for d in range(n_pairs):
            j = first + 2 * d
            s_odd[...] = scores(j + 1)
            softmax_pv(s_even, j, False)
            s_even[...] = scores(j + 2)
            softmax_pv(s_odd, j + 1, False)

    done = 0
    for u in unrolls:
        n_groups = (i // 2 - done) // u

        def group(g, carry, u=u, done=done):
            pairs(2 * (done + g * u), u)
            return carry

        lax.fori_loop(0, n_groups, group, 0)
        done = done + n_groups * u

    @pl.when(i % 2 == 0)
    def _():
        softmax_pv(s_even, i, True)

    @pl.when(i % 2 == 1)
    def _():
        s_odd[...] = scores(i)
        softmax_pv(s_even, i - 1, False)
        softmax_pv(s_odd, i, True)

    o_ref[...] = (acc_sc[:, :V_HEAD_DIM] / acc_sc[:, V_HEAD_DIM:]).astype(o_ref.dtype)


def _mla_attn(q, k, v, t=512, unrolls=(4, 2, 1)):
    s = q.shape[0]
    h = MLA_HEADS
    assert unrolls[-1] == 1
    return pl.pallas_call(
        functools.partial(_mla_attn_kernel, t=t, unrolls=unrolls),
        grid=(h, s // t),
        in_specs=[pl.BlockSpec((t, QK_PAD_DIM), lambda hh, i: (i, hh)),
                  pl.BlockSpec((s, QK_PAD_DIM), lambda hh, i: (0, hh)),
                  pl.BlockSpec((s, V_HEAD_DIM), lambda hh, i: (0, hh))],
        out_specs=pl.BlockSpec((t, V_HEAD_DIM), lambda hh, i: (i, hh)),
        out_shape=jax.ShapeDtypeStruct((s, h * V_HEAD_DIM), BF16),
        scratch_shapes=[pltpu.VMEM((t, LANES), F32),
                        pltpu.VMEM((t, V_HEAD_DIM + LANES), F32),
                        pltpu.VMEM((t, t), F32), pltpu.VMEM((t, t), F32)],
        compiler_params=_params(2),
        name="mla_attn",
    )(q, k, v)


def _sb_attn_kernel(q_ref, k_ref, v_ref, o_ref, tri_sc, carry_sc, acc_sc, *, t, chains):
    n_tiles = q_ref.shape[0] // t
    w = 2 * t
    tri_sc[...] = (lax.broadcasted_iota(jnp.int32, (w, w), 0)
                   > lax.broadcasted_iota(jnp.int32, (w, w), 1)).astype(BF16)
    row = lax.broadcasted_iota(jnp.int32, (t, w), 0)
    col = lax.broadcasted_iota(jnp.int32, (t, w), 1)

    def log_probs(q, start, width):
        z = _dot_nt(q, k_ref[pl.ds(start, width), :])
        soft = jnp.log2(1.0 + jnp.exp2(-jnp.abs(z)))
        log_om = -(jnp.maximum(z, 0.0) + soft)
        log_beta = log_om + z
        return log_om, log_beta

    def window(c, i):
        q = q_ref[pl.ds(pl.multiple_of(i * t, t), t), :]
        first = jnp.maximum(i - 1, 0)
        start = pl.multiple_of(first * t, t)
        valid = col < row + (i - first) * t
        log_om, log_beta = log_probs(q, start, w)
        log_om = jnp.where(valid, log_om, 0.0)
        after = _dot(log_om.astype(BF16), tri_sc[...])
        a = jnp.where(valid, jnp.exp2(log_beta + after), 0.0)
        acc_sc[c] = _dot(a.astype(BF16), v_ref[pl.ds(start, w), :])
        carry_sc[c] = jnp.broadcast_to(jnp.sum(log_om, axis=1, keepdims=True), (t, LANES))

    def earlier_tile(c, q, j):
        start = pl.multiple_of(j * t, t)
        log_om, log_beta = log_probs(q, start, t)
        after = _dot(log_om.astype(BF16), tri_sc[:t, :t]) + jnp.tile(carry_sc[c], (1, t // LANES))
        a = jnp.exp2(log_beta + after)
        acc_sc[c] += _dot(a.astype(BF16), v_ref[pl.ds(start, t), :])
        carry_sc[c] += jnp.sum(log_om, axis=1, keepdims=True)

    def earlier_tiles(c, i):
        rows = pl.ds(pl.multiple_of(i * t, t), t)
        q = q_ref[rows, :]

        def more(state):
            j, carry_max = state
            return (j >= 0) & (carry_max > SB_LOG2_WEIGHT_FLOOR)

        def body(state):
            j, _ = state
            earlier_tile(c, q, j)
            return j - 1, jnp.max(carry_sc[c])

        lax.while_loop(more, body, (i - 2, jnp.max(carry_sc[c])))
        o_ref[rows, :] = acc_sc[c].astype(o_ref.dtype)

    def q_tiles(p, _):
        for c in range(chains):
            window(c, p * chains + c)
        for c in range(chains):
            earlier_tiles(c, p * chains + c)
        return 0

    lax.fori_loop(0, n_tiles // chains, q_tiles, 0)


def _sb_attn(qkv, t=256, chains=2):
    s = qkv.shape[0]
    h, d = SB_HEADS, SB_HEAD_DIM
    assert s % (t * chains) == 0 and s >= 2 * t
    return pl.pallas_call(
        functools.partial(_sb_attn_kernel, t=t, chains=chains),
        grid=(h,),
        in_specs=[pl.BlockSpec((s, d), lambda hh: (0, hh)),
                  pl.BlockSpec((s, d), lambda hh: (0, h + hh)),
                  pl.BlockSpec((s, d), lambda hh: (0, 2 * h + hh))],
        out_specs=pl.BlockSpec((s, d), lambda hh: (0, hh)),
        out_shape=jax.ShapeDtypeStruct((s, h * d), BF16),
        scratch_shapes=[pltpu.VMEM((2 * t, 2 * t), BF16), pltpu.VMEM((chains, t, LANES), F32),
                        pltpu.VMEM((chains, t, d), F32)],
        compiler_params=_params(1),
        name="sb_attn",
    )(qkv, qkv, qkv)


def _normalize_rows_once(h_ref, g_ref, a_sc):
    @pl.when(pl.program_id(1) == 0)
    def _():
        a_sc[...] = _rms(h_ref[...], g_ref[...]).astype(BF16)


def _norm_matmul_scale_kernel(h_ref, g_ref, w_ref, cs_ref, o_ref, a_sc):
    _normalize_rows_once(h_ref, g_ref, a_sc)
    o_ref[...] = (_dot(a_sc[...], w_ref[...].astype(BF16)) * cs_ref[...]).astype(o_ref.dtype)


def _norm_matmul_scale(h, norm_g, w, layer, col_scale, tm=1024, tn=1024):
    s, k = h.shape
    n = w.shape[2]
    return pl.pallas_call(
        _norm_matmul_scale_kernel,
        grid=(s // tm, n // tn),
        in_specs=[pl.BlockSpec((tm, k), lambda i, j: (i, 0)), pl.BlockSpec((1, k), lambda i, j: (0, 0)),
                  pl.BlockSpec((None, k, tn), lambda i, j: (layer, 0, j)),
                  pl.BlockSpec((1, tn), lambda i, j: (0, j))],
        out_specs=pl.BlockSpec((tm, tn), lambda i, j: (i, j)),
        out_shape=jax.ShapeDtypeStruct((s, n), BF16),
        scratch_shapes=[pltpu.VMEM((tm, k), BF16)],
        compiler_params=_params(2),
        name="norm_matmul_scale",
    )(h, norm_g[None, :], w, col_scale[None, :])


def _matmul_residual_kernel(a_ref, w_ref, r_ref, o_ref):
    o_ref[...] = r_ref[...] + _dot(a_ref[...], w_ref[...].astype(BF16))


def _matmul_residual(a, w, layer, res, tm, tn):
    s, k = a.shape
    n = w.shape[2]
    return pl.pallas_call(
        _matmul_residual_kernel,
        grid=(s // tm, n // tn),
        in_specs=[pl.BlockSpec((tm, k), lambda i, j: (i, 0)),
                  pl.BlockSpec((None, k, tn), lambda i, j: (layer, 0, j)),
                  pl.BlockSpec((tm, tn), lambda i, j: (i, j))],
        out_specs=pl.BlockSpec((tm, tn), lambda i, j: (i, j)),
        out_shape=jax.ShapeDtypeStruct((s, n), F32),
        compiler_params=_params(2),
        name="matmul_residual",
    )(a, w, res)


def _ffn_up_kernel(h_ref, g_ref, wg_ref, wu_ref, o_ref, a_sc):
    _normalize_rows_once(h_ref, g_ref, a_sc)
    a = a_sc[...]
    g = _dot(a, wg_ref[...].astype(BF16))
    u = _dot(a, wu_ref[...].astype(BF16))
    o_ref[...] = (g * jax.nn.sigmoid(g) * u).astype(o_ref.dtype)


def _ffn_up(h, norm_g, w_gate_up, layer, tm=1024, tn=512):
    s, k = h.shape
    d_ff = w_gate_up.shape[2] // 2
    nj = d_ff // tn
    return pl.pallas_call(
        _ffn_up_kernel,
        grid=(s // tm, nj),
        in_specs=[pl.BlockSpec((tm, k), lambda i, j: (i, 0)), pl.BlockSpec((1, k), lambda i, j: (0, 0)),
                  pl.BlockSpec((None, k, tn), lambda i, j: (layer, 0, j)),
                  pl.BlockSpec((None, k, tn), lambda i, j: (layer, 0, nj + j))],
        out_specs=pl.BlockSpec((tm, tn), lambda i, j: (i, j)),
        out_shape=jax.ShapeDtypeStruct((s, d_ff), BF16),
        scratch_shapes=[pltpu.VMEM((tm, k), BF16)],
        compiler_params=_params(2),
        name="ffn_up",
    )(h, norm_g[None, :], w_gate_up, w_gate_up)


def _ffn(h, norm_g, w_gate_up, w_down, layer):
    act = _ffn_up(h, norm_g, w_gate_up, layer)
    return _matmul_residual(act, w_down, layer, h, tm=1024, tn=256)


def kernel(x, positions, attn_norm, mla_w_in, mla_q_norm, mla_kv_norm, mla_w_uq, mla_w_ukv, mla_w_o,
           sb_w_in, sb_w_o, ffn_norm, ffn_w_gate_up, ffn_w_down, final_norm):
    b, s, d = x.shape
    assert b == 1 and d == D_MODEL
    h = x.reshape(s, d)

    tabs = _rope_tables(positions.reshape(s, 1))
    w_in_pad = jnp.pad(mla_w_in[0], ((0, 0), (0, LANES - QK_ROPE_DIM))).astype(BF16)
    qk_dim = QK_NOPE_DIM + QK_ROPE_DIM
    w_uq_pad = jnp.pad(mla_w_uq[0].reshape(Q_LORA_RANK, MLA_HEADS, qk_dim),
                       ((0, 0), (0, 0), (0, QK_PAD_DIM - qk_dim)))
    w_uq_pad = w_uq_pad.reshape(Q_LORA_RANK, MLA_HEADS * QK_PAD_DIM).astype(BF16)
    w_ukv = mla_w_ukv[0].reshape(KV_LORA_RANK, MLA_HEADS, QK_NOPE_DIM + V_HEAD_DIM)
    w_uk = w_ukv[:, :, :QK_NOPE_DIM].reshape(KV_LORA_RANK, MLA_HEADS * QK_NOPE_DIM).astype(BF16)
    w_uv = w_ukv[:, :, QK_NOPE_DIM:].reshape(KV_LORA_RANK, MLA_HEADS * V_HEAD_DIM).astype(BF16)

    cq, ckv, kr = _mla_in(h, attn_norm[0], w_in_pad, mla_q_norm[0], mla_kv_norm[0], tabs)
    q = _mla_q(cq, w_uq_pad, tabs, qk_dim ** -0.5 * LOG2_E)
    k, v = _mla_kv(ckv, w_uk, w_uv, kr)
    o = _mla_attn(q, k, v)
    h = _matmul_residual(o, mla_w_o, 0, h, tm=1024, tn=1024)
    h = _ffn(h, ffn_norm[0], ffn_w_gate_up, ffn_w_down, 0)

    n_q = SB_HEADS * SB_HEAD_DIM
    col_scale = jnp.concatenate([jnp.full((n_q,), SB_HEAD_DIM ** -0.5 * LOG2_E, F32), jnp.ones((2 * n_q,), F32)])
    qkv = _norm_matmul_scale(h, attn_norm[1], sb_w_in, 0, col_scale)
    o = _sb_attn(qkv)
    h = _matmul_residual(o, sb_w_o, 0, h, tm=1024, tn=1024)
    h = _ffn(h, ffn_norm[1], ffn_w_gate_up, ffn_w_down, 1)

    return _rmsnorm(h, final_norm, F32).reshape(b, s, d)
```

```python
import functools

import jax
import jax.numpy as jnp
from jax import lax
from jax.experimental import pallas as pl
from jax.experimental.pallas import tpu as pltpu

D_MODEL = 2048
MLA_HEADS = 16
Q_LORA_RANK = 512
KV_LORA_RANK = 512
QK_NOPE_DIM = 128
QK_ROPE_DIM = 64
V_HEAD_DIM = 128
ROPE_THETA = 10000.0
SB_HEADS = 16
SB_HEAD_DIM = 128
RMS_EPS = 1e-6

LANES = 128
QK_PAD_DIM = 2 * LANES
VMEM_LIMIT_BYTES = 56 * 1024 * 1024

LOG2_E = 1.4426950408889634
SB_LOG2_WEIGHT_FLOOR = -120.0 * LOG2_E

F32 = jnp.float32
BF16 = jnp.bfloat16


def _params(n_axes):
    return pltpu.CompilerParams(
        dimension_semantics=("arbitrary",) * n_axes, vmem_limit_bytes=VMEM_LIMIT_BYTES)


def _dot(a, b):
    return jnp.dot(a, b, preferred_element_type=F32)


def _dot_nt(a, b):
    return lax.dot_general(a, b, (((1,), (1,)), ((), ())), preferred_element_type=F32)


def _rms(x, g):
    return x * lax.rsqrt(jnp.mean(x * x, axis=-1, keepdims=True) + RMS_EPS) * g


def _rope(t, c, s1, s2):
    half = QK_ROPE_DIM // 2
    return t * c + pltpu.roll(t, half, 1) * s1 + pltpu.roll(t, LANES - half, 1) * s2


def _rope_tables_kernel(pos_ref, freq_ref, c_ref, s1_ref, s2_ref):
    half = QK_ROPE_DIM // 2
    ang = pos_ref[...].astype(F32) * freq_ref[...]
    lane = lax.broadcasted_iota(jnp.int32, ang.shape, 1)
    cos, sin = jnp.cos(ang), jnp.sin(ang)
    c_ref[...] = jnp.where(lane < 2 * half, cos, 0.0)
    s1_ref[...] = jnp.where((lane >= half) & (lane < 2 * half), sin, 0.0)
    s2_ref[...] = jnp.where(lane < half, -sin, 0.0)


def _rope_tables(positions, tm=2048):
    s = positions.shape[0]
    inv_freq = ROPE_THETA ** (-jnp.arange(0, QK_ROPE_DIM, 2, dtype=F32) / QK_ROPE_DIM)
    freq = jnp.concatenate([inv_freq, inv_freq, jnp.zeros((LANES - QK_ROPE_DIM,), F32)])[None, :]
    out = jax.ShapeDtypeStruct((s, LANES), F32)
    row = pl.BlockSpec((tm, LANES), lambda i: (i, 0))
    return pl.pallas_call(
        _rope_tables_kernel,
        grid=(s // tm,),
        in_specs=[pl.BlockSpec((tm, 1), lambda i: (i, 0)), pl.BlockSpec((1, LANES), lambda i: (0, 0))],
        out_specs=[row, row, row],
        out_shape=[out, out, out],
        compiler_params=_params(1),
        name="rope_tables",
    )(positions, freq)


def _rope_tables_t_kernel(pos_ref, freq_ref, c_ref, s1_ref, s2_ref):
    half = QK_ROPE_DIM // 2
    ang = freq_ref[...] * pos_ref[...].astype(F32)
    r = lax.broadcasted_iota(jnp.int32, ang.shape, 0)
    cos, sin = jnp.cos(ang), jnp.sin(ang)
    c_ref[...] = jnp.where(r < 2 * half, cos, 0.0)
    s1_ref[...] = jnp.where((r >= half) & (r < 2 * half), sin, 0.0)
    s2_ref[...] = jnp.where(r < half, -sin, 0.0)


def _rope_tables_t(positions, tm=2048):
    s = positions.shape[1]
    inv_freq = ROPE_THETA ** (-jnp.arange(0, QK_ROPE_DIM, 2, dtype=F32) / QK_ROPE_DIM)
    freq = jnp.concatenate([inv_freq, inv_freq, jnp.zeros((LANES - QK_ROPE_DIM,), F32)])[:, None]
    out = jax.ShapeDtypeStruct((LANES, s), F32)
    col = pl.BlockSpec((LANES, tm), lambda i: (0, i))
    return pl.pallas_call(
        _rope_tables_t_kernel,
        grid=(s // tm,),
        in_specs=[pl.BlockSpec((1, tm), lambda i: (0, i)), pl.BlockSpec((LANES, 1), lambda i: (0, 0))],
        out_specs=[col, col, col],
        out_shape=[out, out, out],
        compiler_params=_params(1),
        name="rope_tables_t",
    )(positions, freq)


def _norm_kernel(x_ref, g_ref, o_ref):
    o_ref[...] = _rms(x_ref[...], g_ref[...]).astype(o_ref.dtype)


def _rmsnorm(x, g, out_dtype, tm=512):
    s, d = x.shape
    return pl.pallas_call(
        _norm_kernel,
        grid=(s // tm,),
        in_specs=[pl.BlockSpec((tm, d), lambda i: (i, 0)), pl.BlockSpec((1, d), lambda i: (0, 0))],
        out_specs=pl.BlockSpec((tm, d), lambda i: (i, 0)),
        out_shape=jax.ShapeDtypeStruct((s, d), out_dtype),
        compiler_params=_params(1),
        name="rmsnorm",
    )(x, g[None, :])


def _mla_in_kernel(h_ref, g_ref, w_ref, qn_ref, kvn_ref, c_ref, s1_ref, s2_ref, cq_ref, ckv_ref, kr_ref):
    a = _rms(h_ref[...], g_ref[...]).astype(BF16)
    proj = _dot(a, w_ref[...])
    cq_ref[...] = _rms(proj[:, :Q_LORA_RANK], qn_ref[...]).astype(BF16)
    ckv_ref[...] = _rms(proj[:, Q_LORA_RANK:Q_LORA_RANK + KV_LORA_RANK], kvn_ref[...]).astype(BF16)
    kr = proj[:, Q_LORA_RANK + KV_LORA_RANK:]
    kr_ref[...] = _rope(kr, c_ref[...], s1_ref[...], s2_ref[...]).astype(BF16)


def _mla_in(h, norm_g, w_pad, q_norm, kv_norm, tabs, tm=1024):
    s, d = h.shape
    n = w_pad.shape[1]
    row = lambda w: pl.BlockSpec((tm, w), lambda i: (i, 0))
    const = lambda r, w: pl.BlockSpec((r, w), lambda i: (0, 0))
    return pl.pallas_call(
        _mla_in_kernel,
        grid=(s // tm,),
        in_specs=[row(d), const(1, d), const(d, n), const(1, Q_LORA_RANK), const(1, KV_LORA_RANK),
                  row(LANES), row(LANES), row(LANES)],
        out_specs=[row(Q_LORA_RANK), row(KV_LORA_RANK), row(LANES)],
        out_shape=[jax.ShapeDtypeStruct((s, Q_LORA_RANK), BF16),
                   jax.ShapeDtypeStruct((s, KV_LORA_RANK), BF16),
                   jax.ShapeDtypeStruct((s, LANES), BF16)],
        compiler_params=_params(1),
        name="mla_in",
    )(h, norm_g[None, :], w_pad, q_norm[None, :], kv_norm[None, :], *tabs)


def _mla_q_kernel(w_ref, cq_ref, c_ref, s1_ref, s2_ref, q_ref, *, heads, scale):
    acc = _dot_nt(w_ref[...], cq_ref[...])
    c, s1, s2 = c_ref[...], s1_ref[...], s2_ref[...]
    half = QK_ROPE_DIM // 2
    for h in range(heads):
        lo = h * QK_PAD_DIM
        q_ref[lo:lo + LANES, :] = (acc[lo:lo + LANES, :] * scale).astype(BF16)
        t = acc[lo + LANES:lo + QK_PAD_DIM, :]
        down = jnp.concatenate([t[LANES - half:], t[:LANES - half]], axis=0)
        up = jnp.concatenate([t[half:], t[:half]], axis=0)
        q_ref[lo + LANES:lo + QK_PAD_DIM, :] = ((t * c + down * s1 + up * s2) * scale).astype(BF16)


def _mla_q(cq, w_uq_pad_t, tabs_t, scale, tm=1024, heads_per_tile=8):
    s, r = cq.shape
    n = w_uq_pad_t.shape[0]
    tn = heads_per_tile * QK_PAD_DIM
    col = pl.BlockSpec((LANES, tm), lambda j, i: (0, i))
    return pl.pallas_call(
        functools.partial(_mla_q_kernel, heads=heads_per_tile, scale=scale),
        grid=(n // tn, s // tm),
        in_specs=[pl.BlockSpec((tn, r), lambda j, i: (j, 0)), pl.BlockSpec((tm, r), lambda j, i: (i, 0)),
                  col, col, col],
        out_specs=pl.BlockSpec((tn, tm), lambda j, i: (j, i)),
        out_shape=jax.ShapeDtypeStruct((n, s), BF16),
        compiler_params=_params(2),
        name="mla_q",
    )(w_uq_pad_t, cq, *tabs_t)


V_ROWS = V_HEAD_DIM + 16


def _mla_kv_kernel(ckv_ref, wk_ref, wv_ref, kr_ref, k_ref, v_ref, *, heads, t):
    ckv = ckv_ref[...]
    kn = _dot(ckv, wk_ref[...])
    vt = _dot_nt(wv_ref[...], ckv)
    kr = kr_ref[...]
    ones = jnp.ones((V_ROWS - V_HEAD_DIM, t), BF16)
    for h in range(heads):
        k_ref[:, h * QK_PAD_DIM:h * QK_PAD_DIM + LANES] = kn[:, h * LANES:(h + 1) * LANES].astype(BF16)
        k_ref[:, h * QK_PAD_DIM + LANES:(h + 1) * QK_PAD_DIM] = kr
        for tt in range(ckv.shape[0] // t):
            v_ref[h, tt, :V_HEAD_DIM, :] = vt[h * V_HEAD_DIM:(h + 1) * V_HEAD_DIM, tt * t:(tt + 1) * t].astype(BF16)
            v_ref[h, tt, V_HEAD_DIM:, :] = ones


def _mla_kv(ckv, w_uk, w_uv_t, kr, t, tm=1024, heads_per_tile=8):
    s, r = ckv.shape
    n = w_uk.shape[1]
    heads = n // LANES
    tn = heads_per_tile * LANES
    return pl.pallas_call(
        functools.partial(_mla_kv_kernel, heads=heads_per_tile, t=t),
        grid=(s // tm, n // tn),
        in_specs=[pl.BlockSpec((tm, r), lambda i, j: (i, 0)),
                  pl.BlockSpec((r, tn), lambda i, j: (0, j)),
                  pl.BlockSpec((tn, r), lambda i, j: (j, 0)),
                  pl.BlockSpec((tm, LANES), lambda i, j: (i, 0))],
        out_specs=[pl.BlockSpec((tm, 2 * tn), lambda i, j: (i, j)),
                   pl.BlockSpec((heads_per_tile, tm // t, V_ROWS, t), lambda i, j: (j, i, 0, 0))],
        out_shape=[jax.ShapeDtypeStruct((s, 2 * n), BF16),
                   jax.ShapeDtypeStruct((heads, s // t, V_ROWS, t), BF16)],
        compiler_params=_params(2),
        name="mla_kv",
    )(ckv, w_uk, w_uv_t, kr)


def _mla_attn_kernel(q_ref, k_ref, v_ref, o_ref, m_sc, acc_sc, s_even, s_odd, *, t, unrolls):
    i = pl.program_id(1)
    m_sc[...] = jnp.full(m_sc.shape, -jnp.inf, F32)
    acc_sc[...] = jnp.zeros(acc_sc.shape, F32)
    q = q_ref[...]

    def scores(j):
        return _dot(k_ref[pl.ds(pl.multiple_of(j * t, t), t), :], q)

    def softmax_pv(s_ref, j, diagonal):
        def load():
            s = s_ref[...]
            if diagonal:
                key = lax.broadcasted_iota(jnp.int32, (t, t), 0)
                query = lax.broadcasted_iota(jnp.int32, (t, t), 1)
                s = jnp.where(key <= query, s, -jnp.inf)
            return s
        m_prev = m_sc[...]
        m_next = jnp.maximum(m_prev, jnp.max(load(), axis=0, keepdims=True))
        m_sc[...] = m_next
        p = jnp.exp2(load() - m_sc[...]).astype(BF16)
        alpha = jnp.exp2(m_prev - m_next)
        acc_sc[...] = alpha * acc_sc[...] + _dot(v_ref[j], p)

    s_even[...] = scores(0)

    def pairs(first, n_pairs):
        for d in range(n_pairs):
            j = first + 2 * d
            s_odd[...] = scores(j + 1)
            softmax_pv(s_even, j, False)
            s_even[...] = scores(j + 2)
            softmax_pv(s_odd, j + 1, False)

    done = 0
    for u in unrolls:
        n_groups = (i // 2 - done) // u

        def group(g, carry, u=u, done=done):
            pairs(2 * (done + g * u), u)
            return carry

        lax.fori_loop(0, n_groups, group, 0)
        done = done + n_groups * u

    @pl.when(i % 2 == 0)
    def _():
        softmax_pv(s_even, i, True)

    @pl.when(i % 2 == 1)
    def _():
        s_odd[...] = scores(i)
        softmax_pv(s_even, i - 1, False)
        softmax_pv(s_odd, i, True)

    out_t = acc_sc[:V_HEAD_DIM, :] / acc_sc[V_HEAD_DIM:V_HEAD_DIM + 1, :]
    o_ref[...] = out_t.T.astype(o_ref.dtype)


def _mla_attn(q_t, k, v_t, t, unrolls=(4, 2, 1)):
    s = k.shape[0]
    h = MLA_HEADS
    assert unrolls[-1] == 1
    return pl.pallas_call(
        functools.partial(_mla_attn_kernel, t=t, unrolls=unrolls),
        grid=(h, s // t),
        in_specs=[pl.BlockSpec((QK_PAD_DIM, t), lambda hh, i: (hh, i)),
                  pl.BlockSpec((s, QK_PAD_DIM), lambda hh, i: (0, hh)),
                  pl.BlockSpec((None, s // t, V_ROWS, t), lambda hh, i: (hh, 0, 0, 0))],
        out_specs=pl.BlockSpec((t, V_HEAD_DIM), lambda hh, i: (i, hh)),
        out_shape=jax.ShapeDtypeStruct((s, h * V_HEAD_DIM), BF16),
        scratch_shapes=[pltpu.VMEM((1, t), F32),
                        pltpu.VMEM((V_ROWS, t), F32),
                        pltpu.VMEM((t, t), F32), pltpu.VMEM((t, t), F32)],
        compiler_params=_params(2),
        name="mla_attn",
    )(q_t, k, v_t)


def _sb_attn_kernel(q_ref, k_ref, v_ref, o_ref, tri_sc, carry_sc, acc_sc, *, t, chains):
    n_tiles = q_ref.shape[0] // t
    w = 2 * t
    tri_sc[...] = (lax.broadcasted_iota(jnp.int32, (w, w), 0)
                   > lax.broadcasted_iota(jnp.int32, (w, w), 1)).astype(BF16)
    row = lax.broadcasted_iota(jnp.int32, (t, w), 0)
    col = lax.broadcasted_iota(jnp.int32, (t, w), 1)

    def log_probs(q, start, width):
        z = _dot_nt(q, k_ref[pl.ds(start, width), :])
        soft = jnp.log2(1.0 + jnp.exp2(-jnp.abs(z)))
        log_om = -(jnp.maximum(z, 0.0) + soft)
        log_beta = log_om + z
        return log_om, log_beta

    def window(c, i):
        q = q_ref[pl.ds(pl.multiple_of(i * t, t), t), :]
        first = jnp.maximum(i - 1, 0)
        start = pl.multiple_of(first * t, t)
        valid = col < row + (i - first) * t
        log_om, log_beta = log_probs(q, start, w)
        log_om = jnp.where(valid, log_om, 0.0)
        after = _dot(log_om.astype(BF16), tri_sc[...])
        a = jnp.where(valid, jnp.exp2(log_beta + after), 0.0)
        acc_sc[c] = _dot(a.astype(BF16), v_ref[pl.ds(start, w), :])
        carry_sc[c] = jnp.broadcast_to(jnp.sum(log_om, axis=1, keepdims=True), (t, LANES))

    def earlier_tile(c, q, j):
        start = pl.multiple_of(j * t, t)
        log_om, log_beta = log_probs(q, start, t)
        after = _dot(log_om.astype(BF16), tri_sc[:t, :t]) + jnp.tile(carry_sc[c], (1, t // LANES))
        a = jnp.exp2(log_beta + after)
        acc_sc[c] += _dot(a.astype(BF16), v_ref[pl.ds(start, t), :])
        carry_sc[c] += jnp.sum(log_om, axis=1, keepdims=True)

    def earlier_tiles(c, i):
        rows = pl.ds(pl.multiple_of(i * t, t), t)
        q = q_ref[rows, :]

        def more(state):
            j, carry_max = state
            return (j >= 0) & (carry_max > SB_LOG2_WEIGHT_FLOOR)

        def body(state):
            j, _ = state
            earlier_tile(c, q, j)
            return j - 1, jnp.max(carry_sc[c])

        lax.while_loop(more, body, (i - 2, jnp.max(carry_sc[c])))
        o_ref[rows, :] = acc_sc[c].astype(o_ref.dtype)

    def q_tiles(p, _):
        for c in range(chains):
            window(c, p * chains + c)
        for c in range(chains):
            earlier_tiles(c, p * chains + c)
        return 0

    lax.fori_loop(0, n_tiles // chains, q_tiles, 0)


def _sb_attn(qkv, t=256, chains=2):
    s = qkv.shape[0]
    h, d = SB_HEADS, SB_HEAD_DIM
    assert s % (t * chains) == 0 and s >= 2 * t
    return pl.pallas_call(
        functools.partial(_sb_attn_kernel, t=t, chains=chains),
        grid=(h,),
        in_specs=[pl.BlockSpec((s, d), lambda hh: (0, hh)),
                  pl.BlockSpec((s, d), lambda hh: (0, h + hh)),
                  pl.BlockSpec((s, d), lambda hh: (0, 2 * h + hh))],
        out_specs=pl.BlockSpec((s, d), lambda hh: (0, hh)),
        out_shape=jax.ShapeDtypeStruct((s, h * d), BF16),
        scratch_shapes=[pltpu.VMEM((2 * t, 2 * t), BF16), pltpu.VMEM((chains, t, LANES), F32),
                        pltpu.VMEM((chains, t, d), F32)],
        compiler_params=_params(1),
        name="sb_attn",
    )(qkv, qkv, qkv)


def _normalize_rows_once(h_ref, g_ref, a_sc):
    @pl.when(pl.program_id(1) == 0)
    def _():
        a_sc[...] = _rms(h_ref[...], g_ref[...]).astype(BF16)


def _norm_matmul_scale_kernel(h_ref, g_ref, w_ref, cs_ref, o_ref, a_sc):
    _normalize_rows_once(h_ref, g_ref, a_sc)
    o_ref[...] = (_dot(a_sc[...], w_ref[...].astype(BF16)) * cs_ref[...]).astype(o_ref.dtype)


def _norm_matmul_scale(h, norm_g, w, layer, col_scale, tm=1024, tn=1024):
    s, k = h.shape
    n = w.shape[2]
    return pl.pallas_call(
        _norm_matmul_scale_kernel,
        grid=(s // tm, n // tn),
        in_specs=[pl.BlockSpec((tm, k), lambda i, j: (i, 0)), pl.BlockSpec((1, k), lambda i, j: (0, 0)),
                  pl.BlockSpec((None, k, tn), lambda i, j: (layer, 0, j)),
                  pl.BlockSpec((1, tn), lambda i, j: (0, j))],
        out_specs=pl.BlockSpec((tm, tn), lambda i, j: (i, j)),
        out_shape=jax.ShapeDtypeStruct((s, n), BF16),
        scratch_shapes=[pltpu.VMEM((tm, k), BF16)],
        compiler_params=_params(2),
        name="norm_matmul_scale",
    )(h, norm_g[None, :], w, col_scale[None, :])


def _matmul_residual_kernel(a_ref, w_ref, r_ref, o_ref):
    o_ref[...] = r_ref[...] + _dot(a_ref[...], w_ref[...].astype(BF16))


def _matmul_residual(a, w, layer, res, tm, tn):
    s, k = a.shape
    n = w.shape[2]
    return pl.pallas_call(
        _matmul_residual_kernel,
        grid=(s // tm, n // tn),
        in_specs=[pl.BlockSpec((tm, k), lambda i, j: (i, 0)),
                  pl.BlockSpec((None, k, tn), lambda i, j: (layer, 0, j)),
                  pl.BlockSpec((tm, tn), lambda i, j: (i, j))],
        out_specs=pl.BlockSpec((tm, tn), lambda i, j: (i, j)),
        out_shape=jax.ShapeDtypeStruct((s, n), F32),
        compiler_params=_params(2),
        name="matmul_residual",
    )(a, w, res)


def _ffn_up_kernel(h_ref, g_ref, wg_ref, wu_ref, o_ref, a_sc):
    _normalize_rows_once(h_ref, g_ref, a_sc)
    a = a_sc[...]
    g = _dot(a, wg_ref[...].astype(BF16))
    u = _dot(a, wu_ref[...].astype(BF16))
    o_ref[...] = (g * jax.nn.sigmoid(g) * u).astype(o_ref.dtype)


def _ffn_up(h, norm_g, w_gate_up, layer, tm=1024, tn=512):
    s, k = h.shape
    d_ff = w_gate_up.shape[2] // 2
    nj = d_ff // tn
    return pl.pallas_call(
        _ffn_up_kernel,
        grid=(s // tm, nj),
        in_specs=[pl.BlockSpec((tm, k), lambda i, j: (i, 0)), pl.BlockSpec((1, k), lambda i, j: (0, 0)),
                  pl.BlockSpec((None, k, tn), lambda i, j: (layer, 0, j)),
                  pl.BlockSpec((None, k, tn), lambda i, j: (layer, 0, nj + j))],
        out_specs=pl.BlockSpec((tm, tn), lambda i, j: (i, j)),
        out_shape=jax.ShapeDtypeStruct((s, d_ff), BF16),
        scratch_shapes=[pltpu.VMEM((tm, k), BF16)],
        compiler_params=_params(2),
        name="ffn_up",
    )(h, norm_g[None, :], w_gate_up, w_gate_up)


def _ffn(h, norm_g, w_gate_up, w_down, layer):
    act = _ffn_up(h, norm_g, w_gate_up, layer)
    return _matmul_residual(act, w_down[layer].astype(BF16)[None], 0, h, tm=1024, tn=512)


def kernel(x, positions, attn_norm, mla_w_in, mla_q_norm, mla_kv_norm, mla_w_uq, mla_w_ukv, mla_w_o,
           sb_w_in, sb_w_o, ffn_norm, ffn_w_gate_up, ffn_w_down, final_norm):
    b, s, d = x.shape
    assert b == 1 and d == D_MODEL
    h = x.reshape(s, d)

    attn_tile = 512
    tabs = _rope_tables(positions.reshape(s, 1))
    tabs_t = _rope_tables_t(positions.reshape(1, s))
    w_in_pad = jnp.pad(mla_w_in[0], ((0, 0), (0, LANES - QK_ROPE_DIM))).astype(BF16)
    qk_dim = QK_NOPE_DIM + QK_ROPE_DIM
    w_uq_pad = jnp.pad(mla_w_uq[0].reshape(Q_LORA_RANK, MLA_HEADS, qk_dim),
                       ((0, 0), (0, 0), (0, QK_PAD_DIM - qk_dim)))
    w_uq_pad_t = w_uq_pad.reshape(Q_LORA_RANK, MLA_HEADS * QK_PAD_DIM).T.astype(BF16)
    w_ukv = mla_w_ukv[0].reshape(KV_LORA_RANK, MLA_HEADS, QK_NOPE_DIM + V_HEAD_DIM)
    w_uk = w_ukv[:, :, :QK_NOPE_DIM].reshape(KV_LORA_RANK, MLA_HEADS * QK_NOPE_DIM).astype(BF16)
    w_uv_t = w_ukv[:, :, QK_NOPE_DIM:].reshape(KV_LORA_RANK, MLA_HEADS * V_HEAD_DIM).T.astype(BF16)

    cq, ckv, kr = _mla_in(h, attn_norm[0], w_in_pad, mla_q_norm[0], mla_kv_norm[0], tabs)
    q_t = _mla_q(cq, w_uq_pad_t, tabs_t, qk_dim ** -0.5 * LOG2_E)
    k, v_t = _mla_kv(ckv, w_uk, w_uv_t, kr, attn_tile)
    o = _mla_attn(q_t, k, v_t, attn_tile)
    h = _matmul_residual(o, mla_w_o[0].astype(BF16)[None], 0, h, tm=1024, tn=1024)
    h = _ffn(h, ffn_norm[0], ffn_w_gate_up, ffn_w_down, 0)

    n_q = SB_HEADS * SB_HEAD_DIM
    col_scale = jnp.concatenate([jnp.full((n_q,), SB_HEAD_DIM ** -0.5 * LOG2_E, F32), jnp.ones((2 * n_q,), F32)])
    qkv = _norm_matmul_scale(h, attn_norm[1], sb_w_in, 0, col_scale)
    o = _sb_attn(qkv)
    h = _matmul_residual(o, sb_w_o[0].astype(BF16)[None], 0, h, tm=1024, tn=1024)
    h = _ffn(h, ffn_norm[1], ffn_w_gate_up, ffn_w_down, 1)

    return _rmsnorm(h, final_norm, F32).reshape(b, s, d)
```

```python
import functools

import jax
import jax.numpy as jnp
from jax import lax
from jax.experimental import pallas as pl
from jax.experimental.pallas import tpu as pltpu

D_MODEL = 2048
MLA_HEADS = 16
Q_LORA_RANK = 512
KV_LORA_RANK = 512
QK_NOPE_DIM = 128
QK_ROPE_DIM = 64
V_HEAD_DIM = 128
ROPE_THETA = 10000.0
SB_HEADS = 16
SB_HEAD_DIM = 128
RMS_EPS = 1e-6

LANES = 128
QK_PAD_DIM = 2 * LANES
VMEM_LIMIT_BYTES = 56 * 1024 * 1024

LOG2_E = 1.4426950408889634
SB_LOG2_WEIGHT_FLOOR = -120.0 * LOG2_E

F32 = jnp.float32
BF16 = jnp.bfloat16


def _params(n_axes):
    return pltpu.CompilerParams(
        dimension_semantics=("arbitrary",) * n_axes, vmem_limit_bytes=VMEM_LIMIT_BYTES)


def _dot(a, b):
    return jnp.dot(a, b, preferred_element_type=F32)


def _dot_nt(a, b):
    return lax.dot_general(a, b, (((1,), (1,)), ((), ())), preferred_element_type=F32)


def _rms(x, g):
    return x * lax.rsqrt(jnp.mean(x * x, axis=-1, keepdims=True) + RMS_EPS) * g


def _rope(t, c, s1, s2):
    half = QK_ROPE_DIM // 2
    return t * c + pltpu.roll(t, half, 1) * s1 + pltpu.roll(t, LANES - half, 1) * s2


def _rope_tables_kernel(pos_ref, freq_ref, c_ref, s1_ref, s2_ref):
    half = QK_ROPE_DIM // 2
    ang = pos_ref[...].astype(F32) * freq_ref[...]
    lane = lax.broadcasted_iota(jnp.int32, ang.shape, 1)
    cos, sin = jnp.cos(ang), jnp.sin(ang)
    c_ref[...] = jnp.where(lane < 2 * half, cos, 0.0)
    s1_ref[...] = jnp.where((lane >= half) & (lane < 2 * half), sin, 0.0)
    s2_ref[...] = jnp.where(lane < half, -sin, 0.0)


def _rope_tables(positions, tm=2048):
    s = positions.shape[0]
    inv_freq = ROPE_THETA ** (-jnp.arange(0, QK_ROPE_DIM, 2, dtype=F32) / QK_ROPE_DIM)
    freq = jnp.concatenate([inv_freq, inv_freq, jnp.zeros((LANES - QK_ROPE_DIM,), F32)])[None, :]
    out = jax.ShapeDtypeStruct((s, LANES), F32)
    row = pl.BlockSpec((tm, LANES), lambda i: (i, 0))
    return pl.pallas_call(
        _rope_tables_kernel,
        grid=(s // tm,),
        in_specs=[pl.BlockSpec((tm, 1), lambda i: (i, 0)), pl.BlockSpec((1, LANES), lambda i: (0, 0))],
        out_specs=[row, row, row],
        out_shape=[out, out, out],
        compiler_params=_params(1),
        name="rope_tables",
    )(positions, freq)


def _rope_tables_t_kernel(pos_ref, freq_ref, c_ref, s1_ref, s2_ref):
    half = QK_ROPE_DIM // 2
    ang = freq_ref[...] * pos_ref[...].astype(F32)
    r = lax.broadcasted_iota(jnp.int32, ang.shape, 0)
    cos, sin = jnp.cos(ang), jnp.sin(ang)
    c_ref[...] = jnp.where(r < 2 * half, cos, 0.0)
    s1_ref[...] = jnp.where((r >= half) & (r < 2 * half), sin, 0.0)
    s2_ref[...] = jnp.where(r < half, -sin, 0.0)


def _rope_tables_t(positions, tm=2048):
    s = positions.shape[1]
    inv_freq = ROPE_THETA ** (-jnp.arange(0, QK_ROPE_DIM, 2, dtype=F32) / QK_ROPE_DIM)
    freq = jnp.concatenate([inv_freq, inv_freq, jnp.zeros((LANES - QK_ROPE_DIM,), F32)])[:, None]
    out = jax.ShapeDtypeStruct((LANES, s), F32)
    col = pl.BlockSpec((LANES, tm), lambda i: (0, i))
    return pl.pallas_call(
        _rope_tables_t_kernel,
        grid=(s // tm,),
        in_specs=[pl.BlockSpec((1, tm), lambda i: (0, i)), pl.BlockSpec((LANES, 1), lambda i: (0, 0))],
        out_specs=[col, col, col],
        out_shape=[out, out, out],
        compiler_params=_params(1),
        name="rope_tables_t",
    )(positions, freq)


def _norm_kernel(x_ref, g_ref, o_ref):
    o_ref[...] = _rms(x_ref[...], g_ref[...]).astype(o_ref.dtype)


def _rmsnorm(x, g, out_dtype, tm=512):
    s, d = x.shape
    return pl.pallas_call(
        _norm_kernel,
        grid=(s // tm,),
        in_specs=[pl.BlockSpec((tm, d), lambda i: (i, 0)), pl.BlockSpec((1, d), lambda i: (0, 0))],
        out_specs=pl.BlockSpec((tm, d), lambda i: (i, 0)),
        out_shape=jax.ShapeDtypeStruct((s, d), out_dtype),
        compiler_params=_params(1),
        name="rmsnorm",
    )(x, g[None, :])


def _mla_in_kernel(h_ref, g_ref, w_ref, qn_ref, kvn_ref, c_ref, s1_ref, s2_ref, cq_ref, ckv_ref, kr_ref):
    a = _rms(h_ref[...], g_ref[...]).astype(BF16)
    proj = _dot(a, w_ref[...])
    cq_ref[...] = _rms(proj[:, :Q_LORA_RANK], qn_ref[...]).astype(BF16)
    ckv_ref[...] = _rms(proj[:, Q_LORA_RANK:Q_LORA_RANK + KV_LORA_RANK], kvn_ref[...]).astype(BF16)
    kr = proj[:, Q_LORA_RANK + KV_LORA_RANK:]
    kr_ref[...] = _rope(kr, c_ref[...], s1_ref[...], s2_ref[...]).astype(BF16)


def _mla_in(h, norm_g, w_pad, q_norm, kv_norm, tabs, tm=1024):
    s, d = h.shape
    n = w_pad.shape[1]
    row = lambda w: pl.BlockSpec((tm, w), lambda i: (i, 0))
    const = lambda r, w: pl.BlockSpec((r, w), lambda i: (0, 0))
    return pl.pallas_call(
        _mla_in_kernel,
        grid=(s // tm,),
        in_specs=[row(d), const(1, d), const(d, n), const(1, Q_LORA_RANK), const(1, KV_LORA_RANK),
                  row(LANES), row(LANES), row(LANES)],
        out_specs=[row(Q_LORA_RANK), row(KV_LORA_RANK), row(LANES)],
        out_shape=[jax.ShapeDtypeStruct((s, Q_LORA_RANK), BF16),
                   jax.ShapeDtypeStruct((s, KV_LORA_RANK), BF16),
                   jax.ShapeDtypeStruct((s, LANES), BF16)],
        compiler_params=_params(1),
        name="mla_in",
    )(h, norm_g[None, :], w_pad, q_norm[None, :], kv_norm[None, :], *tabs)


def _mla_q_kernel(w_ref, cq_ref, c_ref, s1_ref, s2_ref, q_ref, *, heads, scale):
    acc = _dot_nt(w_ref[...], cq_ref[...])
    c, s1, s2 = c_ref[...], s1_ref[...], s2_ref[...]
    half = QK_ROPE_DIM // 2
    tq = q_ref.shape[3]
    for h in range(heads):
        lo = h * QK_PAD_DIM
        nope = (acc[lo:lo + LANES, :] * scale).astype(BF16)
        t = acc[lo + LANES:lo + QK_PAD_DIM, :]
        down = jnp.concatenate([t[LANES - half:], t[:LANES - half]], axis=0)
        up = jnp.concatenate([t[half:], t[:half]], axis=0)
        rope = ((t * c + down * s1 + up * s2) * scale).astype(BF16)
        for tt in range(acc.shape[1] // tq):
            q_ref[h, tt, :LANES, :] = nope[:, tt * tq:(tt + 1) * tq]
            q_ref[h, tt, LANES:, :] = rope[:, tt * tq:(tt + 1) * tq]


def _mla_q(cq, w_uq_pad_t, tabs_t, scale, t, tm=1024, heads_per_tile=8):
    s, r = cq.shape
    n = w_uq_pad_t.shape[0]
    tn = heads_per_tile * QK_PAD_DIM
    col = pl.BlockSpec((LANES, tm), lambda j, i: (0, i))
    return pl.pallas_call(
        functools.partial(_mla_q_kernel, heads=heads_per_tile, scale=scale),
        grid=(n // tn, s // tm),
        in_specs=[pl.BlockSpec((tn, r), lambda j, i: (j, 0)), pl.BlockSpec((tm, r), lambda j, i: (i, 0)),
                  col, col, col],
        out_specs=pl.BlockSpec((heads_per_tile, tm // t, QK_PAD_DIM, t), lambda j, i: (j, i, 0, 0)),
        out_shape=jax.ShapeDtypeStruct((n // QK_PAD_DIM, s // t, QK_PAD_DIM, t), BF16),
        compiler_params=_params(2),
        name="mla_q",
    )(w_uq_pad_t, cq, *tabs_t)


V_ROWS = V_HEAD_DIM + 16


def _mla_kv_kernel(ckv_ref, wk_ref, wv_ref, kr_ref, k_ref, v_ref, *, heads, t):
    ckv = ckv_ref[...]
    kn = _dot(ckv, wk_ref[...])
    vt = _dot_nt(wv_ref[...], ckv)
    kr = kr_ref[...]
    ones = jnp.ones((V_ROWS - V_HEAD_DIM, t), BF16)
    for h in range(heads):
        k_ref[:, h * QK_PAD_DIM:h * QK_PAD_DIM + LANES] = kn[:, h * LANES:(h + 1) * LANES].astype(BF16)
        k_ref[:, h * QK_PAD_DIM + LANES:(h + 1) * QK_PAD_DIM] = kr
        for tt in range(ckv.shape[0] // t):
            v_ref[h, tt, :V_HEAD_DIM, :] = vt[h * V_HEAD_DIM:(h + 1) * V_HEAD_DIM, tt * t:(tt + 1) * t].astype(BF16)
            v_ref[h, tt, V_HEAD_DIM:, :] = ones


def _mla_kv(ckv, w_uk, w_uv_t, kr, t, tm=1024, heads_per_tile=8):
    s, r = ckv.shape
    n = w_uk.shape[1]
    heads = n // LANES
    tn = heads_per_tile * LANES
    return pl.pallas_call(
        functools.partial(_mla_kv_kernel, heads=heads_per_tile, t=t),
        grid=(s // tm, n // tn),
        in_specs=[pl.BlockSpec((tm, r), lambda i, j: (i, 0)),
                  pl.BlockSpec((r, tn), lambda i, j: (0, j)),
                  pl.BlockSpec((tn, r), lambda i, j: (j, 0)),
                  pl.BlockSpec((tm, LANES), lambda i, j: (i, 0))],
        out_specs=[pl.BlockSpec((tm, 2 * tn), lambda i, j: (i, j)),
                   pl.BlockSpec((heads_per_tile, tm // t, V_ROWS, t), lambda i, j: (j, i, 0, 0))],
        out_shape=[jax.ShapeDtypeStruct((s, 2 * n), BF16),
                   jax.ShapeDtypeStruct((heads, s // t, V_ROWS, t), BF16)],
        compiler_params=_params(2),
        name="mla_kv",
    )(ckv, w_uk, w_uv_t, kr)


def _mla_attn_kernel(q_ref, k_ref, v_ref, o_ref, m_sc, acc_sc, s_even, s_odd, *, t, unrolls):
    def q_tile(qi, carry):
        _mla_attn_q_tile(pl.program_id(1) * q_ref.shape[0] + qi, q_ref[qi], k_ref, v_ref,
                         o_ref.at[pl.ds(pl.multiple_of(qi * t, t), t), :],
                         m_sc, acc_sc, s_even, s_odd, t=t, unrolls=unrolls)
        return carry

    lax.fori_loop(0, q_ref.shape[0], q_tile, 0)


def _mla_attn_q_tile(i, q, k_ref, v_ref, o_ref, m_sc, acc_sc, s_even, s_odd, *, t, unrolls):
    m_sc[...] = jnp.full(m_sc.shape, -jnp.inf, F32)
    acc_sc[...] = jnp.zeros(acc_sc.shape, F32)

    def scores(j):
        return _dot(k_ref[pl.ds(pl.multiple_of(j * t, t), t), :], q)

    def softmax_pv(s_ref, j, diagonal):
        def load():
            s = s_ref[...]
            if diagonal:
                key = lax.broadcasted_iota(jnp.int32, (t, t), 0)
                query = lax.broadcasted_iota(jnp.int32, (t, t), 1)
                s = jnp.where(key <= query, s, -jnp.inf)
            return s
        m_prev = m_sc[...]
        m_next = jnp.maximum(m_prev, jnp.max(load(), axis=0, keepdims=True))
        m_sc[...] = m_next
        p = jnp.exp2(load() - m_sc[...]).astype(BF16)
        alpha = jnp.exp2(m_prev - m_next)
        acc_sc[...] = alpha * acc_sc[...] + _dot(v_ref[j], p)

    s_even[...] = scores(0)

    def pairs(first, n_pairs):
        for d in range(n_pairs):
            j = first + 2 * d
            s_odd[...] = scores(j + 1)
            softmax_pv(s_even, j, False)
            s_even[...] = scores(j + 2)
            softmax_pv(s_odd, j + 1, False)

    done = 0
    for u in unrolls:
        n_groups = (i // 2 - done) // u

        def group(g, carry, u=u, done=done):
            pairs(2 * (done + g * u), u)
            return carry

        lax.fori_loop(0, n_groups, group, 0)
        done = done + n_groups * u

    @pl.when(i % 2 == 0)
    def _():
        softmax_pv(s_even, i, True)

    @pl.when(i % 2 == 1)
    def _():
        s_odd[...] = scores(i)
        softmax_pv(s_even, i - 1, False)
        softmax_pv(s_odd, i, True)

    out_t = acc_sc[:V_HEAD_DIM, :] / acc_sc[V_HEAD_DIM:V_HEAD_DIM + 1, :]
    o_ref[...] = out_t.T.astype(o_ref.dtype)


def _mla_attn(q_t, k, v_t, t, q_tiles_per_step=4, unrolls=(4, 2, 1)):
    s = k.shape[0]
    h = MLA_HEADS
    g = q_tiles_per_step
    assert unrolls[-1] == 1
    return pl.pallas_call(
        functools.partial(_mla_attn_kernel, t=t, unrolls=unrolls),
        grid=(h, s // (t * g)),
        in_specs=[pl.BlockSpec((None, g, QK_PAD_DIM, t), lambda hh, i: (hh, i, 0, 0)),
                  pl.BlockSpec((s, QK_PAD_DIM), lambda hh, i: (0, hh)),
                  pl.BlockSpec((None, s // t, V_ROWS, t), lambda hh, i: (hh, 0, 0, 0))],
        out_specs=pl.BlockSpec((g * t, V_HEAD_DIM), lambda hh, i: (i, hh)),
        out_shape=jax.ShapeDtypeStruct((s, h * V_HEAD_DIM), BF16),
        scratch_shapes=[pltpu.VMEM((1, t), F32),
                        pltpu.VMEM((V_ROWS, t), F32),
                        pltpu.VMEM((t, t), F32), pltpu.VMEM((t, t), F32)],
        compiler_params=_params(2),
        name="mla_attn",
    )(q_t, k, v_t)


def _sb_attn_kernel(q_ref, k_ref, v_ref, o_ref, tri_sc, carry_sc, acc_sc, *, t, chains):
    n_tiles = q_ref.shape[0] // t
    w = 2 * t
    tri_sc[...] = (lax.broadcasted_iota(jnp.int32, (w, w), 0)
                   > lax.broadcasted_iota(jnp.int32, (w, w), 1)).astype(BF16)
    row = lax.broadcasted_iota(jnp.int32, (t, w), 0)
    col = lax.broadcasted_iota(jnp.int32, (t, w), 1)

    def log_probs(q, start, width):
        z = _dot_nt(q, k_ref[pl.ds(start, width), :])
        soft = jnp.log2(1.0 + jnp.exp2(-jnp.abs(z)))
        log_om = -(jnp.maximum(z, 0.0) + soft)
        log_beta = log_om + z
        return log_om, log_beta

    def window(c, i):
        q = q_ref[pl.ds(pl.multiple_of(i * t, t), t), :]
        first = jnp.maximum(i - 1, 0)
        start = pl.multiple_of(first * t, t)
        valid = col < row + (i - first) * t
        log_om, log_beta = log_probs(q, start, w)
        log_om = jnp.where(valid, log_om, 0.0)
        after = _dot(log_om.astype(BF16), tri_sc[...])
        a = jnp.where(valid, jnp.exp2(log_beta + after), 0.0)
        acc_sc[c] = _dot(a.astype(BF16), v_ref[pl.ds(start, w), :])
        carry_sc[c] = jnp.broadcast_to(jnp.sum(log_om, axis=1, keepdims=True), (t, LANES))

    def earlier_tile(c, q, j):
        start = pl.multiple_of(j * t, t)
        log_om, log_beta = log_probs(q, start, t)
        after = _dot(log_om.astype(BF16), tri_sc[:t, :t]) + jnp.tile(carry_sc[c], (1, t // LANES))
        a = jnp.exp2(log_beta + after)
        acc_sc[c] += _dot(a.astype(BF16), v_ref[pl.ds(start, t), :])
        carry_sc[c] += jnp.sum(log_om, axis=1, keepdims=True)

    def earlier_tiles(c, i):
        rows = pl.ds(pl.multiple_of(i * t, t), t)
        q = q_ref[rows, :]

        def more(state):
            j, carry_max = state
            return (j >= 0) & (carry_max > SB_LOG2_WEIGHT_FLOOR)

        def body(state):
            j, _ = state
            earlier_tile(c, q, j)
            return j - 1, jnp.max(carry_sc[c])

        lax.while_loop(more, body, (i - 2, jnp.max(carry_sc[c])))
        o_ref[rows, :] = acc_sc[c].astype(o_ref.dtype)

    def q_tiles(p, _):
        for c in range(chains):
            window(c, p * chains + c)
        for c in range(chains):
            earlier_tiles(c, p * chains + c)
        return 0

    lax.fori_loop(0, n_tiles // chains, q_tiles, 0)


def _sb_attn(qkv, t=256, chains=2):
    s = qkv.shape[0]
    h, d = SB_HEADS, SB_HEAD_DIM
    assert s % (t * chains) == 0 and s >= 2 * t
    return pl.pallas_call(
        functools.partial(_sb_attn_kernel, t=t, chains=chains),
        grid=(h,),
        in_specs=[pl.BlockSpec((s, d), lambda hh: (0, hh)),
                  pl.BlockSpec((s, d), lambda hh: (0, h + hh)),
                  pl.BlockSpec((s, d), lambda hh: (0, 2 * h + hh))],
        out_specs=pl.BlockSpec((s, d), lambda hh: (0, hh)),
        out_shape=jax.ShapeDtypeStruct((s, h * d), BF16),
        scratch_shapes=[pltpu.VMEM((2 * t, 2 * t), BF16), pltpu.VMEM((chains, t, LANES), F32),
                        pltpu.VMEM((chains, t, d), F32)],
        compiler_params=_params(1),
        name="sb_attn",
    )(qkv, qkv, qkv)


def _normalize_rows_once(h_ref, g_ref, a_sc):
    @pl.when(pl.program_id(1) == 0)
    def _():
        a_sc[...] = _rms(h_ref[...], g_ref[...]).astype(BF16)


def _norm_matmul_scale_kernel(h_ref, g_ref, w_ref, cs_ref, o_ref, a_sc):
    _normalize_rows_once(h_ref, g_ref, a_sc)
    o_ref[...] = (_dot(a_sc[...], w_ref[...].astype(BF16)) * cs_ref[...]).astype(o_ref.dtype)


def _norm_matmul_scale(h, norm_g, w, layer, col_scale, tm=1024, tn=1024):
    s, k = h.shape
    n = w.shape[2]
    return pl.pallas_call(
        _norm_matmul_scale_kernel,
        grid=(s // tm, n // tn),
        in_specs=[pl.BlockSpec((tm, k), lambda i, j: (i, 0)), pl.BlockSpec((1, k), lambda i, j: (0, 0)),
                  pl.BlockSpec((None, k, tn), lambda i, j: (layer, 0, j)),
                  pl.BlockSpec((1, tn), lambda i, j: (0, j))],
        out_specs=pl.BlockSpec((tm, tn), lambda i, j: (i, j)),
        out_shape=jax.ShapeDtypeStruct((s, n), BF16),
        scratch_shapes=[pltpu.VMEM((tm, k), BF16)],
        compiler_params=_params(2),
        name="norm_matmul_scale",
    )(h, norm_g[None, :], w, col_scale[None, :])


def _matmul_residual_kernel(a_ref, w_ref, r_ref, o_ref):
    o_ref[...] = r_ref[...] + _dot(a_ref[...], w_ref[...].astype(BF16))


def _matmul_residual(a, w, layer, res, tm, tn):
    s, k = a.shape
    n = w.shape[2]
    return pl.pallas_call(
        _matmul_residual_kernel,
        grid=(s // tm, n // tn),
        in_specs=[pl.BlockSpec((tm, k), lambda i, j: (i, 0)),
                  pl.BlockSpec((None, k, tn), lambda i, j: (layer, 0, j)),
                  pl.BlockSpec((tm, tn), lambda i, j: (i, j))],
        out_specs=pl.BlockSpec((tm, tn), lambda i, j: (i, j)),
        out_shape=jax.ShapeDtypeStruct((s, n), F32),
        compiler_params=_params(2),
        name="matmul_residual",
    )(a, w, res)


def _ffn_up_kernel(h_ref, g_ref, wg_ref, wu_ref, o_ref, a_sc):
    _normalize_rows_once(h_ref, g_ref, a_sc)
    a = a_sc[...]
    g = _dot(a, wg_ref[...].astype(BF16))
    u = _dot(a, wu_ref[...].astype(BF16))
    o_ref[...] = (g * jax.nn.sigmoid(g) * u).astype(o_ref.dtype)


def _ffn_up(h, norm_g, w_gate_up, layer, tm=1024, tn=512):
    s, k = h.shape
    d_ff = w_gate_up.shape[2] // 2
    nj = d_ff // tn
    return pl.pallas_call(
        _ffn_up_kernel,
        grid=(s // tm, nj),
        in_specs=[pl.BlockSpec((tm, k), lambda i, j: (i, 0)), pl.BlockSpec((1, k), lambda i, j: (0, 0)),
                  pl.BlockSpec((None, k, tn), lambda i, j: (layer, 0, j)),
                  pl.BlockSpec((None, k, tn), lambda i, j: (layer, 0, nj + j))],
        out_specs=pl.BlockSpec((tm, tn), lambda i, j: (i, j)),
        out_shape=jax.ShapeDtypeStruct((s, d_ff), BF16),
        scratch_shapes=[pltpu.VMEM((tm, k), BF16)],
        compiler_params=_params(2),
        name="ffn_up",
    )(h, norm_g[None, :], w_gate_up, w_gate_up)


def _ffn(h, norm_g, w_gate_up, w_down, layer):
    act = _ffn_up(h, norm_g, w_gate_up, layer)
    return _matmul_residual(act, w_down[layer].astype(BF16)[None], 0, h, tm=1024, tn=512)


def kernel(x, positions, attn_norm, mla_w_in, mla_q_norm, mla_kv_norm, mla_w_uq, mla_w_ukv, mla_w_o,
           sb_w_in, sb_w_o, ffn_norm, ffn_w_gate_up, ffn_w_down, final_norm):
    b, s, d = x.shape
    assert b == 1 and d == D_MODEL
    h = x.reshape(s, d)

    attn_tile = 512
    tabs = _rope_tables(positions.reshape(s, 1))
    tabs_t = _rope_tables_t(positions.reshape(1, s))
    w_in_pad = jnp.pad(mla_w_in[0], ((0, 0), (0, LANES - QK_ROPE_DIM))).astype(BF16)
    qk_dim = QK_NOPE_DIM + QK_ROPE_DIM
    w_uq_pad = jnp.pad(mla_w_uq[0].reshape(Q_LORA_RANK, MLA_HEADS, qk_dim),
                       ((0, 0), (0, 0), (0, QK_PAD_DIM - qk_dim)))
    w_uq_pad_t = w_uq_pad.reshape(Q_LORA_RANK, MLA_HEADS * QK_PAD_DIM).T.astype(BF16)
    w_ukv = mla_w_ukv[0].reshape(KV_LORA_RANK, MLA_HEADS, QK_NOPE_DIM + V_HEAD_DIM)
    w_uk = w_ukv[:, :, :QK_NOPE_DIM].reshape(KV_LORA_RANK, MLA_HEADS * QK_NOPE_DIM).astype(BF16)
    w_uv_t = w_ukv[:, :, QK_NOPE_DIM:].reshape(KV_LORA_RANK, MLA_HEADS * V_HEAD_DIM).T.astype(BF16)

    cq, ckv, kr = _mla_in(h, attn_norm[0], w_in_pad, mla_q_norm[0], mla_kv_norm[0], tabs)
    q_t = _mla_q(cq, w_uq_pad_t, tabs_t, qk_dim ** -0.5 * LOG2_E, attn_tile)
    k, v_t = _mla_kv(ckv, w_uk, w_uv_t, kr, attn_tile)
    o = _mla_attn(q_t, k, v_t, attn_tile)
    h = _matmul_residual(o, mla_w_o[0].astype(BF16)[None], 0, h, tm=1024, tn=1024)
    h = _ffn(h, ffn_norm[0], ffn_w_gate_up, ffn_w_down, 0)

    n_q = SB_HEADS * SB_HEAD_DIM
    col_scale = jnp.concatenate([jnp.full((n_q,), SB_HEAD_DIM ** -0.5 * LOG2_E, F32), jnp.ones((2 * n_q,), F32)])
    qkv = _norm_matmul_scale(h, attn_norm[1], sb_w_in, 0, col_scale)
    o = _sb_attn(qkv)
    h = _matmul_residual(o, sb_w_o[0].astype(BF16)[None], 0, h, tm=1024, tn=1024)
    h = _ffn(h, ffn_norm[1], ffn_w_gate_up, ffn_w_down, 1)

    return _rmsnorm(h, final_norm, F32).reshape(b, s, d)
```

```python
import functools

import jax
import jax.numpy as jnp
from jax import lax
from jax.experimental import pallas as pl
from jax.experimental.pallas import tpu as pltpu

D_MODEL = 2048
MLA_HEADS = 16
Q_LORA_RANK = 512
KV_LORA_RANK = 512
QK_NOPE_DIM = 128
QK_ROPE_DIM = 64
V_HEAD_DIM = 128
ROPE_THETA = 10000.0
SB_HEADS = 16
SB_HEAD_DIM = 128
RMS_EPS = 1e-6

LANES = 128
QK_PAD_DIM = 2 * LANES
VMEM_LIMIT_BYTES = 56 * 1024 * 1024

LOG2_E = 1.4426950408889634
SB_LOG2_WEIGHT_FLOOR = -120.0 * LOG2_E

F32 = jnp.float32
BF16 = jnp.bfloat16


def _params(n_axes):
    return pltpu.CompilerParams(
        dimension_semantics=("arbitrary",) * n_axes, vmem_limit_bytes=VMEM_LIMIT_BYTES)


def _dot(a, b):
    return jnp.dot(a, b, preferred_element_type=F32)


def _dot_nt(a, b):
    return lax.dot_general(a, b, (((1,), (1,)), ((), ())), preferred_element_type=F32)


def _rms(x, g):
    return x * lax.rsqrt(jnp.mean(x * x, axis=-1, keepdims=True) + RMS_EPS) * g


def _rope(t, c, s1, s2):
    half = QK_ROPE_DIM // 2
    return t * c + pltpu.roll(t, half, 1) * s1 + pltpu.roll(t, LANES - half, 1) * s2


def _rope_tables_kernel(pos_ref, freq_ref, c_ref, s1_ref, s2_ref):
    half = QK_ROPE_DIM // 2
    ang = pos_ref[...].astype(F32) * freq_ref[...]
    lane = lax.broadcasted_iota(jnp.int32, ang.shape, 1)
    cos, sin = jnp.cos(ang), jnp.sin(ang)
    c_ref[...] = jnp.where(lane < 2 * half, cos, 0.0)
    s1_ref[...] = jnp.where((lane >= half) & (lane < 2 * half), sin, 0.0)
    s2_ref[...] = jnp.where(lane < half, -sin, 0.0)


def _rope_tables(positions, tm=2048):
    s = positions.shape[0]
    inv_freq = ROPE_THETA ** (-jnp.arange(0, QK_ROPE_DIM, 2, dtype=F32) / QK_ROPE_DIM)
    freq = jnp.concatenate([inv_freq, inv_freq, jnp.zeros((LANES - QK_ROPE_DIM,), F32)])[None, :]
    out = jax.ShapeDtypeStruct((s, LANES), F32)
    row = pl.BlockSpec((tm, LANES), lambda i: (i, 0))
    return pl.pallas_call(
        _rope_tables_kernel,
        grid=(s // tm,),
        in_specs=[pl.BlockSpec((tm, 1), lambda i: (i, 0)), pl.BlockSpec((1, LANES), lambda i: (0, 0))],
        out_specs=[row, row, row],
        out_shape=[out, out, out],
        compiler_params=_params(1),
        name="rope_tables",
    )(positions, freq)


def _rope_tables_t_kernel(pos_ref, freq_ref, c_ref, s1_ref, s2_ref):
    half = QK_ROPE_DIM // 2
    ang = freq_ref[...] * pos_ref[...].astype(F32)
    r = lax.broadcasted_iota(jnp.int32, ang.shape, 0)
    cos, sin = jnp.cos(ang), jnp.sin(ang)
    c_ref[...] = jnp.where(r < 2 * half, cos, 0.0)
    s1_ref[...] = jnp.where((r >= half) & (r < 2 * half), sin, 0.0)
    s2_ref[...] = jnp.where(r < half, -sin, 0.0)


def _rope_tables_t(positions, tm=2048):
    s = positions.shape[1]
    inv_freq = ROPE_THETA ** (-jnp.arange(0, QK_ROPE_DIM, 2, dtype=F32) / QK_ROPE_DIM)
    freq = jnp.concatenate([inv_freq, inv_freq, jnp.zeros((LANES - QK_ROPE_DIM,), F32)])[:, None]
    out = jax.ShapeDtypeStruct((LANES, s), F32)
    col = pl.BlockSpec((LANES, tm), lambda i: (0, i))
    return pl.pallas_call(
        _rope_tables_t_kernel,
        grid=(s // tm,),
        in_specs=[pl.BlockSpec((1, tm), lambda i: (0, i)), pl.BlockSpec((LANES, 1), lambda i: (0, 0))],
        out_specs=[col, col, col],
        out_shape=[out, out, out],
        compiler_params=_params(1),
        name="rope_tables_t",
    )(positions, freq)


def _norm_kernel(x_ref, g_ref, o_ref):
    o_ref[...] = _rms(x_ref[...], g_ref[...]).astype(o_ref.dtype)


def _rmsnorm(x, g, out_dtype, tm=512):
    s, d = x.shape
    return pl.pallas_call(
        _norm_kernel,
        grid=(s // tm,),
        in_specs=[pl.BlockSpec((tm, d), lambda i: (i, 0)), pl.BlockSpec((1, d), lambda i: (0, 0))],
        out_specs=pl.BlockSpec((tm, d), lambda i: (i, 0)),
        out_shape=jax.ShapeDtypeStruct((s, d), out_dtype),
        compiler_params=_params(1),
        name="rmsnorm",
    )(x, g[None, :])


def _mla_in_kernel(h_ref, g_ref, w_ref, qn_ref, kvn_ref, c_ref, s1_ref, s2_ref, cq_ref, ckv_ref, kr_ref):
    a = _rms(h_ref[...], g_ref[...]).astype(BF16)
    proj = _dot(a, w_ref[...])
    cq_ref[...] = _rms(proj[:, :Q_LORA_RANK], qn_ref[...]).astype(BF16)
    ckv_ref[...] = _rms(proj[:, Q_LORA_RANK:Q_LORA_RANK + KV_LORA_RANK], kvn_ref[...]).astype(BF16)
    kr = proj[:, Q_LORA_RANK + KV_LORA_RANK:]
    kr_ref[...] = _rope(kr, c_ref[...], s1_ref[...], s2_ref[...]).astype(BF16)


def _mla_in(h, norm_g, w_pad, q_norm, kv_norm, tabs, tm=1024):
    s, d = h.shape
    n = w_pad.shape[1]
    row = lambda w: pl.BlockSpec((tm, w), lambda i: (i, 0))
    const = lambda r, w: pl.BlockSpec((r, w), lambda i: (0, 0))
    return pl.pallas_call(
        _mla_in_kernel,
        grid=(s // tm,),
        in_specs=[row(d), const(1, d), const(d, n), const(1, Q_LORA_RANK), const(1, KV_LORA_RANK),
                  row(LANES), row(LANES), row(LANES)],
        out_specs=[row(Q_LORA_RANK), row(KV_LORA_RANK), row(LANES)],
        out_shape=[jax.ShapeDtypeStruct((s, Q_LORA_RANK), BF16),
                   jax.ShapeDtypeStruct((s, KV_LORA_RANK), BF16),
                   jax.ShapeDtypeStruct((s, LANES), BF16)],
        compiler_params=_params(1),
        name="mla_in",
    )(h, norm_g[None, :], w_pad, q_norm[None, :], kv_norm[None, :], *tabs)


def _mla_q_kernel(w_ref, cq_ref, c_ref, s1_ref, s2_ref, q_ref, *, heads, scale):
    acc = _dot_nt(w_ref[...], cq_ref[...])
    c, s1, s2 = c_ref[...], s1_ref[...], s2_ref[...]
    half = QK_ROPE_DIM // 2
    tq = q_ref.shape[3]
    for h in range(heads):
        lo = h * QK_PAD_DIM
        nope = (acc[lo:lo + LANES, :] * scale).astype(BF16)
        t = acc[lo + LANES:lo + QK_PAD_DIM, :]
        down = jnp.concatenate([t[LANES - half:], t[:LANES - half]], axis=0)
        up = jnp.concatenate([t[half:], t[:half]], axis=0)
        rope = ((t * c + down * s1 + up * s2) * scale).astype(BF16)
        for tt in range(acc.shape[1] // tq):
            q_ref[h, tt, :LANES, :] = nope[:, tt * tq:(tt + 1) * tq]
            q_ref[h, tt, LANES:, :] = rope[:, tt * tq:(tt + 1) * tq]


def _mla_q(cq, w_uq_pad_t, tabs_t, scale, t, tm=1024, heads_per_tile=8):
    s, r = cq.shape
    n = w_uq_pad_t.shape[0]
    tn = heads_per_tile * QK_PAD_DIM
    col = pl.BlockSpec((LANES, tm), lambda j, i: (0, i))
    return pl.pallas_call(
        functools.partial(_mla_q_kernel, heads=heads_per_tile, scale=scale),
        grid=(n // tn, s // tm),
        in_specs=[pl.BlockSpec((tn, r), lambda j, i: (j, 0)), pl.BlockSpec((tm, r), lambda j, i: (i, 0)),
                  col, col, col],
        out_specs=pl.BlockSpec((heads_per_tile, tm // t, QK_PAD_DIM, t), lambda j, i: (j, i, 0, 0)),
        out_shape=jax.ShapeDtypeStruct((n // QK_PAD_DIM, s // t, QK_PAD_DIM, t), BF16),
        compiler_params=_params(2),
        name="mla_q",
    )(w_uq_pad_t, cq, *tabs_t)


V_ROWS = V_HEAD_DIM + 16


def _mla_kv_kernel(ckv_ref, wk_ref, wv_ref, kr_ref, k_ref, v_ref, *, heads, t):
    ckv = ckv_ref[...]
    kn = _dot(ckv, wk_ref[...])
    vt = _dot_nt(wv_ref[...], ckv)
    kr = kr_ref[...]
    ones = jnp.ones((V_ROWS - V_HEAD_DIM, t), BF16)
    for h in range(heads):
        k_ref[:, h * QK_PAD_DIM:h * QK_PAD_DIM + LANES] = kn[:, h * LANES:(h + 1) * LANES].astype(BF16)
        k_ref[:, h * QK_PAD_DIM + LANES:(h + 1) * QK_PAD_DIM] = kr
        for tt in range(ckv.shape[0] // t):
            v_ref[h, tt, :V_HEAD_DIM, :] = vt[h * V_HEAD_DIM:(h + 1) * V_HEAD_DIM, tt * t:(tt + 1) * t].astype(BF16)
            v_ref[h, tt, V_HEAD_DIM:, :] = ones


def _mla_kv(ckv, w_uk, w_uv_t, kr, t, tm=1024, heads_per_tile=8):
    s, r = ckv.shape
    n = w_uk.shape[1]
    heads = n // LANES
    tn = heads_per_tile * LANES
    return pl.pallas_call(
        functools.partial(_mla_kv_kernel, heads=heads_per_tile, t=t),
        grid=(s // tm, n // tn),
        in_specs=[pl.BlockSpec((tm, r), lambda i, j: (i, 0)),
                  pl.BlockSpec((r, tn), lambda i, j: (0, j)),
                  pl.BlockSpec((tn, r), lambda i, j: (j, 0)),
                  pl.BlockSpec((tm, LANES), lambda i, j: (i, 0))],
        out_specs=[pl.BlockSpec((tm, 2 * tn), lambda i, j: (i, j)),
                   pl.BlockSpec((heads_per_tile, tm // t, V_ROWS, t), lambda i, j: (j, i, 0, 0))],
        out_shape=[jax.ShapeDtypeStruct((s, 2 * n), BF16),
                   jax.ShapeDtypeStruct((heads, s // t, V_ROWS, t), BF16)],
        compiler_params=_params(2),
        name="mla_kv",
    )(ckv, w_uk, w_uv_t, kr)


def _mla_attn_kernel(q_ref, k_ref, v_ref, o_ref, m_sc, acc_sc, s_even, s_odd, *, t, unrolls):
    def q_tile(qi, carry):
        _mla_attn_q_tile(pl.program_id(1) * q_ref.shape[0] + qi, q_ref[qi], k_ref, v_ref,
                         o_ref.at[pl.ds(pl.multiple_of(qi * t, t), t), :],
                         m_sc, acc_sc, s_even, s_odd, t=t, unrolls=unrolls)
        return carry

    lax.fori_loop(0, q_ref.shape[0], q_tile, 0)


def _mla_attn_q_tile(i, q, k_ref, v_ref, o_ref, m_sc, acc_sc, s_even, s_odd, *, t, unrolls):
    m_sc[...] = jnp.full(m_sc.shape, -jnp.inf, F32)
    acc_sc[...] = jnp.zeros(acc_sc.shape, F32)

    def scores(j):
        return _dot(k_ref[pl.ds(pl.multiple_of(j * t, t), t), :], q)

    def softmax_pv(s_ref, j, diagonal):
        def load():
            s = s_ref[...]
            if diagonal:
                key = lax.broadcasted_iota(jnp.int32, (t, t), 0)
                query = lax.broadcasted_iota(jnp.int32, (t, t), 1)
                s = jnp.where(key <= query, s, -jnp.inf)
            return s
        m_prev = m_sc[...]
        m_next = jnp.maximum(m_prev, jnp.max(load(), axis=0, keepdims=True))
        m_sc[...] = m_next
        p = jnp.exp2(load() - m_sc[...]).astype(BF16)
        alpha = jnp.exp2(m_prev - m_next)
        acc_sc[...] = alpha * acc_sc[...] + _dot(v_ref[j], p)

    s_even[...] = scores(0)

    def pairs(first, n_pairs):
        for d in range(n_pairs):
            j = first + 2 * d
            s_odd[...] = scores(j + 1)
            softmax_pv(s_even, j, False)
            s_even[...] = scores(j + 2)
            softmax_pv(s_odd, j + 1, False)

    done = 0
    for u in unrolls:
        n_groups = (i // 2 - done) // u

        def group(g, carry, u=u, done=done):
            pairs(2 * (done + g * u), u)
            return carry

        lax.fori_loop(0, n_groups, group, 0)
        done = done + n_groups * u

    @pl.when(i % 2 == 0)
    def _():
        softmax_pv(s_even, i, True)

    @pl.when(i % 2 == 1)
    def _():
        s_odd[...] = scores(i)
        softmax_pv(s_even, i - 1, False)
        softmax_pv(s_odd, i, True)

    out_t = acc_sc[:V_HEAD_DIM, :] / acc_sc[V_HEAD_DIM:V_HEAD_DIM + 1, :]
    o_ref[...] = out_t.T.astype(o_ref.dtype)


def _mla_attn(q_t, k, v_t, t, q_tiles_per_step=4, unrolls=(4, 2, 1)):
    s = k.shape[0]
    h = MLA_HEADS
    g = q_tiles_per_step
    assert unrolls[-1] == 1
    return pl.pallas_call(
        functools.partial(_mla_attn_kernel, t=t, unrolls=unrolls),
        grid=(h, s // (t * g)),
        in_specs=[pl.BlockSpec((None, g, QK_PAD_DIM, t), lambda hh, i: (hh, i, 0, 0)),
                  pl.BlockSpec((s, QK_PAD_DIM), lambda hh, i: (0, hh)),
                  pl.BlockSpec((None, s // t, V_ROWS, t), lambda hh, i: (hh, 0, 0, 0))],
        out_specs=pl.BlockSpec((g * t, V_HEAD_DIM), lambda hh, i: (i, hh)),
        out_shape=jax.ShapeDtypeStruct((s, h * V_HEAD_DIM), BF16),
        scratch_shapes=[pltpu.VMEM((1, t), F32),
                        pltpu.VMEM((V_ROWS, t), F32),
                        pltpu.VMEM((t, t), F32), pltpu.VMEM((t, t), F32)],
        compiler_params=_params(2),
        name="mla_attn",
    )(q_t, k, v_t)


def _sb_attn_kernel(q_ref, k_ref, v_ref, o_ref, tri_sc, *bufs, t, group):
    n_tiles = q_ref.shape[0] // t
    n_groups = n_tiles // group
    w = 2 * t
    kinds = 6
    per_kind = 2 * group
    lom_sc, lb_sc, a_sc, wsum_sc, carry_sc, acc_sc = [
        [bufs[kind * per_kind + slot * group:kind * per_kind + (slot + 1) * group] for slot in range(2)]
        for kind in range(kinds)]
    tri_sc[...] = jnp.where(lax.broadcasted_iota(jnp.int32, (w, w), 0)
                            > lax.broadcasted_iota(jnp.int32, (w, w), 1), -1.0, 0.0).astype(BF16)

    def log_probs(q, start, width):
        z = _dot_nt(q, k_ref[pl.ds(start, width), :])
        neg_log_om = jnp.maximum(z, 0.0) + jnp.log2(1.0 + jnp.exp2(-jnp.abs(z)))
        return neg_log_om, z - neg_log_om

    def window_start(i):
        return pl.multiple_of(jnp.maximum(i - 1, 0) * t, t)

    def window_logits(slot, n, i, first_tile=False):
        i = jnp.minimum(i, n_tiles - 1)
        q = q_ref[pl.ds(pl.multiple_of(i * t, t), t), :]
        neg_log_om, log_beta = log_probs(q, window_start(i), w)
        lom, lb, wsum = lom_sc[slot][n], lb_sc[slot][n], wsum_sc[slot][n]
        strict = (lax.broadcasted_iota(jnp.int32, (t, t), 1)
                  < lax.broadcasted_iota(jnp.int32, (t, t), 0))
        if first_tile:
            halves = ((strict, neg_log_om[:, :t], log_beta[:, :t]), (None, None, None))
        else:
            halves = ((True, neg_log_om[:, :t], log_beta[:, :t]), (strict, neg_log_om[:, t:], log_beta[:, t:]))
        total = jnp.zeros((t, 1), F32)
        for half, (mask, nlo, lbeta) in enumerate(halves):
            cols = slice(half * t, (half + 1) * t)
            if mask is None:
                lom[:, cols] = jnp.zeros((t, t), BF16)
                lb[:, cols] = jnp.full((t, t), -jnp.inf, F32)
                continue
            if mask is not True:
                nlo = jnp.where(mask, nlo, 0.0)
                lbeta = jnp.where(mask, lbeta, -jnp.inf)
            lom[:, cols] = nlo.astype(BF16)
            lb[:, cols] = lbeta
            total = total + jnp.sum(nlo, axis=1, keepdims=True)
        wsum[...] = jnp.broadcast_to(-total, (t, LANES))

    def window_weights(slot, n):
        after = _dot(lom_sc[slot][n][...], tri_sc[...])
        a_sc[slot][n][...] = jnp.exp2(lb_sc[slot][n][...] + after).astype(BF16)
        carry_sc[slot][n][...] = wsum_sc[slot][n][...]

    def window_values(slot, n, i):
        acc_sc[slot][n][...] = _dot(a_sc[slot][n][...], v_ref[pl.ds(window_start(i), w), :])

    def earlier_tiles(slot, n, i):
        carry, acc = carry_sc[slot][n], acc_sc[slot][n]
        rows = pl.ds(pl.multiple_of(i * t, t), t)
        q = q_ref[rows, :]

        def more(state):
            j, carry_max = state
            return (j >= 0) & (carry_max > SB_LOG2_WEIGHT_FLOOR)

        def body(state):
            j, _ = state
            start = pl.multiple_of(j * t, t)
            neg_log_om, log_beta = log_probs(q, start, t)
            after = _dot(neg_log_om.astype(BF16), tri_sc[:t, :t]) + jnp.tile(carry[...], (1, t // LANES))
            a = jnp.exp2(log_beta + after)
            acc[...] += _dot(a.astype(BF16), v_ref[pl.ds(start, t), :])
            carry[...] -= jnp.sum(neg_log_om, axis=1, keepdims=True)
            return j - 1, jnp.max(carry[...])

        lax.while_loop(more, body, (i - 2, jnp.max(carry[...])))
        o_ref[rows, :] = acc[...].astype(o_ref.dtype)

    def step(g, slot):
        for n in range(group):
            window_logits(1 - slot, n, (g + 1) * group + n)
        for n in range(group):
            window_weights(slot, n)
        for n in range(group):
            window_values(1 - slot, n, (g - 1) * group + n)
        for n in range(group):
            earlier_tiles(1 - slot, n, (g - 1) * group + n)

    def step_pair(p, _):
        step(2 * p + 1, 1)
        step(2 * p + 2, 0)
        return 0

    for n in range(group):
        window_logits(0, n, n, first_tile=(n == 0))
    for n in range(group):
        window_weights(0, n)
    for n in range(group):
        window_logits(1, n, group + n)
    lax.fori_loop(0, n_groups // 2, step_pair, 0)


def _sb_attn(qkv, t=256, group=1):
    s = qkv.shape[0]
    h, d = SB_HEADS, SB_HEAD_DIM
    assert s % (2 * group * t) == 0
    per_kind = 2 * group
    return pl.pallas_call(
        functools.partial(_sb_attn_kernel, t=t, group=group),
        grid=(h,),
        in_specs=[pl.BlockSpec((s, d), lambda hh: (0, hh)),
                  pl.BlockSpec((s, d), lambda hh: (0, h + hh)),
                  pl.BlockSpec((s, d), lambda hh: (0, 2 * h + hh))],
        out_specs=pl.BlockSpec((s, d), lambda hh: (0, hh)),
        out_shape=jax.ShapeDtypeStruct((s, h * d), BF16),
        scratch_shapes=[pltpu.VMEM((2 * t, 2 * t), BF16)]
        + per_kind * [pltpu.VMEM((t, 2 * t), BF16)]
        + per_kind * [pltpu.VMEM((t, 2 * t), F32)]
        + per_kind * [pltpu.VMEM((t, 2 * t), BF16)]
        + per_kind * [pltpu.VMEM((t, LANES), F32)]
        + per_kind * [pltpu.VMEM((t, LANES), F32)]
        + per_kind * [pltpu.VMEM((t, d), F32)],
        compiler_params=_params(1),
        name="sb_attn",
    )(qkv, qkv, qkv)


def _normalize_rows_once(h_ref, g_ref, a_sc):
    @pl.when(pl.program_id(1) == 0)
    def _():
        a_sc[...] = _rms(h_ref[...], g_ref[...]).astype(BF16)


def _norm_matmul_scale_kernel(h_ref, g_ref, w_ref, cs_ref, o_ref, a_sc):
    _normalize_rows_once(h_ref, g_ref, a_sc)
    o_ref[...] = (_dot(a_sc[...], w_ref[...].astype(BF16)) * cs_ref[...]).astype(o_ref.dtype)


def _norm_matmul_scale(h, norm_g, w, layer, col_scale, tm=1024, tn=1024):
    s, k = h.shape
    n = w.shape[2]
    return pl.pallas_call(
        _norm_matmul_scale_kernel,
        grid=(s // tm, n // tn),
        in_specs=[pl.BlockSpec((tm, k), lambda i, j: (i, 0)), pl.BlockSpec((1, k), lambda i, j: (0, 0)),
                  pl.BlockSpec((None, k, tn), lambda i, j: (layer, 0, j)),
                  pl.BlockSpec((1, tn), lambda i, j: (0, j))],
        out_specs=pl.BlockSpec((tm, tn), lambda i, j: (i, j)),
        out_shape=jax.ShapeDtypeStruct((s, n), BF16),
        scratch_shapes=[pltpu.VMEM((tm, k), BF16)],
        compiler_params=_params(2),
        name="norm_matmul_scale",
    )(h, norm_g[None, :], w, col_scale[None, :])


def _matmul_residual_kernel(a_ref, w_ref, r_ref, o_ref):
    o_ref[...] = r_ref[...] + _dot(a_ref[...], w_ref[...].astype(BF16))


def _matmul_residual(a, w, layer, res, tm, tn):
    s, k = a.shape
    n = w.shape[2]
    return pl.pallas_call(
        _matmul_residual_kernel,
        grid=(s // tm, n // tn),
        in_specs=[pl.BlockSpec((tm, k), lambda i, j: (i, 0)),
                  pl.BlockSpec((None, k, tn), lambda i, j: (layer, 0, j)),
                  pl.BlockSpec((tm, tn), lambda i, j: (i, j))],
        out_specs=pl.BlockSpec((tm, tn), lambda i, j: (i, j)),
        out_shape=jax.ShapeDtypeStruct((s, n), F32),
        compiler_params=_params(2),
        name="matmul_residual",
    )(a, w, res)


def _ffn_up_kernel(h_ref, g_ref, wg_ref, wu_ref, o_ref, a_sc):
    _normalize_rows_once(h_ref, g_ref, a_sc)
    a = a_sc[...]
    g = _dot(a, wg_ref[...].astype(BF16))
    u = _dot(a, wu_ref[...].astype(BF16))
    o_ref[...] = (g * jax.nn.sigmoid(g) * u).astype(o_ref.dtype)


def _ffn_up(h, norm_g, w_gate_up, layer, tm=1024, tn=512):
    s, k = h.shape
    d_ff = w_gate_up.shape[2] // 2
    nj = d_ff // tn
    return pl.pallas_call(
        _ffn_up_kernel,
        grid=(s // tm, nj),
        in_specs=[pl.BlockSpec((tm, k), lambda i, j: (i, 0)), pl.BlockSpec((1, k), lambda i, j: (0, 0)),
                  pl.BlockSpec((None, k, tn), lambda i, j: (layer, 0, j)),
                  pl.BlockSpec((None, k, tn), lambda i, j: (layer, 0, nj + j))],
        out_specs=pl.BlockSpec((tm, tn), lambda i, j: (i, j)),
        out_shape=jax.ShapeDtypeStruct((s, d_ff), BF16),
        scratch_shapes=[pltpu.VMEM((tm, k), BF16)],
        compiler_params=_params(2),
        name="ffn_up",
    )(h, norm_g[None, :], w_gate_up, w_gate_up)


def _ffn(h, norm_g, w_gate_up, w_down, layer):
    act = _ffn_up(h, norm_g, w_gate_up, layer)
    return _matmul_residual(act, w_down[layer].astype(BF16)[None], 0, h, tm=1024, tn=512)


def kernel(x, positions, attn_norm, mla_w_in, mla_q_norm, mla_kv_norm, mla_w_uq, mla_w_ukv, mla_w_o,
           sb_w_in, sb_w_o, ffn_norm, ffn_w_gate_up, ffn_w_down, final_norm):
    b, s, d = x.shape
    assert b == 1 and d == D_MODEL
    h = x.reshape(s, d)

    attn_tile = 512
    tabs = _rope_tables(positions.reshape(s, 1))
    tabs_t = _rope_tables_t(positions.reshape(1, s))
    w_in_pad = jnp.pad(mla_w_in[0], ((0, 0), (0, LANES - QK_ROPE_DIM))).astype(BF16)
    qk_dim = QK_NOPE_DIM + QK_ROPE_DIM
    w_uq_pad = jnp.pad(mla_w_uq[0].reshape(Q_LORA_RANK, MLA_HEADS, qk_dim),
                       ((0, 0), (0, 0), (0, QK_PAD_DIM - qk_dim)))
    w_uq_pad_t = w_uq_pad.reshape(Q_LORA_RANK, MLA_HEADS * QK_PAD_DIM).T.astype(BF16)
    w_ukv = mla_w_ukv[0].reshape(KV_LORA_RANK, MLA_HEADS, QK_NOPE_DIM + V_HEAD_DIM)
    w_uk = w_ukv[:, :, :QK_NOPE_DIM].reshape(KV_LORA_RANK, MLA_HEADS * QK_NOPE_DIM).astype(BF16)
    w_uv_t = w_ukv[:, :, QK_NOPE_DIM:].reshape(KV_LORA_RANK, MLA_HEADS * V_HEAD_DIM).T.astype(BF16)

    cq, ckv, kr = _mla_in(h, attn_norm[0], w_in_pad, mla_q_norm[0], mla_kv_norm[0], tabs)
    q_t = _mla_q(cq, w_uq_pad_t, tabs_t, qk_dim ** -0.5 * LOG2_E, attn_tile)
    k, v_t = _mla_kv(ckv, w_uk, w_uv_t, kr, attn_tile)
    o = _mla_attn(q_t, k, v_t, attn_tile)
    h = _matmul_residual(o, mla_w_o[0].astype(BF16)[None], 0, h, tm=1024, tn=1024)
    h = _ffn(h, ffn_norm[0], ffn_w_gate_up, ffn_w_down, 0)

    n_q = SB_HEADS * SB_HEAD_DIM
    col_scale = jnp.concatenate([jnp.full((n_q,), SB_HEAD_DIM ** -0.5 * LOG2_E, F32), jnp.ones((2 * n_q,), F32)])
    qkv = _norm_matmul_scale(h, attn_norm[1], sb_w_in, 0, col_scale)
    o = _sb_attn(qkv)
    h = _matmul_residual(o, sb_w_o[0].astype(BF16)[None], 0, h, tm=1024, tn=1024)
    h = _ffn(h, ffn_norm[1], ffn_w_gate_up, ffn_w_down, 1)

    return _rmsnorm(h, final_norm, F32).reshape(b, s, d)
```

```python
import functools

import jax
import jax.numpy as jnp
from jax import lax
from jax.experimental import pallas as pl
from jax.experimental.pallas import tpu as pltpu

D_MODEL = 2048
MLA_HEADS = 16
Q_LORA_RANK = 512
KV_LORA_RANK = 512
QK_NOPE_DIM = 128
QK_ROPE_DIM = 64
V_HEAD_DIM = 128
ROPE_THETA = 10000.0
SB_HEADS = 16
SB_HEAD_DIM = 128
RMS_EPS = 1e-6

LANES = 128
QK_PAD_DIM = 2 * LANES
VMEM_LIMIT_BYTES = 56 * 1024 * 1024

LOG2_E = 1.4426950408889634
SB_LOG2_WEIGHT_FLOOR = -120.0 * LOG2_E

F32 = jnp.float32
BF16 = jnp.bfloat16


def _params(n_axes):
    return pltpu.CompilerParams(
        dimension_semantics=("arbitrary",) * n_axes, vmem_limit_bytes=VMEM_LIMIT_BYTES)


def _dot(a, b):
    return jnp.dot(a, b, preferred_element_type=F32)


def _dot_nt(a, b):
    return lax.dot_general(a, b, (((1,), (1,)), ((), ())), preferred_element_type=F32)


def _rms(x, g):
    return x * lax.rsqrt(jnp.mean(x * x, axis=-1, keepdims=True) + RMS_EPS) * g


def _rope(t, c, s1, s2):
    half = QK_ROPE_DIM // 2
    return t * c + pltpu.roll(t, half, 1) * s1 + pltpu.roll(t, LANES - half, 1) * s2


def _rope_tables_kernel(pos_ref, freq_ref, c_ref, s1_ref, s2_ref):
    half = QK_ROPE_DIM // 2
    ang = pos_ref[...].astype(F32) * freq_ref[...]
    lane = lax.broadcasted_iota(jnp.int32, ang.shape, 1)
    cos, sin = jnp.cos(ang), jnp.sin(ang)
    c_ref[...] = jnp.where(lane < 2 * half, cos, 0.0)
    s1_ref[...] = jnp.where((lane >= half) & (lane < 2 * half), sin, 0.0)
    s2_ref[...] = jnp.where(lane < half, -sin, 0.0)


def _rope_tables(positions, tm=2048):
    s = positions.shape[0]
    inv_freq = ROPE_THETA ** (-jnp.arange(0, QK_ROPE_DIM, 2, dtype=F32) / QK_ROPE_DIM)
    freq = jnp.concatenate([inv_freq, inv_freq, jnp.zeros((LANES - QK_ROPE_DIM,), F32)])[None, :]
    out = jax.ShapeDtypeStruct((s, LANES), F32)
    row = pl.BlockSpec((tm, LANES), lambda i: (i, 0))
    return pl.pallas_call(
        _rope_tables_kernel,
        grid=(s // tm,),
        in_specs=[pl.BlockSpec((tm, 1), lambda i: (i, 0)), pl.BlockSpec((1, LANES), lambda i: (0, 0))],
        out_specs=[row, row, row],
        out_shape=[out, out, out],
        compiler_params=_params(1),
        name="rope_tables",
    )(positions, freq)


def _rope_tables_t_kernel(pos_ref, freq_ref, c_ref, s1_ref, s2_ref):
    half = QK_ROPE_DIM // 2
    ang = freq_ref[...] * pos_ref[...].astype(F32)
    r = lax.broadcasted_iota(jnp.int32, ang.shape, 0)
    cos, sin = jnp.cos(ang), jnp.sin(ang)
    c_ref[...] = jnp.where(r < 2 * half, cos, 0.0)
    s1_ref[...] = jnp.where((r >= half) & (r < 2 * half), sin, 0.0)
    s2_ref[...] = jnp.where(r < half, -sin, 0.0)


def _rope_tables_t(positions, tm=2048):
    s = positions.shape[1]
    inv_freq = ROPE_THETA ** (-jnp.arange(0, QK_ROPE_DIM, 2, dtype=F32) / QK_ROPE_DIM)
    freq = jnp.concatenate([inv_freq, inv_freq, jnp.zeros((LANES - QK_ROPE_DIM,), F32)])[:, None]
    out = jax.ShapeDtypeStruct((LANES, s), F32)
    col = pl.BlockSpec((LANES, tm), lambda i: (0, i))
    return pl.pallas_call(
        _rope_tables_t_kernel,
        grid=(s // tm,),
        in_specs=[pl.BlockSpec((1, tm), lambda i: (0, i)), pl.BlockSpec((LANES, 1), lambda i: (0, 0))],
        out_specs=[col, col, col],
        out_shape=[out, out, out],
        compiler_params=_params(1),
        name="rope_tables_t",
    )(positions, freq)


def _norm_kernel(x_ref, g_ref, o_ref):
    o_ref[...] = _rms(x_ref[...], g_ref[...]).astype(o_ref.dtype)


def _rmsnorm(x, g, out_dtype, tm=512):
    s, d = x.shape
    return pl.pallas_call(
        _norm_kernel,
        grid=(s // tm,),
        in_specs=[pl.BlockSpec((tm, d), lambda i: (i, 0)), pl.BlockSpec((1, d), lambda i: (0, 0))],
        out_specs=pl.BlockSpec((tm, d), lambda i: (i, 0)),
        out_shape=jax.ShapeDtypeStruct((s, d), out_dtype),
        compiler_params=_params(1),
        name="rmsnorm",
    )(x, g[None, :])


def _mla_in_kernel(h_ref, g_ref, w_ref, qn_ref, kvn_ref, c_ref, s1_ref, s2_ref, cq_ref, ckv_ref, kr_ref):
    a = _rms(h_ref[...], g_ref[...]).astype(BF16)
    proj = _dot(a, w_ref[...])
    cq_ref[...] = _rms(proj[:, :Q_LORA_RANK], qn_ref[...]).astype(BF16)
    ckv_ref[...] = _rms(proj[:, Q_LORA_RANK:Q_LORA_RANK + KV_LORA_RANK], kvn_ref[...]).astype(BF16)
    kr = proj[:, Q_LORA_RANK + KV_LORA_RANK:]
    kr_ref[...] = _rope(kr, c_ref[...], s1_ref[...], s2_ref[...]).astype(BF16)


def _mla_in(h, norm_g, w_pad, q_norm, kv_norm, tabs, tm=1024):
    s, d = h.shape
    n = w_pad.shape[1]
    row = lambda w: pl.BlockSpec((tm, w), lambda i: (i, 0))
    const = lambda r, w: pl.BlockSpec((r, w), lambda i: (0, 0))
    return pl.pallas_call(
        _mla_in_kernel,
        grid=(s // tm,),
        in_specs=[row(d), const(1, d), const(d, n), const(1, Q_LORA_RANK), const(1, KV_LORA_RANK),
                  row(LANES), row(LANES), row(LANES)],
        out_specs=[row(Q_LORA_RANK), row(KV_LORA_RANK), row(LANES)],
        out_shape=[jax.ShapeDtypeStruct((s, Q_LORA_RANK), BF16),
                   jax.ShapeDtypeStruct((s, KV_LORA_RANK), BF16),
                   jax.ShapeDtypeStruct((s, LANES), BF16)],
        compiler_params=_params(1),
        name="mla_in",
    )(h, norm_g[None, :], w_pad, q_norm[None, :], kv_norm[None, :], *tabs)


def _mla_q_kernel(w_ref, cq_ref, c_ref, s1_ref, s2_ref, q_ref, *, heads, scale):
    acc = _dot_nt(w_ref[...], cq_ref[...])
    c, s1, s2 = c_ref[...], s1_ref[...], s2_ref[...]
    half = QK_ROPE_DIM // 2
    tq = q_ref.shape[3]
    for h in range(heads):
        lo = h * QK_PAD_DIM
        nope = (acc[lo:lo + LANES, :] * scale).astype(BF16)
        t = acc[lo + LANES:lo + QK_PAD_DIM, :]
        down = jnp.concatenate([t[LANES - half:], t[:LANES - half]], axis=0)
        up = jnp.concatenate([t[half:], t[:half]], axis=0)
        rope = ((t * c + down * s1 + up * s2) * scale).astype(BF16)
        for tt in range(acc.shape[1] // tq):
            q_ref[h, tt, :LANES, :] = nope[:, tt * tq:(tt + 1) * tq]
            q_ref[h, tt, LANES:, :] = rope[:, tt * tq:(tt + 1) * tq]


def _mla_q(cq, w_uq_pad_t, tabs_t, scale, t, tm=1024, heads_per_tile=8):
    s, r = cq.shape
    n = w_uq_pad_t.shape[0]
    tn = heads_per_tile * QK_PAD_DIM
    col = pl.BlockSpec((LANES, tm), lambda j, i: (0, i))
    return pl.pallas_call(
        functools.partial(_mla_q_kernel, heads=heads_per_tile, scale=scale),
        grid=(n // tn, s // tm),
        in_specs=[pl.BlockSpec((tn, r), lambda j, i: (j, 0)), pl.BlockSpec((tm, r), lambda j, i: (i, 0)),
                  col, col, col],
        out_specs=pl.BlockSpec((heads_per_tile, tm // t, QK_PAD_DIM, t), lambda j, i: (j, i, 0, 0)),
        out_shape=jax.ShapeDtypeStruct((n // QK_PAD_DIM, s // t, QK_PAD_DIM, t), BF16),
        compiler_params=_params(2),
        name="mla_q",
    )(w_uq_pad_t, cq, *tabs_t)


V_ROWS = V_HEAD_DIM + 16


def _mla_kv_kernel(ckv_ref, wk_ref, wv_ref, kr_ref, k_ref, v_ref, *, heads, t):
    ckv = ckv_ref[...]
    kn = _dot(ckv, wk_ref[...])
    vt = _dot_nt(wv_ref[...], ckv)
    kr = kr_ref[...]
    ones = jnp.ones((V_ROWS - V_HEAD_DIM, t), BF16)
    for h in range(heads):
        k_ref[:, h * QK_PAD_DIM:h * QK_PAD_DIM + LANES] = kn[:, h * LANES:(h + 1) * LANES].astype(BF16)
        k_ref[:, h * QK_PAD_DIM + LANES:(h + 1) * QK_PAD_DIM] = kr
        for tt in range(ckv.shape[0] // t):
            v_ref[h, tt, :V_HEAD_DIM, :] = vt[h * V_HEAD_DIM:(h + 1) * V_HEAD_DIM, tt * t:(tt + 1) * t].astype(BF16)
            v_ref[h, tt, V_HEAD_DIM:, :] = ones


def _mla_kv(ckv, w_uk, w_uv_t, kr, t, tm=1024, heads_per_tile=8):
    s, r = ckv.shape
    n = w_uk.shape[1]
    heads = n // LANES
    tn = heads_per_tile * LANES
    return pl.pallas_call(
        functools.partial(_mla_kv_kernel, heads=heads_per_tile, t=t),
        grid=(s // tm, n // tn),
        in_specs=[pl.BlockSpec((tm, r), lambda i, j: (i, 0)),
                  pl.BlockSpec((r, tn), lambda i, j: (0, j)),
                  pl.BlockSpec((tn, r), lambda i, j: (j, 0)),
                  pl.BlockSpec((tm, LANES), lambda i, j: (i, 0))],
        out_specs=[pl.BlockSpec((tm, 2 * tn), lambda i, j: (i, j)),
                   pl.BlockSpec((heads_per_tile, tm // t, V_ROWS, t), lambda i, j: (j, i, 0, 0))],
        out_shape=[jax.ShapeDtypeStruct((s, 2 * n), BF16),
                   jax.ShapeDtypeStruct((heads, s // t, V_ROWS, t), BF16)],
        compiler_params=_params(2),
        name="mla_kv",
    )(ckv, w_uk, w_uv_t, kr)


def _mla_attn_kernel(q_ref, k_ref, v_ref, *rest, t, unrolls, n_cast):
    w_refs, o_ref, w_bf16_refs = rest[:n_cast], rest[n_cast], rest[n_cast + 1:2 * n_cast + 1]
    m_sc, acc_sc, s_even, s_odd = rest[2 * n_cast + 1:]
    for w_ref, w_bf16_ref in zip(w_refs, w_bf16_refs):
        w_bf16_ref[...] = w_ref[...].astype(BF16)

    def q_tile(qi, carry):
        _mla_attn_q_tile(pl.program_id(1) * q_ref.shape[0] + qi, q_ref[qi], k_ref, v_ref,
                         o_ref.at[pl.ds(pl.multiple_of(qi * t, t), t), :],
                         m_sc, acc_sc, s_even, s_odd, t=t, unrolls=unrolls)
        return carry

    lax.fori_loop(0, q_ref.shape[0], q_tile, 0)


def _mla_attn_q_tile(i, q, k_ref, v_ref, o_ref, m_sc, acc_sc, s_even, s_odd, *, t, unrolls):
    m_sc[...] = jnp.full(m_sc.shape, -jnp.inf, F32)
    acc_sc[...] = jnp.zeros(acc_sc.shape, F32)

    def scores(j):
        return _dot(k_ref[pl.ds(pl.multiple_of(j * t, t), t), :], q)

    def softmax_pv(s_ref, j, diagonal):
        def load():
            s = s_ref[...]
            if diagonal:
                key = lax.broadcasted_iota(jnp.int32, (t, t), 0)
                query = lax.broadcasted_iota(jnp.int32, (t, t), 1)
                s = jnp.where(key <= query, s, -jnp.inf)
            return s
        m_prev = m_sc[...]
        m_next = jnp.maximum(m_prev, jnp.max(load(), axis=0, keepdims=True))
        m_sc[...] = m_next
        p = jnp.exp2(load() - m_sc[...]).astype(BF16)
        alpha = jnp.exp2(m_prev - m_next)
        acc_sc[...] = alpha * acc_sc[...] + _dot(v_ref[j], p)

    s_even[...] = scores(0)

    def pairs(first, n_pairs):
        for d in range(n_pairs):
            j = first + 2 * d
            s_odd[...] = scores(j + 1)
            softmax_pv(s_even, j, False)
            s_even[...] = scores(j + 2)
            softmax_pv(s_odd, j + 1, False)

    done = 0
    for u in unrolls:
        n_groups = (i // 2 - done) // u

        def group(g, carry, u=u, done=done):
            pairs(2 * (done + g * u), u)
            return carry

        lax.fori_loop(0, n_groups, group, 0)
        done = done + n_groups * u

    @pl.when(i % 2 == 0)
    def _():
        softmax_pv(s_even, i, True)

    @pl.when(i % 2 == 1)
    def _():
        s_odd[...] = scores(i)
        softmax_pv(s_even, i - 1, False)
        softmax_pv(s_odd, i, True)

    out_t = acc_sc[:V_HEAD_DIM, :] / acc_sc[V_HEAD_DIM:V_HEAD_DIM + 1, :]
    o_ref[...] = out_t.T.astype(o_ref.dtype)


BF16_SUBLANES = 16


def _mla_attn(q_t, k, v_t, t, cast_weights=(), q_tiles_per_step=4, unrolls=(4, 2, 1)):
    s = k.shape[0]
    h = MLA_HEADS
    g = q_tiles_per_step
    n_i = s // (t * g)
    n_steps = h * n_i
    assert unrolls[-1] == 1
    slab = lambda hh, i: (hh * n_i + i, 0)
    cast_specs = []
    for w in cast_weights:
        rows, cols = w.shape
        assert rows % (n_steps * BF16_SUBLANES) == 0, w.shape
        cast_specs.append(pl.BlockSpec((rows // n_steps, cols), slab))
    outs = pl.pallas_call(
        functools.partial(_mla_attn_kernel, t=t, unrolls=unrolls, n_cast=len(cast_weights)),
        grid=(h, n_i),
        in_specs=[pl.BlockSpec((None, g, QK_PAD_DIM, t), lambda hh, i: (hh, i, 0, 0)),
                  pl.BlockSpec((s, QK_PAD_DIM), lambda hh, i: (0, hh)),
                  pl.BlockSpec((None, s // t, V_ROWS, t), lambda hh, i: (hh, 0, 0, 0))] + cast_specs,
        out_specs=[pl.BlockSpec((g * t, V_HEAD_DIM), lambda hh, i: (i, hh))] + cast_specs,
        out_shape=[jax.ShapeDtypeStruct((s, h * V_HEAD_DIM), BF16)]
        + [jax.ShapeDtypeStruct(w.shape, BF16) for w in cast_weights],
        scratch_shapes=[pltpu.VMEM((1, t), F32),
                        pltpu.VMEM((V_ROWS, t), F32),
                        pltpu.VMEM((t, t), F32), pltpu.VMEM((t, t), F32)],
        compiler_params=_params(2),
        name="mla_attn",
    )(q_t, k, v_t, *cast_weights)
    return outs[0], outs[1:]


def _sb_attn_kernel(q_ref, k_ref, v_ref, o_ref, tri_sc, *bufs, t, group):
    n_tiles = q_ref.shape[0] // t
    n_groups = n_tiles // group
    w = 2 * t
    kinds = 6
    per_kind = 2 * group
    lom_sc, lb_sc, a_sc, wsum_sc, carry_sc, acc_sc = [
        [bufs[kind * per_kind + slot * group:kind * per_kind + (slot + 1) * group] for slot in range(2)]
        for kind in range(kinds)]
    tri_sc[...] = jnp.where(lax.broadcasted_iota(jnp.int32, (w, w), 0)
                            > lax.broadcasted_iota(jnp.int32, (w, w), 1), -1.0, 0.0).astype(BF16)

    def log_probs(q, start, width):
        z = _dot_nt(q, k_ref[pl.ds(start, width), :])
        neg_log_om = jnp.maximum(z, 0.0) + jnp.log2(1.0 + jnp.exp2(-jnp.abs(z)))
        return neg_log_om, z - neg_log_om

    def window_start(i):
        return pl.multiple_of(jnp.maximum(i - 1, 0) * t, t)

    def window_logits(slot, n, i, first_tile=False):
        i = jnp.minimum(i, n_tiles - 1)
        q = q_ref[pl.ds(pl.multiple_of(i * t, t), t), :]
        neg_log_om, log_beta = log_probs(q, window_start(i), w)
        lom, lb, wsum = lom_sc[slot][n], lb_sc[slot][n], wsum_sc[slot][n]
        strict = (lax.broadcasted_iota(jnp.int32, (t, t), 1)
                  < lax.broadcasted_iota(jnp.int32, (t, t), 0))
        if first_tile:
            halves = ((strict, neg_log_om[:, :t], log_beta[:, :t]), (None, None, None))
        else:
            halves = ((True, neg_log_om[:, :t], log_beta[:, :t]), (strict, neg_log_om[:, t:], log_beta[:, t:]))
        total = jnp.zeros((t, 1), F32)
        for half, (mask, nlo, lbeta) in enumerate(halves):
            cols = slice(half * t, (half + 1) * t)
            if mask is None:
                lom[:, cols] = jnp.zeros((t, t), BF16)
                lb[:, cols] = jnp.full((t, t), -jnp.inf, F32)
                continue
            if mask is not True:
                nlo = jnp.where(mask, nlo, 0.0)
                lbeta = jnp.where(mask, lbeta, -jnp.inf)
            lom[:, cols] = nlo.astype(BF16)
            lb[:, cols] = lbeta
            total = total + jnp.sum(nlo, axis=1, keepdims=True)
        wsum[...] = jnp.broadcast_to(-total, (t, LANES))

    def window_weights(slot, n):
        after = _dot(lom_sc[slot][n][...], tri_sc[...])
        a_sc[slot][n][...] = jnp.exp2(lb_sc[slot][n][...] + after).astype(BF16)
        carry_sc[slot][n][...] = wsum_sc[slot][n][...]

    def window_values(slot, n, i):
        acc_sc[slot][n][...] = _dot(a_sc[slot][n][...], v_ref[pl.ds(window_start(i), w), :])

    def earlier_tiles(slot, n, i):
        carry, acc = carry_sc[slot][n], acc_sc[slot][n]
        rows = pl.ds(pl.multiple_of(i * t, t), t)
        q = q_ref[rows, :]

        def more(state):
            j, carry_max = state
            return (j >= 0) & (carry_max > SB_LOG2_WEIGHT_FLOOR)

        def body(state):
            j, _ = state
            start = pl.multiple_of(j * t, t)
            neg_log_om, log_beta = log_probs(q, start, t)
            after = _dot(neg_log_om.astype(BF16), tri_sc[:t, :t]) + jnp.tile(carry[...], (1, t // LANES))
            a = jnp.exp2(log_beta + after)
            acc[...] += _dot(a.astype(BF16), v_ref[pl.ds(start, t), :])
            carry[...] -= jnp.sum(neg_log_om, axis=1, keepdims=True)
            return j - 1, jnp.max(carry[...])

        lax.while_loop(more, body, (i - 2, jnp.max(carry[...])))
        o_ref[rows, :] = acc[...].astype(o_ref.dtype)

    def step(g, slot):
        for n in range(group):
            window_logits(1 - slot, n, (g + 1) * group + n)
        for n in range(group):
            window_weights(slot, n)
        for n in range(group):
            window_values(1 - slot, n, (g - 1) * group + n)
        for n in range(group):
            earlier_tiles(1 - slot, n, (g - 1) * group + n)

    def step_pair(p, _):
        step(2 * p + 1, 1)
        step(2 * p + 2, 0)
        return 0

    for n in range(group):
        window_logits(0, n, n, first_tile=(n == 0))
    for n in range(group):
        window_weights(0, n)
    for n in range(group):
        window_logits(1, n, group + n)
    lax.fori_loop(0, n_groups // 2, step_pair, 0)


def _sb_attn(qkv, t=256, group=1):
    s = qkv.shape[0]
    h, d = SB_HEADS, SB_HEAD_DIM
    assert s % (2 * group * t) == 0
    per_kind = 2 * group
    return pl.pallas_call(
        functools.partial(_sb_attn_kernel, t=t, group=group),
        grid=(h,),
        in_specs=[pl.BlockSpec((s, d), lambda hh: (0, hh)),
                  pl.BlockSpec((s, d), lambda hh: (0, h + hh)),
                  pl.BlockSpec((s, d), lambda hh: (0, 2 * h + hh))],
        out_specs=pl.BlockSpec((s, d), lambda hh: (0, hh)),
        out_shape=jax.ShapeDtypeStruct((s, h * d), BF16),
        scratch_shapes=[pltpu.VMEM((2 * t, 2 * t), BF16)]
        + per_kind * [pltpu.VMEM((t, 2 * t), BF16)]
        + per_kind * [pltpu.VMEM((t, 2 * t), F32)]
        + per_kind * [pltpu.VMEM((t, 2 * t), BF16)]
        + per_kind * [pltpu.VMEM((t, LANES), F32)]
        + per_kind * [pltpu.VMEM((t, LANES), F32)]
        + per_kind * [pltpu.VMEM((t, d), F32)],
        compiler_params=_params(1),
        name="sb_attn",
    )(qkv, qkv, qkv)


def _normalize_rows_once(h_ref, g_ref, a_sc):
    @pl.when(pl.program_id(1) == 0)
    def _():
        a_sc[...] = _rms(h_ref[...], g_ref[...]).astype(BF16)


def _norm_matmul_scale_kernel(h_ref, g_ref, w_ref, cs_ref, o_ref, a_sc):
    _normalize_rows_once(h_ref, g_ref, a_sc)
    o_ref[...] = (_dot(a_sc[...], w_ref[...].astype(BF16)) * cs_ref[...]).astype(o_ref.dtype)


def _norm_matmul_scale(h, norm_g, w, layer, col_scale, tm=1024, tn=1024):
    s, k = h.shape
    n = w.shape[2]
    return pl.pallas_call(
        _norm_matmul_scale_kernel,
        grid=(s // tm, n // tn),
        in_specs=[pl.BlockSpec((tm, k), lambda i, j: (i, 0)), pl.BlockSpec((1, k), lambda i, j: (0, 0)),
                  pl.BlockSpec((None, k, tn), lambda i, j: (layer, 0, j)),
                  pl.BlockSpec((1, tn), lambda i, j: (0, j))],
        out_specs=pl.BlockSpec((tm, tn), lambda i, j: (i, j)),
        out_shape=jax.ShapeDtypeStruct((s, n), BF16),
        scratch_shapes=[pltpu.VMEM((tm, k), BF16)],
        compiler_params=_params(2),
        name="norm_matmul_scale",
    )(h, norm_g[None, :], w, col_scale[None, :])


def _matmul_residual_kernel(a_ref, w_ref, r_ref, o_ref):
    o_ref[...] = r_ref[...] + _dot(a_ref[...], w_ref[...].astype(BF16))


def _matmul_residual(a, w, layer, res, tm, tn):
    s, k = a.shape
    n = w.shape[2]
    return pl.pallas_call(
        _matmul_residual_kernel,
        grid=(s // tm, n // tn),
        in_specs=[pl.BlockSpec((tm, k), lambda i, j: (i, 0)),
                  pl.BlockSpec((None, k, tn), lambda i, j: (layer, 0, j)),
                  pl.BlockSpec((tm, tn), lambda i, j: (i, j))],
        out_specs=pl.BlockSpec((tm, tn), lambda i, j: (i, j)),
        out_shape=jax.ShapeDtypeStruct((s, n), F32),
        compiler_params=_params(2),
        name="matmul_residual",
    )(a, w, res)


def _ffn_up_kernel(h_ref, g_ref, wg_ref, wu_ref, o_ref, a_sc):
    _normalize_rows_once(h_ref, g_ref, a_sc)
    a = a_sc[...]
    g = _dot(a, wg_ref[...].astype(BF16))
    u = _dot(a, wu_ref[...].astype(BF16))
    o_ref[...] = (g * jax.nn.sigmoid(g) * u).astype(o_ref.dtype)


def _ffn_up(h, norm_g, w_gate_up, layer, tm=1024, tn=512):
    s, k = h.shape
    d_ff = w_gate_up.shape[2] // 2
    nj = d_ff // tn
    return pl.pallas_call(
        _ffn_up_kernel,
        grid=(s // tm, nj),
        in_specs=[pl.BlockSpec((tm, k), lambda i, j: (i, 0)), pl.BlockSpec((1, k), lambda i, j: (0, 0)),
                  pl.BlockSpec((None, k, tn), lambda i, j: (layer, 0, j)),
                  pl.BlockSpec((None, k, tn), lambda i, j: (layer, 0, nj + j))],
        out_specs=pl.BlockSpec((tm, tn), lambda i, j: (i, j)),
        out_shape=jax.ShapeDtypeStruct((s, d_ff), BF16),
        scratch_shapes=[pltpu.VMEM((tm, k), BF16)],
        compiler_params=_params(2),
        name="ffn_up",
    )(h, norm_g[None, :], w_gate_up, w_gate_up)


def _ffn(h, norm_g, w_gate_up, w_down, layer):
    act = _ffn_up(h, norm_g, w_gate_up, layer)
    return _matmul_residual(act, w_down, layer, h, tm=1024, tn=512)


def kernel(x, positions, attn_norm, mla_w_in, mla_q_norm, mla_kv_norm, mla_w_uq, mla_w_ukv, mla_w_o,
           sb_w_in, sb_w_o, ffn_norm, ffn_w_gate_up, ffn_w_down, final_norm):
    b, s, d = x.shape
    assert b == 1 and d == D_MODEL
    h = x.reshape(s, d)

    attn_tile = 512
    tabs = _rope_tables(positions.reshape(s, 1))
    tabs_t = _rope_tables_t(positions.reshape(1, s))
    w_in_pad = jnp.pad(mla_w_in[0], ((0, 0), (0, LANES - QK_ROPE_DIM))).astype(BF16)
    qk_dim = QK_NOPE_DIM + QK_ROPE_DIM
    w_uq_pad = jnp.pad(mla_w_uq[0].reshape(Q_LORA_RANK, MLA_HEADS, qk_dim),
                       ((0, 0), (0, 0), (0, QK_PAD_DIM - qk_dim)))
    w_uq_pad_t = w_uq_pad.reshape(Q_LORA_RANK, MLA_HEADS * QK_PAD_DIM).T.astype(BF16)
    w_ukv = mla_w_ukv[0].reshape(KV_LORA_RANK, MLA_HEADS, QK_NOPE_DIM + V_HEAD_DIM)
    w_uk = w_ukv[:, :, :QK_NOPE_DIM].reshape(KV_LORA_RANK, MLA_HEADS * QK_NOPE_DIM).astype(BF16)
    w_uv_t = w_ukv[:, :, QK_NOPE_DIM:].reshape(KV_LORA_RANK, MLA_HEADS * V_HEAD_DIM).T.astype(BF16)

    cq, ckv, kr = _mla_in(h, attn_norm[0], w_in_pad, mla_q_norm[0], mla_kv_norm[0], tabs)
    q_t = _mla_q(cq, w_uq_pad_t, tabs_t, qk_dim ** -0.5 * LOG2_E, attn_tile)
    k, v_t = _mla_kv(ckv, w_uk, w_uv_t, kr, attn_tile)
    to_cast = (mla_w_o, sb_w_in, sb_w_o, ffn_w_gate_up, ffn_w_down)
    slab_cols = (D_MODEL, sb_w_in.shape[2], D_MODEL, ffn_w_gate_up.shape[2], D_MODEL // 2)
    o, cast = _mla_attn(q_t, k, v_t, attn_tile,
                        cast_weights=[w.reshape(-1, c) for w, c in zip(to_cast, slab_cols)])
    w_o, w_sb_in, w_sb_o, w_gate_up, w_down = [c.reshape(w.shape) for c, w in zip(cast, to_cast)]
    h = _matmul_residual(o, w_o, 0, h, tm=1024, tn=1024)
    h = _ffn(h, ffn_norm[0], w_gate_up, w_down, 0)

    n_q = SB_HEADS * SB_HEAD_DIM
    col_scale = jnp.concatenate([jnp.full((n_q,), SB_HEAD_DIM ** -0.5 * LOG2_E, F32), jnp.ones((2 * n_q,), F32)])
    qkv = _norm_matmul_scale(h, attn_norm[1], w_sb_in, 0, col_scale)
    o = _sb_attn(qkv)
    h = _matmul_residual(o, w_sb_o, 0, h, tm=1024, tn=1024)
    h = _ffn(h, ffn_norm[1], w_gate_up, w_down, 1)

    return _rmsnorm(h, final_norm, F32).reshape(b, s, d)
```

```python
import functools

import jax
import jax.numpy as jnp
from jax import lax
from jax.experimental import pallas as pl
from jax.experimental.pallas import tpu as pltpu

D_MODEL = 2048
MLA_HEADS = 16
Q_LORA_RANK = 512
KV_LORA_RANK = 512
QK_NOPE_DIM = 128
QK_ROPE_DIM = 64
V_HEAD_DIM = 128
ROPE_THETA = 10000.0
SB_HEADS = 16
SB_HEAD_DIM = 128
RMS_EPS = 1e-6

LANES = 128
QK_PAD_DIM = 2 * LANES
VMEM_LIMIT_BYTES = 56 * 1024 * 1024

LOG2_E = 1.4426950408889634
SB_LOG2_WEIGHT_FLOOR = -120.0 * LOG2_E

F32 = jnp.float32
BF16 = jnp.bfloat16


def _params(n_axes):
    return pltpu.CompilerParams(
        dimension_semantics=("arbitrary",) * n_axes, vmem_limit_bytes=VMEM_LIMIT_BYTES)


def _dot(a, b):
    return jnp.dot(a, b, preferred_element_type=F32)


def _dot_nt(a, b):
    return lax.dot_general(a, b, (((1,), (1,)), ((), ())), preferred_element_type=F32)


def _rms(x, g):
    return x * lax.rsqrt(jnp.mean(x * x, axis=-1, keepdims=True) + RMS_EPS) * g


def _rope(t, c, s1, s2):
    half = QK_ROPE_DIM // 2
    return t * c + pltpu.roll(t, half, 1) * s1 + pltpu.roll(t, LANES - half, 1) * s2


def _rope_tables_kernel(pos_ref, freq_ref, c_ref, s1_ref, s2_ref):
    half = QK_ROPE_DIM // 2
    ang = pos_ref[...].astype(F32) * freq_ref[...]
    lane = lax.broadcasted_iota(jnp.int32, ang.shape, 1)
    cos, sin = jnp.cos(ang), jnp.sin(ang)
    c_ref[...] = jnp.where(lane < 2 * half, cos, 0.0)
    s1_ref[...] = jnp.where((lane >= half) & (lane < 2 * half), sin, 0.0)
    s2_ref[...] = jnp.where(lane < half, -sin, 0.0)


def _rope_tables(positions, tm=2048):
    s = positions.shape[0]
    inv_freq = ROPE_THETA ** (-jnp.arange(0, QK_ROPE_DIM, 2, dtype=F32) / QK_ROPE_DIM)
    freq = jnp.concatenate([inv_freq, inv_freq, jnp.zeros((LANES - QK_ROPE_DIM,), F32)])[None, :]
    out = jax.ShapeDtypeStruct((s, LANES), F32)
    row = pl.BlockSpec((tm, LANES), lambda i: (i, 0))
    return pl.pallas_call(
        _rope_tables_kernel,
        grid=(s // tm,),
        in_specs=[pl.BlockSpec((tm, 1), lambda i: (i, 0)), pl.BlockSpec((1, LANES), lambda i: (0, 0))],
        out_specs=[row, row, row],
        out_shape=[out, out, out],
        compiler_params=_params(1),
        name="rope_tables",
    )(positions, freq)


def _rope_tables_t_kernel(pos_ref, freq_ref, c_ref, s1_ref, s2_ref):
    half = QK_ROPE_DIM // 2
    ang = freq_ref[...] * pos_ref[...].astype(F32)
    r = lax.broadcasted_iota(jnp.int32, ang.shape, 0)
    cos, sin = jnp.cos(ang), jnp.sin(ang)
    c_ref[...] = jnp.where(r < 2 * half, cos, 0.0)
    s1_ref[...] = jnp.where((r >= half) & (r < 2 * half), sin, 0.0)
    s2_ref[...] = jnp.where(r < half, -sin, 0.0)


def _rope_tables_t(positions, tm=2048):
    s = positions.shape[1]
    inv_freq = ROPE_THETA ** (-jnp.arange(0, QK_ROPE_DIM, 2, dtype=F32) / QK_ROPE_DIM)
    freq = jnp.concatenate([inv_freq, inv_freq, jnp.zeros((LANES - QK_ROPE_DIM,), F32)])[:, None]
    out = jax.ShapeDtypeStruct((LANES, s), F32)
    col = pl.BlockSpec((LANES, tm), lambda i: (0, i))
    return pl.pallas_call(
        _rope_tables_t_kernel,
        grid=(s // tm,),
        in_specs=[pl.BlockSpec((1, tm), lambda i: (0, i)), pl.BlockSpec((LANES, 1), lambda i: (0, 0))],
        out_specs=[col, col, col],
        out_shape=[out, out, out],
        compiler_params=_params(1),
        name="rope_tables_t",
    )(positions, freq)


def _norm_kernel(x_ref, g_ref, o_ref):
    o_ref[...] = _rms(x_ref[...], g_ref[...]).astype(o_ref.dtype)


def _rmsnorm(x, g, out_dtype, tm=512):
    s, d = x.shape
    return pl.pallas_call(
        _norm_kernel,
        grid=(s // tm,),
        in_specs=[pl.BlockSpec((tm, d), lambda i: (i, 0)), pl.BlockSpec((1, d), lambda i: (0, 0))],
        out_specs=pl.BlockSpec((tm, d), lambda i: (i, 0)),
        out_shape=jax.ShapeDtypeStruct((s, d), out_dtype),
        compiler_params=_params(1),
        name="rmsnorm",
    )(x, g[None, :])


def _mla_in_kernel(h_ref, g_ref, w_ref, qn_ref, kvn_ref, c_ref, s1_ref, s2_ref, cq_ref, ckv_ref, kr_ref):
    a = _rms(h_ref[...], g_ref[...]).astype(BF16)
    proj = _dot(a, w_ref[...])
    cq_ref[...] = _rms(proj[:, :Q_LORA_RANK], qn_ref[...]).astype(BF16)
    ckv_ref[...] = _rms(proj[:, Q_LORA_RANK:Q_LORA_RANK + KV_LORA_RANK], kvn_ref[...]).astype(BF16)
    kr = proj[:, Q_LORA_RANK + KV_LORA_RANK:]
    kr_ref[...] = _rope(kr, c_ref[...], s1_ref[...], s2_ref[...]).astype(BF16)


def _mla_in(h, norm_g, w_pad, q_norm, kv_norm, tabs, tm=1024):
    s, d = h.shape
    n = w_pad.shape[1]
    row = lambda w: pl.BlockSpec((tm, w), lambda i: (i, 0))
    const = lambda r, w: pl.BlockSpec((r, w), lambda i: (0, 0))
    return pl.pallas_call(
        _mla_in_kernel,
        grid=(s // tm,),
        in_specs=[row(d), const(1, d), const(d, n), const(1, Q_LORA_RANK), const(1, KV_LORA_RANK),
                  row(LANES), row(LANES), row(LANES)],
        out_specs=[row(Q_LORA_RANK), row(KV_LORA_RANK), row(LANES)],
        out_shape=[jax.ShapeDtypeStruct((s, Q_LORA_RANK), BF16),
                   jax.ShapeDtypeStruct((s, KV_LORA_RANK), BF16),
                   jax.ShapeDtypeStruct((s, LANES), BF16)],
        compiler_params=_params(1),
        name="mla_in",
    )(h, norm_g[None, :], w_pad, q_norm[None, :], kv_norm[None, :], *tabs)


def _mla_q_kernel(w_ref, cq_ref, c_ref, s1_ref, s2_ref, q_ref, *, heads, scale):
    acc = _dot_nt(w_ref[...], cq_ref[...])
    c, s1, s2 = c_ref[...], s1_ref[...], s2_ref[...]
    half = QK_ROPE_DIM // 2
    tq = q_ref.shape[3]
    for h in range(heads):
        lo = h * QK_PAD_DIM
        nope = (acc[lo:lo + LANES, :] * scale).astype(BF16)
        t = acc[lo + LANES:lo + QK_PAD_DIM, :]
        down = jnp.concatenate([t[LANES - half:], t[:LANES - half]], axis=0)
        up = jnp.concatenate([t[half:], t[:half]], axis=0)
        rope = ((t * c + down * s1 + up * s2) * scale).astype(BF16)
        for tt in range(acc.shape[1] // tq):
            q_ref[h, tt, :LANES, :] = nope[:, tt * tq:(tt + 1) * tq]
            q_ref[h, tt, LANES:, :] = rope[:, tt * tq:(tt + 1) * tq]


def _mla_q(cq, w_uq_pad_t, tabs_t, scale, t, tm=1024, heads_per_tile=8):
    s, r = cq.shape
    n = w_uq_pad_t.shape[0]
    tn = heads_per_tile * QK_PAD_DIM
    col = pl.BlockSpec((LANES, tm), lambda j, i: (0, i))
    return pl.pallas_call(
        functools.partial(_mla_q_kernel, heads=heads_per_tile, scale=scale),
        grid=(n // tn, s // tm),
        in_specs=[pl.BlockSpec((tn, r), lambda j, i: (j, 0)), pl.BlockSpec((tm, r), lambda j, i: (i, 0)),
                  col, col, col],
        out_specs=pl.BlockSpec((heads_per_tile, tm // t, QK_PAD_DIM, t), lambda j, i: (j, i, 0, 0)),
        out_shape=jax.ShapeDtypeStruct((n // QK_PAD_DIM, s // t, QK_PAD_DIM, t), BF16),
        compiler_params=_params(2),
        name="mla_q",
    )(w_uq_pad_t, cq, *tabs_t)


V_ROWS = V_HEAD_DIM + 16


def _mla_kv_kernel(ckv_ref, wk_ref, wv_ref, kr_ref, k_ref, v_ref, *, heads, t):
    ckv = ckv_ref[...]
    kn = _dot(ckv, wk_ref[...])
    vt = _dot_nt(wv_ref[...], ckv)
    kr = kr_ref[...]
    ones = jnp.ones((V_ROWS - V_HEAD_DIM, t), BF16)
    for h in range(heads):
        k_ref[:, h * QK_PAD_DIM:h * QK_PAD_DIM + LANES] = kn[:, h * LANES:(h + 1) * LANES].astype(BF16)
        k_ref[:, h * QK_PAD_DIM + LANES:(h + 1) * QK_PAD_DIM] = kr
        for tt in range(ckv.shape[0] // t):
            v_ref[h, tt, :V_HEAD_DIM, :] = vt[h * V_HEAD_DIM:(h + 1) * V_HEAD_DIM, tt * t:(tt + 1) * t].astype(BF16)
            v_ref[h, tt, V_HEAD_DIM:, :] = ones


def _mla_kv(ckv, w_uk, w_uv_t, kr, t, tm=1024, heads_per_tile=8):
    s, r = ckv.shape
    n = w_uk.shape[1]
    heads = n // LANES
    tn = heads_per_tile * LANES
    return pl.pallas_call(
        functools.partial(_mla_kv_kernel, heads=heads_per_tile, t=t),
        grid=(s // tm, n // tn),
        in_specs=[pl.BlockSpec((tm, r), lambda i, j: (i, 0)),
                  pl.BlockSpec((r, tn), lambda i, j: (0, j)),
                  pl.BlockSpec((tn, r), lambda i, j: (j, 0)),
                  pl.BlockSpec((tm, LANES), lambda i, j: (i, 0))],
        out_specs=[pl.BlockSpec((tm, 2 * tn), lambda i, j: (i, j)),
                   pl.BlockSpec((heads_per_tile, tm // t, V_ROWS, t), lambda i, j: (j, i, 0, 0))],
        out_shape=[jax.ShapeDtypeStruct((s, 2 * n), BF16),
                   jax.ShapeDtypeStruct((heads, s // t, V_ROWS, t), BF16)],
        compiler_params=_params(2),
        name="mla_kv",
    )(ckv, w_uk, w_uv_t, kr)


def _mla_attn_kernel(q_ref, k_ref, v_ref, *rest, t, unrolls, n_cast):
    w_refs, o_ref, w_bf16_refs = rest[:n_cast], rest[n_cast], rest[n_cast + 1:2 * n_cast + 1]
    m_sc, acc_sc, s_even, s_odd = rest[2 * n_cast + 1:]
    for w_ref, w_bf16_ref in zip(w_refs, w_bf16_refs):
        w_bf16_ref[...] = w_ref[...].astype(BF16)

    def q_tile(qi, carry):
        _mla_attn_q_tile(pl.program_id(1) * q_ref.shape[0] + qi, q_ref[qi], k_ref, v_ref,
                         o_ref.at[pl.ds(pl.multiple_of(qi * t, t), t), :],
                         m_sc, acc_sc, s_even, s_odd, t=t, unrolls=unrolls)
        return carry

    lax.fori_loop(0, q_ref.shape[0], q_tile, 0)


def _mla_attn_q_tile(i, q, k_ref, v_ref, o_ref, m_sc, acc_sc, s_even, s_odd, *, t, unrolls):
    m_sc[...] = jnp.full(m_sc.shape, -jnp.inf, F32)
    acc_sc[...] = jnp.zeros(acc_sc.shape, F32)

    def scores(j):
        return _dot(k_ref[pl.ds(pl.multiple_of(j * t, t), t), :], q)

    def softmax_pv(s_ref, j, diagonal):
        def load():
            s = s_ref[...]
            if diagonal:
                key = lax.broadcasted_iota(jnp.int32, (t, t), 0)
                query = lax.broadcasted_iota(jnp.int32, (t, t), 1)
                s = jnp.where(key <= query, s, -jnp.inf)
            return s
        m_prev = m_sc[...]
        m_next = jnp.maximum(m_prev, jnp.max(load(), axis=0, keepdims=True))
        m_sc[...] = m_next
        p = jnp.exp2(load() - m_sc[...]).astype(BF16)
        alpha = jnp.exp2(m_prev - m_next)
        acc_sc[...] = alpha * acc_sc[...] + _dot(v_ref[j], p)

    s_even[...] = scores(0)

    def pairs(first, n_pairs):
        for d in range(n_pairs):
            j = first + 2 * d
            s_odd[...] = scores(j + 1)
            softmax_pv(s_even, j, False)
            s_even[...] = scores(j + 2)
            softmax_pv(s_odd, j + 1, False)

    done = 0
    for u in unrolls:
        n_groups = (i // 2 - done) // u

        def group(g, carry, u=u, done=done):
            pairs(2 * (done + g * u), u)
            return carry

        lax.fori_loop(0, n_groups, group, 0)
        done = done + n_groups * u

    @pl.when(i % 2 == 0)
    def _():
        softmax_pv(s_even, i, True)

    @pl.when(i % 2 == 1)
    def _():
        s_odd[...] = scores(i)
        softmax_pv(s_even, i - 1, False)
        softmax_pv(s_odd, i, True)

    out_t = acc_sc[:V_HEAD_DIM, :] / acc_sc[V_HEAD_DIM:V_HEAD_DIM + 1, :]
    o_ref[...] = out_t.T.astype(o_ref.dtype)


BF16_SUBLANES = 16


def _mla_attn(q_t, k, v_t, t, cast_weights=(), q_tiles_per_step=4, unrolls=(4, 2, 1)):
    s = k.shape[0]
    h = MLA_HEADS
    g = q_tiles_per_step
    n_i = s // (t * g)
    n_steps = h * n_i
    assert unrolls[-1] == 1
    cast_specs = []
    for w in cast_weights:
        rows, cols = w.shape
        share = next(sh for sh in (1, 2, 4, 8) if rows % (n_steps // sh * BF16_SUBLANES) == 0)
        cast_specs.append(pl.BlockSpec((rows // (n_steps // share), cols),
                                       lambda hh, i, share=share: ((hh * n_i + i) // share, 0)))
    outs = pl.pallas_call(
        functools.partial(_mla_attn_kernel, t=t, unrolls=unrolls, n_cast=len(cast_weights)),
        grid=(h, n_i),
        in_specs=[pl.BlockSpec((None, g, QK_PAD_DIM, t), lambda hh, i: (hh, i, 0, 0)),
                  pl.BlockSpec((s, QK_PAD_DIM), lambda hh, i: (0, hh)),
                  pl.BlockSpec((None, s // t, V_ROWS, t), lambda hh, i: (hh, 0, 0, 0))] + cast_specs,
        out_specs=[pl.BlockSpec((g * t, V_HEAD_DIM), lambda hh, i: (i, hh))] + cast_specs,
        out_shape=[jax.ShapeDtypeStruct((s, h * V_HEAD_DIM), BF16)]
        + [jax.ShapeDtypeStruct(w.shape, BF16) for w in cast_weights],
        scratch_shapes=[pltpu.VMEM((1, t), F32),
                        pltpu.VMEM((V_ROWS, t), F32),
                        pltpu.VMEM((t, t), F32), pltpu.VMEM((t, t), F32)],
        compiler_params=_params(2),
        name="mla_attn",
    )(q_t, k, v_t, *cast_weights)
    return outs[0], outs[1:]


def _sb_attn_kernel(q_ref, k_ref, v_ref, o_ref, tri_sc, *bufs, t, group):
    n_tiles = q_ref.shape[0] // t
    n_groups = n_tiles // group
    w = 2 * t
    kinds = 6
    per_kind = 2 * group
    lom_sc, lb_sc, a_sc, wsum_sc, carry_sc, acc_sc = [
        [bufs[kind * per_kind + slot * group:kind * per_kind + (slot + 1) * group] for slot in range(2)]
        for kind in range(kinds)]
    tri_sc[...] = jnp.where(lax.broadcasted_iota(jnp.int32, (w, w), 0)
                            > lax.broadcasted_iota(jnp.int32, (w, w), 1), -1.0, 0.0).astype(BF16)

    def log_probs(q, start, width):
        z = _dot_nt(q, k_ref[pl.ds(start, width), :])
        neg_log_om = jnp.maximum(z, 0.0) + jnp.log2(1.0 + jnp.exp2(-jnp.abs(z)))
        return neg_log_om, z - neg_log_om

    def window_start(i):
        return pl.multiple_of(jnp.maximum(i - 1, 0) * t, t)

    def window_logits(slot, n, i, first_tile=False):
        i = jnp.minimum(i, n_tiles - 1)
        q = q_ref[pl.ds(pl.multiple_of(i * t, t), t), :]
        neg_log_om, log_beta = log_probs(q, window_start(i), w)
        lom, lb, wsum = lom_sc[slot][n], lb_sc[slot][n], wsum_sc[slot][n]
        strict = (lax.broadcasted_iota(jnp.int32, (t, t), 1)
                  < lax.broadcasted_iota(jnp.int32, (t, t), 0))
        if first_tile:
            halves = ((strict, neg_log_om[:, :t], log_beta[:, :t]), (None, None, None))
        else:
            halves = ((True, neg_log_om[:, :t], log_beta[:, :t]), (strict, neg_log_om[:, t:], log_beta[:, t:]))
        total = jnp.zeros((t, 1), F32)
        for half, (mask, nlo, lbeta) in enumerate(halves):
            cols = slice(half * t, (half + 1) * t)
            if mask is None:
                lom[:, cols] = jnp.zeros((t, t), BF16)
                lb[:, cols] = jnp.full((t, t), -jnp.inf, F32)
                continue
            if mask is not True:
                nlo = jnp.where(mask, nlo, 0.0)
                lbeta = jnp.where(mask, lbeta, -jnp.inf)
            lom[:, cols] = nlo.astype(BF16)
            lb[:, cols] = lbeta
            total = total + jnp.sum(nlo, axis=1, keepdims=True)
        wsum[...] = jnp.broadcast_to(-total, (t, LANES))

    def window_weights(slot, n):
        after = _dot(lom_sc[slot][n][...], tri_sc[...])
        a_sc[slot][n][...] = jnp.exp2(lb_sc[slot][n][...] + after).astype(BF16)
        carry_sc[slot][n][...] = wsum_sc[slot][n][...]

    def window_values(slot, n, i):
        acc_sc[slot][n][...] = _dot(a_sc[slot][n][...], v_ref[pl.ds(window_start(i), w), :])

    def earlier_tiles(slot, n, i):
        carry, acc = carry_sc[slot][n], acc_sc[slot][n]
        rows = pl.ds(pl.multiple_of(i * t, t), t)
        q = q_ref[rows, :]

        def more(state):
            j, carry_max = state
            return (j >= 0) & (carry_max > SB_LOG2_WEIGHT_FLOOR)

        def body(state):
            j, _ = state
            start = pl.multiple_of(j * t, t)
            neg_log_om, log_beta = log_probs(q, start, t)
            after = _dot(neg_log_om.astype(BF16), tri_sc[:t, :t]) + jnp.tile(carry[...], (1, t // LANES))
            a = jnp.exp2(log_beta + after)
            acc[...] += _dot(a.astype(BF16), v_ref[pl.ds(start, t), :])
            carry[...] -= jnp.sum(neg_log_om, axis=1, keepdims=True)
            return j - 1, jnp.max(carry[...])

        lax.while_loop(more, body, (i - 2, jnp.max(carry[...])))
        o_ref[rows, :] = acc[...].astype(o_ref.dtype)

    def step(g, slot):
        for n in range(group):
            window_logits(1 - slot, n, (g + 1) * group + n)
        for n in range(group):
            window_weights(slot, n)
        for n in range(group):
            window_values(1 - slot, n, (g - 1) * group + n)
        for n in range(group):
            earlier_tiles(1 - slot, n, (g - 1) * group + n)

    def step_pair(p, _):
        step(2 * p + 1, 1)
        step(2 * p + 2, 0)
        return 0

    for n in range(group):
        window_logits(0, n, n, first_tile=(n == 0))
    for n in range(group):
        window_weights(0, n)
    for n in range(group):
        window_logits(1, n, group + n)
    lax.fori_loop(0, n_groups // 2, step_pair, 0)


def _sb_attn(qkv, t=256, group=1):
    s = qkv.shape[0]
    h, d = SB_HEADS, SB_HEAD_DIM
    assert s % (2 * group * t) == 0
    per_kind = 2 * group
    return pl.pallas_call(
        functools.partial(_sb_attn_kernel, t=t, group=group),
        grid=(h,),
        in_specs=[pl.BlockSpec((s, d), lambda hh: (0, hh)),
                  pl.BlockSpec((s, d), lambda hh: (0, h + hh)),
                  pl.BlockSpec((s, d), lambda hh: (0, 2 * h + hh))],
        out_specs=pl.BlockSpec((s, d), lambda hh: (0, hh)),
        out_shape=jax.ShapeDtypeStruct((s, h * d), BF16),
        scratch_shapes=[pltpu.VMEM((2 * t, 2 * t), BF16)]
        + per_kind * [pltpu.VMEM((t, 2 * t), BF16)]
        + per_kind * [pltpu.VMEM((t, 2 * t), F32)]
        + per_kind * [pltpu.VMEM((t, 2 * t), BF16)]
        + per_kind * [pltpu.VMEM((t, LANES), F32)]
        + per_kind * [pltpu.VMEM((t, LANES), F32)]
        + per_kind * [pltpu.VMEM((t, d), F32)],
        compiler_params=_params(1),
        name="sb_attn",
    )(qkv, qkv, qkv)


def _normalize_rows_once(h_ref, g_ref, a_sc):
    @pl.when(pl.program_id(1) == 0)
    def _():
        a_sc[...] = _rms(h_ref[...], g_ref[...]).astype(BF16)


def _norm_matmul_scale_kernel(h_ref, g_ref, w_ref, cs_ref, o_ref, a_sc):
    _normalize_rows_once(h_ref, g_ref, a_sc)
    o_ref[...] = (_dot(a_sc[...], w_ref[...].astype(BF16)) * cs_ref[...]).astype(o_ref.dtype)


def _norm_matmul_scale(h, norm_g, w, layer, col_scale, tm=1024, tn=1024):
    s, k = h.shape
    n = w.shape[2]
    return pl.pallas_call(
        _norm_matmul_scale_kernel,
        grid=(s // tm, n // tn),
        in_specs=[pl.BlockSpec((tm, k), lambda i, j: (i, 0)), pl.BlockSpec((1, k), lambda i, j: (0, 0)),
                  pl.BlockSpec((None, k, tn), lambda i, j: (layer, 0, j)),
                  pl.BlockSpec((1, tn), lambda i, j: (0, j))],
        out_specs=pl.BlockSpec((tm, tn), lambda i, j: (i, j)),
        out_shape=jax.ShapeDtypeStruct((s, n), BF16),
        scratch_shapes=[pltpu.VMEM((tm, k), BF16)],
        compiler_params=_params(2),
        name="norm_matmul_scale",
    )(h, norm_g[None, :], w, col_scale[None, :])


def _matmul_residual_kernel(a_ref, w_ref, r_ref, o_ref):
    o_ref[...] = r_ref[...] + _dot(a_ref[...], w_ref[...].astype(BF16))


def _matmul_residual(a, w, layer, res, tm, tn):
    s, k = a.shape
    n = w.shape[2]
    return pl.pallas_call(
        _matmul_residual_kernel,
        grid=(s // tm, n // tn),
        in_specs=[pl.BlockSpec((tm, k), lambda i, j: (i, 0)),
                  pl.BlockSpec((None, k, tn), lambda i, j: (layer, 0, j)),
                  pl.BlockSpec((tm, tn), lambda i, j: (i, j))],
        out_specs=pl.BlockSpec((tm, tn), lambda i, j: (i, j)),
        out_shape=jax.ShapeDtypeStruct((s, n), F32),
        compiler_params=_params(2),
        name="matmul_residual",
    )(a, w, res)


def _ffn_up_kernel(h_ref, g_ref, wg_ref, wu_ref, o_ref, a_sc):
    _normalize_rows_once(h_ref, g_ref, a_sc)
    a = a_sc[...]
    g = _dot(a, wg_ref[...].astype(BF16))
    u = _dot(a, wu_ref[...].astype(BF16))
    o_ref[...] = (g * jax.nn.sigmoid(g) * u).astype(o_ref.dtype)


def _ffn_up(h, norm_g, w_gate_up, layer, tm=1024, tn=512):
    s, k = h.shape
    d_ff = w_gate_up.shape[2] // 2
    nj = d_ff // tn
    return pl.pallas_call(
        _ffn_up_kernel,
        grid=(s // tm, nj),
        in_specs=[pl.BlockSpec((tm, k), lambda i, j: (i, 0)), pl.BlockSpec((1, k), lambda i, j: (0, 0)),
                  pl.BlockSpec((None, k, tn), lambda i, j: (layer, 0, j)),
                  pl.BlockSpec((None, k, tn), lambda i, j: (layer, 0, nj + j))],
        out_specs=pl.BlockSpec((tm, tn), lambda i, j: (i, j)),
        out_shape=jax.ShapeDtypeStruct((s, d_ff), BF16),
        scratch_shapes=[pltpu.VMEM((tm, k), BF16)],
        compiler_params=_params(2),
        name="ffn_up",
    )(h, norm_g[None, :], w_gate_up, w_gate_up)


def _ffn(h, norm_g, w_gate_up, w_down, layer):
    act = _ffn_up(h, norm_g, w_gate_up, layer)
    return _matmul_residual(act, w_down, layer, h, tm=1024, tn=512)


def kernel(x, positions, attn_norm, mla_w_in, mla_q_norm, mla_kv_norm, mla_w_uq, mla_w_ukv, mla_w_o,
           sb_w_in, sb_w_o, ffn_norm, ffn_w_gate_up, ffn_w_down, final_norm):
    b, s, d = x.shape
    assert b == 1 and d == D_MODEL
    h = x.reshape(s, d)

    attn_tile = 512
    tabs = _rope_tables(positions.reshape(s, 1))
    tabs_t = _rope_tables_t(positions.reshape(1, s))
    w_in_pad = jnp.pad(mla_w_in[0], ((0, 0), (0, LANES - QK_ROPE_DIM))).astype(BF16)
    qk_dim = QK_NOPE_DIM + QK_ROPE_DIM
    w_uq_pad = jnp.pad(mla_w_uq[0].reshape(Q_LORA_RANK, MLA_HEADS, qk_dim),
                       ((0, 0), (0, 0), (0, QK_PAD_DIM - qk_dim)))
    w_uq_pad_t = w_uq_pad.reshape(Q_LORA_RANK, MLA_HEADS * QK_PAD_DIM).T.astype(BF16)
    w_ukv = mla_w_ukv[0].reshape(KV_LORA_RANK, MLA_HEADS, QK_NOPE_DIM + V_HEAD_DIM)
    w_uk = w_ukv[:, :, :QK_NOPE_DIM].reshape(KV_LORA_RANK, MLA_HEADS * QK_NOPE_DIM).astype(BF16)
    w_uv_t = w_ukv[:, :, QK_NOPE_DIM:].reshape(KV_LORA_RANK, MLA_HEADS * V_HEAD_DIM).T.astype(BF16)

    cq, ckv, kr = _mla_in(h, attn_norm[0], w_in_pad, mla_q_norm[0], mla_kv_norm[0], tabs)
    q_t = _mla_q(cq, w_uq_pad_t, tabs_t, qk_dim ** -0.5 * LOG2_E, attn_tile)
    k, v_t = _mla_kv(ckv, w_uk, w_uv_t, kr, attn_tile)
    to_cast = (mla_w_o, sb_w_in, sb_w_o, ffn_w_gate_up, ffn_w_down)
    o, cast = _mla_attn(q_t, k, v_t, attn_tile, cast_weights=[w.reshape(-1, w.shape[2]) for w in to_cast])
    w_o, w_sb_in, w_sb_o, w_gate_up, w_down = [c.reshape(w.shape) for c, w in zip(cast, to_cast)]
    h = _matmul_residual(o, w_o, 0, h, tm=1024, tn=1024)
    h = _ffn(h, ffn_norm[0], w_gate_up, w_down, 0)

    n_q = SB_HEADS * SB_HEAD_DIM
    col_scale = jnp.concatenate([jnp.full((n_q,), SB_HEAD_DIM ** -0.5 * LOG2_E, F32), jnp.ones((2 * n_q,), F32)])
    qkv = _norm_matmul_scale(h, attn_norm[1], w_sb_in, 0, col_scale)
    o = _sb_attn(qkv)
    h = _matmul_residual(o, w_sb_o, 0, h, tm=1024, tn=1024)
    h = _ffn(h, ffn_norm[1], w_gate_up, w_down, 1)

    return _rmsnorm(h, final_norm, F32).reshape(b, s, d)
```

```python
import functools

import jax
import jax.numpy as jnp
from jax import lax
from jax.experimental import pallas as pl
from jax.experimental.pallas import tpu as pltpu

D_MODEL = 2048
MLA_HEADS = 16
Q_LORA_RANK = 512
KV_LORA_RANK = 512
QK_NOPE_DIM = 128
QK_ROPE_DIM = 64
V_HEAD_DIM = 128
ROPE_THETA = 10000.0
SB_HEADS = 16
SB_HEAD_DIM = 128
RMS_EPS = 1e-6

LANES = 128
QK_PAD_DIM = 2 * LANES
VMEM_LIMIT_BYTES = 56 * 1024 * 1024

LOG2_E = 1.4426950408889634
SB_LOG2_WEIGHT_FLOOR = -120.0 * LOG2_E

F32 = jnp.float32
BF16 = jnp.bfloat16


def _params(n_axes):
    return pltpu.CompilerParams(
        dimension_semantics=("arbitrary",) * n_axes, vmem_limit_bytes=VMEM_LIMIT_BYTES)


def _dot(a, b):
    return jnp.dot(a, b, preferred_element_type=F32)


def _dot_nt(a, b):
    return lax.dot_general(a, b, (((1,), (1,)), ((), ())), preferred_element_type=F32)


def _rms(x, g):
    return x * lax.rsqrt(jnp.mean(x * x, axis=-1, keepdims=True) + RMS_EPS) * g


def _rope(t, c, s1, s2):
    half = QK_ROPE_DIM // 2
    return t * c + pltpu.roll(t, half, 1) * s1 + pltpu.roll(t, LANES - half, 1) * s2


def _rope_tables_kernel(pos_ref, freq_ref, c_ref, s1_ref, s2_ref):
    half = QK_ROPE_DIM // 2
    ang = pos_ref[...].astype(F32) * freq_ref[...]
    lane = lax.broadcasted_iota(jnp.int32, ang.shape, 1)
    cos, sin = jnp.cos(ang), jnp.sin(ang)
    c_ref[...] = jnp.where(lane < 2 * half, cos, 0.0)
    s1_ref[...] = jnp.where((lane >= half) & (lane < 2 * half), sin, 0.0)
    s2_ref[...] = jnp.where(lane < half, -sin, 0.0)


def _rope_tables(positions, tm=2048):
    s = positions.shape[0]
    inv_freq = ROPE_THETA ** (-jnp.arange(0, QK_ROPE_DIM, 2, dtype=F32) / QK_ROPE_DIM)
    freq = jnp.concatenate([inv_freq, inv_freq, jnp.zeros((LANES - QK_ROPE_DIM,), F32)])[None, :]
    out = jax.ShapeDtypeStruct((s, LANES), F32)
    row = pl.BlockSpec((tm, LANES), lambda i: (i, 0))
    return pl.pallas_call(
        _rope_tables_kernel,
        grid=(s // tm,),
        in_specs=[pl.BlockSpec((tm, 1), lambda i: (i, 0)), pl.BlockSpec((1, LANES), lambda i: (0, 0))],
        out_specs=[row, row, row],
        out_shape=[out, out, out],
        compiler_params=_params(1),
        name="rope_tables",
    )(positions, freq)


def _rope_tables_t_kernel(pos_ref, freq_ref, c_ref, s1_ref, s2_ref):
    half = QK_ROPE_DIM // 2
    ang = freq_ref[...] * pos_ref[...].astype(F32)
    r = lax.broadcasted_iota(jnp.int32, ang.shape, 0)
    cos, sin = jnp.cos(ang), jnp.sin(ang)
    c_ref[...] = jnp.where(r < 2 * half, cos, 0.0)
    s1_ref[...] = jnp.where((r >= half) & (r < 2 * half), sin, 0.0)
    s2_ref[...] = jnp.where(r < half, -sin, 0.0)


def _rope_tables_t(positions, tm=2048):
    s = positions.shape[1]
    inv_freq = ROPE_THETA ** (-jnp.arange(0, QK_ROPE_DIM, 2, dtype=F32) / QK_ROPE_DIM)
    freq = jnp.concatenate([inv_freq, inv_freq, jnp.zeros((LANES - QK_ROPE_DIM,), F32)])[:, None]
    out = jax.ShapeDtypeStruct((LANES, s), F32)
    col = pl.BlockSpec((LANES, tm), lambda i: (0, i))
    return pl.pallas_call(
        _rope_tables_t_kernel,
        grid=(s // tm,),
        in_specs=[pl.BlockSpec((1, tm), lambda i: (0, i)), pl.BlockSpec((LANES, 1), lambda i: (0, 0))],
        out_specs=[col, col, col],
        out_shape=[out, out, out],
        compiler_params=_params(1),
        name="rope_tables_t",
    )(positions, freq)


def _mla_in_kernel(h_ref, g_ref, w_ref, qn_ref, kvn_ref, c_ref, s1_ref, s2_ref, cq_ref, ckv_ref, kr_ref):
    a = _rms(h_ref[...], g_ref[...]).astype(BF16)
    proj = _dot(a, w_ref[...])
    cq_ref[...] = _rms(proj[:, :Q_LORA_RANK], qn_ref[...]).astype(BF16)
    ckv_ref[...] = _rms(proj[:, Q_LORA_RANK:Q_LORA_RANK + KV_LORA_RANK], kvn_ref[...]).astype(BF16)
    kr = proj[:, Q_LORA_RANK + KV_LORA_RANK:]
    kr_ref[...] = _rope(kr, c_ref[...], s1_ref[...], s2_ref[...]).astype(BF16)


def _mla_in(h, norm_g, w_pad, q_norm, kv_norm, tabs, tm=1024):
    s, d = h.shape
    n = w_pad.shape[1]
    row = lambda w: pl.BlockSpec((tm, w), lambda i: (i, 0))
    const = lambda r, w: pl.BlockSpec((r, w), lambda i: (0, 0))
    return pl.pallas_call(
        _mla_in_kernel,
        grid=(s // tm,),
        in_specs=[row(d), const(1, d), const(d, n), const(1, Q_LORA_RANK), const(1, KV_LORA_RANK),
                  row(LANES), row(LANES), row(LANES)],
        out_specs=[row(Q_LORA_RANK), row(KV_LORA_RANK), row(LANES)],
        out_shape=[jax.ShapeDtypeStruct((s, Q_LORA_RANK), BF16),
                   jax.ShapeDtypeStruct((s, KV_LORA_RANK), BF16),
                   jax.ShapeDtypeStruct((s, LANES), BF16)],
        compiler_params=_params(1),
        name="mla_in",
    )(h, norm_g[None, :], w_pad, q_norm[None, :], kv_norm[None, :], *tabs)


def _mla_q_kernel(w_ref, cq_ref, c_ref, s1_ref, s2_ref, q_ref, *, heads, scale):
    acc = _dot_nt(w_ref[...], cq_ref[...])
    c, s1, s2 = c_ref[...], s1_ref[...], s2_ref[...]
    half = QK_ROPE_DIM // 2
    tq = q_ref.shape[3]
    for h in range(heads):
        lo = h * QK_PAD_DIM
        nope = (acc[lo:lo + LANES, :] * scale).astype(BF16)
        t = acc[lo + LANES:lo + QK_PAD_DIM, :]
        down = jnp.concatenate([t[LANES - half:], t[:LANES - half]], axis=0)
        up = jnp.concatenate([t[half:], t[:half]], axis=0)
        rope = ((t * c + down * s1 + up * s2) * scale).astype(BF16)
        for tt in range(acc.shape[1] // tq):
            q_ref[h, tt, :LANES, :] = nope[:, tt * tq:(tt + 1) * tq]
            q_ref[h, tt, LANES:, :] = rope[:, tt * tq:(tt + 1) * tq]


def _mla_q(cq, w_uq_pad_t, tabs_t, scale, t, tm=1024, heads_per_tile=8):
    s, r = cq.shape
    n = w_uq_pad_t.shape[0]
    tn = heads_per_tile * QK_PAD_DIM
    col = pl.BlockSpec((LANES, tm), lambda j, i: (0, i))
    return pl.pallas_call(
        functools.partial(_mla_q_kernel, heads=heads_per_tile, scale=scale),
        grid=(n // tn, s // tm),
        in_specs=[pl.BlockSpec((tn, r), lambda j, i: (j, 0)), pl.BlockSpec((tm, r), lambda j, i: (i, 0)),
                  col, col, col],
        out_specs=pl.BlockSpec((heads_per_tile, tm // t, QK_PAD_DIM, t), lambda j, i: (j, i, 0, 0)),
        out_shape=jax.ShapeDtypeStruct((n // QK_PAD_DIM, s // t, QK_PAD_DIM, t), BF16),
        compiler_params=_params(2),
        name="mla_q",
    )(w_uq_pad_t, cq, *tabs_t)


V_ROWS = V_HEAD_DIM + 16


def _mla_kv_kernel(ckv_ref, wk_ref, wv_ref, kr_ref, k_ref, v_ref, *, heads, t):
    ckv = ckv_ref[...]
    kn = _dot(ckv, wk_ref[...])
    vt = _dot_nt(wv_ref[...], ckv)
    kr = kr_ref[...]
    ones = jnp.ones((V_ROWS - V_HEAD_DIM, t), BF16)
    for h in range(heads):
        k_ref[:, h * QK_PAD_DIM:h * QK_PAD_DIM + LANES] = kn[:, h * LANES:(h + 1) * LANES].astype(BF16)
        k_ref[:, h * QK_PAD_DIM + LANES:(h + 1) * QK_PAD_DIM] = kr
        for tt in range(ckv.shape[0] // t):
            v_ref[h, tt, :V_HEAD_DIM, :] = vt[h * V_HEAD_DIM:(h + 1) * V_HEAD_DIM, tt * t:(tt + 1) * t].astype(BF16)
            v_ref[h, tt, V_HEAD_DIM:, :] = ones


def _mla_kv(ckv, w_uk, w_uv_t, kr, t, tm=1024, heads_per_tile=8):
    s, r = ckv.shape
    n = w_uk.shape[1]
    heads = n // LANES
    tn = heads_per_tile * LANES
    return pl.pallas_call(
        functools.partial(_mla_kv_kernel, heads=heads_per_tile, t=t),
        grid=(s // tm, n // tn),
        in_specs=[pl.BlockSpec((tm, r), lambda i, j: (i, 0)),
                  pl.BlockSpec((r, tn), lambda i, j: (0, j)),
                  pl.BlockSpec((tn, r), lambda i, j: (j, 0)),
                  pl.BlockSpec((tm, LANES), lambda i, j: (i, 0))],
        out_specs=[pl.BlockSpec((tm, 2 * tn), lambda i, j: (i, j)),
                   pl.BlockSpec((heads_per_tile, tm // t, V_ROWS, t), lambda i, j: (j, i, 0, 0))],
        out_shape=[jax.ShapeDtypeStruct((s, 2 * n), BF16),
                   jax.ShapeDtypeStruct((heads, s // t, V_ROWS, t), BF16)],
        compiler_params=_params(2),
        name="mla_kv",
    )(ckv, w_uk, w_uv_t, kr)


def _mla_attn_kernel(q_ref, k_ref, v_ref, *rest, t, unrolls, n_cast):
    w_refs, o_ref, w_bf16_refs = rest[:n_cast], rest[n_cast], rest[n_cast + 1:2 * n_cast + 1]
    m_sc, acc_sc, s_even, s_odd = rest[2 * n_cast + 1:]
    for w_ref, w_bf16_ref in zip(w_refs, w_bf16_refs):
        w_bf16_ref[...] = w_ref[...].astype(BF16)

    def q_tile(qi, carry):
        _mla_attn_q_tile(pl.program_id(1) * q_ref.shape[0] + qi, q_ref[qi], k_ref, v_ref,
                         o_ref.at[pl.ds(pl.multiple_of(qi * t, t), t), :],
                         m_sc, acc_sc, s_even, s_odd, t=t, unrolls=unrolls)
        return carry

    lax.fori_loop(0, q_ref.shape[0], q_tile, 0)


def _mla_attn_q_tile(i, q, k_ref, v_ref, o_ref, m_sc, acc_sc, s_even, s_odd, *, t, unrolls):
    m_sc[...] = jnp.full(m_sc.shape, -jnp.inf, F32)
    acc_sc[...] = jnp.zeros(acc_sc.shape, F32)

    def scores(j):
        return _dot(k_ref[pl.ds(pl.multiple_of(j * t, t), t), :], q)

    def softmax_pv(s_ref, j, diagonal):
        def load():
            s = s_ref[...]
            if diagonal:
                key = lax.broadcasted_iota(jnp.int32, (t, t), 0)
                query = lax.broadcasted_iota(jnp.int32, (t, t), 1)
                s = jnp.where(key <= query, s, -jnp.inf)
            return s
        m_prev = m_sc[...]
        m_next = jnp.maximum(m_prev, jnp.max(load(), axis=0, keepdims=True))
        m_sc[...] = m_next
        p = jnp.exp2(load() - m_sc[...]).astype(BF16)
        alpha = jnp.exp2(m_prev - m_next)
        acc_sc[...] = alpha * acc_sc[...] + _dot(v_ref[j], p)

    s_even[...] = scores(0)

    def pairs(first, n_pairs):
        for d in range(n_pairs):
            j = first + 2 * d
            s_odd[...] = scores(j + 1)
            softmax_pv(s_even, j, False)
            s_even[...] = scores(j + 2)
            softmax_pv(s_odd, j + 1, False)

    done = 0
    for u in unrolls:
        n_groups = (i // 2 - done) // u

        def group(g, carry, u=u, done=done):
            pairs(2 * (done + g * u), u)
            return carry

        lax.fori_loop(0, n_groups, group, 0)
        done = done + n_groups * u

    @pl.when(i % 2 == 0)
    def _():
        softmax_pv(s_even, i, True)

    @pl.when(i % 2 == 1)
    def _():
        s_odd[...] = scores(i)
        softmax_pv(s_even, i - 1, False)
        softmax_pv(s_odd, i, True)

    out_t = acc_sc[:V_HEAD_DIM, :] / acc_sc[V_HEAD_DIM:V_HEAD_DIM + 1, :]
    o_ref[...] = out_t.T.astype(o_ref.dtype)


BF16_SUBLANES = 16


def _mla_attn(q_t, k, v_t, t, cast_weights=(), q_tiles_per_step=4, unrolls=(4, 2, 1)):
    s = k.shape[0]
    h = MLA_HEADS
    g = q_tiles_per_step
    n_i = s // (t * g)
    n_steps = h * n_i
    assert unrolls[-1] == 1
    cast_specs = []
    for w in cast_weights:
        rows, cols = w.shape
        share = next(sh for sh in (1, 2, 4, 8) if rows % (n_steps // sh * BF16_SUBLANES) == 0)
        cast_specs.append(pl.BlockSpec((rows // (n_steps // share), cols),
                                       lambda hh, i, share=share: ((hh * n_i + i) // share, 0)))
    outs = pl.pallas_call(
        functools.partial(_mla_attn_kernel, t=t, unrolls=unrolls, n_cast=len(cast_weights)),
        grid=(h, n_i),
        in_specs=[pl.BlockSpec((None, g, QK_PAD_DIM, t), lambda hh, i: (hh, i, 0, 0)),
                  pl.BlockSpec((s, QK_PAD_DIM), lambda hh, i: (0, hh)),
                  pl.BlockSpec((None, s // t, V_ROWS, t), lambda hh, i: (hh, 0, 0, 0))] + cast_specs,
        out_specs=[pl.BlockSpec((g * t, V_HEAD_DIM), lambda hh, i: (i, hh))] + cast_specs,
        out_shape=[jax.ShapeDtypeStruct((s, h * V_HEAD_DIM), BF16)]
        + [jax.ShapeDtypeStruct(w.shape, BF16) for w in cast_weights],
        scratch_shapes=[pltpu.VMEM((1, t), F32),
                        pltpu.VMEM((V_ROWS, t), F32),
                        pltpu.VMEM((t, t), F32), pltpu.VMEM((t, t), F32)],
        compiler_params=_params(2),
        name="mla_attn",
    )(q_t, k, v_t, *cast_weights)
    return outs[0], outs[1:]


def _sb_attn_kernel(q_ref, k_ref, v_ref, o_ref, tri_sc, *bufs, t, group):
    n_tiles = q_ref.shape[0] // t
    n_groups = n_tiles // group
    w = 2 * t
    kinds = 6
    per_kind = 2 * group
    lom_sc, lb_sc, a_sc, wsum_sc, carry_sc, acc_sc = [
        [bufs[kind * per_kind + slot * group:kind * per_kind + (slot + 1) * group] for slot in range(2)]
        for kind in range(kinds)]
    tri_sc[...] = jnp.where(lax.broadcasted_iota(jnp.int32, (w, w), 0)
                            > lax.broadcasted_iota(jnp.int32, (w, w), 1), -1.0, 0.0).astype(BF16)

    def log_probs(q, start, width):
        z = _dot_nt(q, k_ref[pl.ds(start, width), :])
        neg_log_om = jnp.maximum(z, 0.0) + jnp.log2(1.0 + jnp.exp2(-jnp.abs(z)))
        return neg_log_om, z - neg_log_om

    def window_start(i):
        return pl.multiple_of(jnp.maximum(i - 1, 0) * t, t)

    def window_logits(slot, n, i, first_tile=False):
        i = jnp.minimum(i, n_tiles - 1)
        q = q_ref[pl.ds(pl.multiple_of(i * t, t), t), :]
        neg_log_om, log_beta = log_probs(q, window_start(i), w)
        lom, lb, wsum = lom_sc[slot][n], lb_sc[slot][n], wsum_sc[slot][n]
        strict = (lax.broadcasted_iota(jnp.int32, (t, t), 1)
                  < lax.broadcasted_iota(jnp.int32, (t, t), 0))
        if first_tile:
            halves = ((strict, neg_log_om[:, :t], log_beta[:, :t]), (None, None, None))
        else:
            halves = ((True, neg_log_om[:, :t], log_beta[:, :t]), (strict, neg_log_om[:, t:], log_beta[:, t:]))
        total = jnp.zeros((t, 1), F32)
        for half, (mask, nlo, lbeta) in enumerate(halves):
            cols = slice(half * t, (half + 1) * t)
            if mask is None:
                lom[:, cols] = jnp.zeros((t, t), BF16)
                lb[:, cols] = jnp.full((t, t), -jnp.inf, F32)
                continue
            if mask is not True:
                nlo = jnp.where(mask, nlo, 0.0)
                lbeta = jnp.where(mask, lbeta, -jnp.inf)
            lom[:, cols] = nlo.astype(BF16)
            lb[:, cols] = lbeta
            total = total + jnp.sum(nlo, axis=1, keepdims=True)
        wsum[...] = jnp.broadcast_to(-total, (t, LANES))

    def window_weights(slot, n):
        after = _dot(lom_sc[slot][n][...], tri_sc[...])
        a_sc[slot][n][...] = jnp.exp2(lb_sc[slot][n][...] + after).astype(BF16)
        carry_sc[slot][n][...] = wsum_sc[slot][n][...]

    def window_values(slot, n, i):
        acc_sc[slot][n][...] = _dot(a_sc[slot][n][...], v_ref[pl.ds(window_start(i), w), :])

    def earlier_tiles(slot, n, i):
        carry, acc = carry_sc[slot][n], acc_sc[slot][n]
        rows = pl.ds(pl.multiple_of(i * t, t), t)
        q = q_ref[rows, :]

        def more(state):
            j, carry_max = state
            return (j >= 0) & (carry_max > SB_LOG2_WEIGHT_FLOOR)

        def body(state):
            j, _ = state
            start = pl.multiple_of(j * t, t)
            neg_log_om, log_beta = log_probs(q, start, t)
            after = _dot(neg_log_om.astype(BF16), tri_sc[:t, :t]) + jnp.tile(carry[...], (1, t // LANES))
            a = jnp.exp2(log_beta + after)
            acc[...] += _dot(a.astype(BF16), v_ref[pl.ds(start, t), :])
            carry[...] -= jnp.sum(neg_log_om, axis=1, keepdims=True)
            return j - 1, jnp.max(carry[...])

        lax.while_loop(more, body, (i - 2, jnp.max(carry[...])))
        o_ref[rows, :] = acc[...].astype(o_ref.dtype)

    def step(g, slot):
        for n in range(group):
            window_logits(1 - slot, n, (g + 1) * group + n)
        for n in range(group):
            window_weights(slot, n)
        for n in range(group):
            window_values(1 - slot, n, (g - 1) * group + n)
        for n in range(group):
            earlier_tiles(1 - slot, n, (g - 1) * group + n)

    def step_pair(p, _):
        step(2 * p + 1, 1)
        step(2 * p + 2, 0)
        return 0

    for n in range(group):
        window_logits(0, n, n, first_tile=(n == 0))
    for n in range(group):
        window_weights(0, n)
    for n in range(group):
        window_logits(1, n, group + n)
    lax.fori_loop(0, n_groups // 2, step_pair, 0)


def _sb_attn(qkv, t=256, group=1):
    s = qkv.shape[0]
    h, d = SB_HEADS, SB_HEAD_DIM
    assert s % (2 * group * t) == 0
    per_kind = 2 * group
    return pl.pallas_call(
        functools.partial(_sb_attn_kernel, t=t, group=group),
        grid=(h,),
        in_specs=[pl.BlockSpec((s, d), lambda hh: (0, hh)),
                  pl.BlockSpec((s, d), lambda hh: (0, h + hh)),
                  pl.BlockSpec((s, d), lambda hh: (0, 2 * h + hh))],
        out_specs=pl.BlockSpec((s, d), lambda hh: (0, hh)),
        out_shape=jax.ShapeDtypeStruct((s, h * d), BF16),
        scratch_shapes=[pltpu.VMEM((2 * t, 2 * t), BF16)]
        + per_kind * [pltpu.VMEM((t, 2 * t), BF16)]
        + per_kind * [pltpu.VMEM((t, 2 * t), F32)]
        + per_kind * [pltpu.VMEM((t, 2 * t), BF16)]
        + per_kind * [pltpu.VMEM((t, LANES), F32)]
        + per_kind * [pltpu.VMEM((t, LANES), F32)]
        + per_kind * [pltpu.VMEM((t, d), F32)],
        compiler_params=_params(1),
        name="sb_attn",
    )(qkv, qkv, qkv)


def _normalize_rows_once(h_ref, g_ref, a_sc):
    @pl.when(pl.program_id(1) == 0)
    def _():
        a_sc[...] = _rms(h_ref[...], g_ref[...]).astype(BF16)


def _norm_matmul_scale_kernel(h_ref, g_ref, w_ref, cs_ref, o_ref, a_sc):
    _normalize_rows_once(h_ref, g_ref, a_sc)
    o_ref[...] = (_dot(a_sc[...], w_ref[...].astype(BF16)) * cs_ref[...]).astype(o_ref.dtype)


def _norm_matmul_scale(h, norm_g, w, layer, col_scale, tm=1024, tn=1536):
    s, k = h.shape
    n = w.shape[2]
    return pl.pallas_call(
        _norm_matmul_scale_kernel,
        grid=(s // tm, n // tn),
        in_specs=[pl.BlockSpec((tm, k), lambda i, j: (i, 0)), pl.BlockSpec((1, k), lambda i, j: (0, 0)),
                  pl.BlockSpec((None, k, tn), lambda i, j: (layer, 0, j)),
                  pl.BlockSpec((1, tn), lambda i, j: (0, j))],
        out_specs=pl.BlockSpec((tm, tn), lambda i, j: (i, j)),
        out_shape=jax.ShapeDtypeStruct((s, n), BF16),
        scratch_shapes=[pltpu.VMEM((tm, k), BF16)],
        compiler_params=_params(2),
        name="norm_matmul_scale",
    )(h, norm_g[None, :], w, col_scale[None, :])


def _matmul_residual_kernel(a_ref, w_ref, r_ref, o_ref):
    o_ref[...] = r_ref[...] + _dot(a_ref[...], w_ref[...].astype(BF16))


def _matmul_residual(a, w, layer, res, tm, tn):
    s, k = a.shape
    n = w.shape[2]
    return pl.pallas_call(
        _matmul_residual_kernel,
        grid=(s // tm, n // tn),
        in_specs=[pl.BlockSpec((tm, k), lambda i, j: (i, 0)),
                  pl.BlockSpec((None, k, tn), lambda i, j: (layer, 0, j)),
                  pl.BlockSpec((tm, tn), lambda i, j: (i, j))],
        out_specs=pl.BlockSpec((tm, tn), lambda i, j: (i, j)),
        out_shape=jax.ShapeDtypeStruct((s, n), F32),
        compiler_params=_params(2),
        name="matmul_residual",
    )(a, w, res)


def _matmul_residual_norm_kernel(a_ref, w_ref, r_ref, g_ref, o_ref):
    kk = pl.program_id(1)

    @pl.when(kk == 0)
    def _():
        o_ref[...] = r_ref[...]

    o_ref[...] += _dot(a_ref[...], w_ref[...].astype(BF16))

    @pl.when(kk == pl.num_programs(1) - 1)
    def _():
        o_ref[...] = _rms(o_ref[...], g_ref[...])


def _matmul_residual_norm(a, w, layer, res, norm_g, tm=512, tk=2816):
    s, k = a.shape
    n = w.shape[2]
    return pl.pallas_call(
        _matmul_residual_norm_kernel,
        grid=(s // tm, k // tk),
        in_specs=[pl.BlockSpec((tm, tk), lambda i, kk: (i, kk)),
                  pl.BlockSpec((None, tk, n), lambda i, kk: (layer, kk, 0)),
                  pl.BlockSpec((tm, n), lambda i, kk: (i, 0)),
                  pl.BlockSpec((1, n), lambda i, kk: (0, 0))],
        out_specs=pl.BlockSpec((tm, n), lambda i, kk: (i, 0)),
        out_shape=jax.ShapeDtypeStruct((s, n), F32),
        compiler_params=_params(2),
        name="matmul_residual_norm",
    )(a, w, res, norm_g[None, :])


def _ffn_up_kernel(h_ref, g_ref, wg_ref, wu_ref, o_ref, a_sc):
    _normalize_rows_once(h_ref, g_ref, a_sc)
    a = a_sc[...]
    g = _dot(a, wg_ref[...].astype(BF16))
    u = _dot(a, wu_ref[...].astype(BF16))
    o_ref[...] = (g * jax.nn.sigmoid(g) * u).astype(o_ref.dtype)


def _ffn_up(h, norm_g, w_gate_up, layer, tm=1024, tn=512):
    s, k = h.shape
    d_ff = w_gate_up.shape[2] // 2
    nj = d_ff // tn
    return pl.pallas_call(
        _ffn_up_kernel,
        grid=(s // tm, nj),
        in_specs=[pl.BlockSpec((tm, k), lambda i, j: (i, 0)), pl.BlockSpec((1, k), lambda i, j: (0, 0)),
                  pl.BlockSpec((None, k, tn), lambda i, j: (layer, 0, j)),
                  pl.BlockSpec((None, k, tn), lambda i, j: (layer, 0, nj + j))],
        out_specs=pl.BlockSpec((tm, tn), lambda i, j: (i, j)),
        out_shape=jax.ShapeDtypeStruct((s, d_ff), BF16),
        scratch_shapes=[pltpu.VMEM((tm, k), BF16)],
        compiler_params=_params(2),
        name="ffn_up",
    )(h, norm_g[None, :], w_gate_up, w_gate_up)


def _ffn(h, norm_g, w_gate_up, w_down, layer, out_norm_g=None):
    act = _ffn_up(h, norm_g, w_gate_up, layer)
    if out_norm_g is not None:
        return _matmul_residual_norm(act, w_down, layer, h, out_norm_g)
    return _matmul_residual(act, w_down, layer, h, tm=1024, tn=512)


def kernel(x, positions, attn_norm, mla_w_in, mla_q_norm, mla_kv_norm, mla_w_uq, mla_w_ukv, mla_w_o,
           sb_w_in, sb_w_o, ffn_norm, ffn_w_gate_up, ffn_w_down, final_norm):
    b, s, d = x.shape
    assert b == 1 and d == D_MODEL
    h = x.reshape(s, d)

    attn_tile = 512
    tabs = _rope_tables(positions.reshape(s, 1))
    tabs_t = _rope_tables_t(positions.reshape(1, s))
    w_in_pad = jnp.pad(mla_w_in[0], ((0, 0), (0, LANES - QK_ROPE_DIM))).astype(BF16)
    qk_dim = QK_NOPE_DIM + QK_ROPE_DIM
    w_uq_pad = jnp.pad(mla_w_uq[0].reshape(Q_LORA_RANK, MLA_HEADS, qk_dim),
                       ((0, 0), (0, 0), (0, QK_PAD_DIM - qk_dim)))
    w_uq_pad_t = w_uq_pad.reshape(Q_LORA_RANK, MLA_HEADS * QK_PAD_DIM).T.astype(BF16)
    w_ukv = mla_w_ukv[0].reshape(KV_LORA_RANK, MLA_HEADS, QK_NOPE_DIM + V_HEAD_DIM)
    w_uk = w_ukv[:, :, :QK_NOPE_DIM].reshape(KV_LORA_RANK, MLA_HEADS * QK_NOPE_DIM).astype(BF16)
    w_uv_t = w_ukv[:, :, QK_NOPE_DIM:].reshape(KV_LORA_RANK, MLA_HEADS * V_HEAD_DIM).T.astype(BF16)

    cq, ckv, kr = _mla_in(h, attn_norm[0], w_in_pad, mla_q_norm[0], mla_kv_norm[0], tabs)
    q_t = _mla_q(cq, w_uq_pad_t, tabs_t, qk_dim ** -0.5 * LOG2_E, attn_tile)
    k, v_t = _mla_kv(ckv, w_uk, w_uv_t, kr, attn_tile)
    to_cast = (mla_w_o, sb_w_in, sb_w_o, ffn_w_gate_up, ffn_w_down)
    o, cast = _mla_attn(q_t, k, v_t, attn_tile, cast_weights=[w.reshape(-1, w.shape[2]) for w in to_cast])
    w_o, w_sb_in, w_sb_o, w_gate_up, w_down = [c.reshape(w.shape) for c, w in zip(cast, to_cast)]
    h = _matmul_residual(o, w_o, 0, h, tm=1024, tn=1024)
    h = _ffn(h, ffn_norm[0], w_gate_up, w_down, 0)

    n_q = SB_HEADS * SB_HEAD_DIM
    col_scale = jnp.concatenate([jnp.full((n_q,), SB_HEAD_DIM ** -0.5 * LOG2_E, F32), jnp.ones((2 * n_q,), F32)])
    qkv = _norm_matmul_scale(h, attn_norm[1], w_sb_in, 0, col_scale)
    o = _sb_attn(qkv)
    h = _matmul_residual(o, w_sb_o, 0, h, tm=1024, tn=1024)
    return _ffn(h, ffn_norm[1], w_gate_up, w_down, 1, out_norm_g=final_norm).reshape(b, s, d)
```

```python
import functools

import jax
import jax.numpy as jnp
from jax import lax
from jax.experimental import pallas as pl
from jax.experimental.pallas import tpu as pltpu

D_MODEL = 2048
MLA_HEADS = 16
Q_LORA_RANK = 512
KV_LORA_RANK = 512
QK_NOPE_DIM = 128
QK_ROPE_DIM = 64
V_HEAD_DIM = 128
ROPE_THETA = 10000.0
SB_HEADS = 16
SB_HEAD_DIM = 128
RMS_EPS = 1e-6

LANES = 128
QK_PAD_DIM = 2 * LANES
VMEM_LIMIT_BYTES = 56 * 1024 * 1024

LOG2_E = 1.4426950408889634
SB_LOG2_WEIGHT_FLOOR = -120.0 * LOG2_E

F32 = jnp.float32
BF16 = jnp.bfloat16


def _params(n_axes):
    return pltpu.CompilerParams(
        dimension_semantics=("arbitrary",) * n_axes, vmem_limit_bytes=VMEM_LIMIT_BYTES)


def _dot(a, b):
    return jnp.dot(a, b, preferred_element_type=F32)


def _dot_nt(a, b):
    return lax.dot_general(a, b, (((1,), (1,)), ((), ())), preferred_element_type=F32)


def _rms(x, g):
    return x * lax.rsqrt(jnp.mean(x * x, axis=-1, keepdims=True) + RMS_EPS) * g


def _rope(t, c, s1, s2):
    half = QK_ROPE_DIM // 2
    return t * c + pltpu.roll(t, half, 1) * s1 + pltpu.roll(t, LANES - half, 1) * s2


def _rope_tables_kernel(pos_ref, freq_ref, c_ref, s1_ref, s2_ref):
    half = QK_ROPE_DIM // 2
    ang = pos_ref[...].astype(F32) * freq_ref[...]
    lane = lax.broadcasted_iota(jnp.int32, ang.shape, 1)
    cos, sin = jnp.cos(ang), jnp.sin(ang)
    c_ref[...] = jnp.where(lane < 2 * half, cos, 0.0)
    s1_ref[...] = jnp.where((lane >= half) & (lane < 2 * half), sin, 0.0)
    s2_ref[...] = jnp.where(lane < half, -sin, 0.0)


def _rope_tables(positions, tm=2048):
    s = positions.shape[0]
    inv_freq = ROPE_THETA ** (-jnp.arange(0, QK_ROPE_DIM, 2, dtype=F32) / QK_ROPE_DIM)
    freq = jnp.concatenate([inv_freq, inv_freq, jnp.zeros((LANES - QK_ROPE_DIM,), F32)])[None, :]
    out = jax.ShapeDtypeStruct((s, LANES), F32)
    row = pl.BlockSpec((tm, LANES), lambda i: (i, 0))
    return pl.pallas_call(
        _rope_tables_kernel,
        grid=(s // tm,),
        in_specs=[pl.BlockSpec((tm, 1), lambda i: (i, 0)), pl.BlockSpec((1, LANES), lambda i: (0, 0))],
        out_specs=[row, row, row],
        out_shape=[out, out, out],
        compiler_params=_params(1),
        name="rope_tables",
    )(positions, freq)


def _rope_tables_t_kernel(pos_ref, freq_ref, c_ref, s1_ref, s2_ref):
    half = QK_ROPE_DIM // 2
    ang = freq_ref[...] * pos_ref[...].astype(F32)
    r = lax.broadcasted_iota(jnp.int32, ang.shape, 0)
    cos, sin = jnp.cos(ang), jnp.sin(ang)
    c_ref[...] = jnp.where(r < 2 * half, cos, 0.0)
    s1_ref[...] = jnp.where((r >= half) & (r < 2 * half), sin, 0.0)
    s2_ref[...] = jnp.where(r < half, -sin, 0.0)


def _rope_tables_t(positions, tm=2048):
    s = positions.shape[1]
    inv_freq = ROPE_THETA ** (-jnp.arange(0, QK_ROPE_DIM, 2, dtype=F32) / QK_ROPE_DIM)
    freq = jnp.concatenate([inv_freq, inv_freq, jnp.zeros((LANES - QK_ROPE_DIM,), F32)])[:, None]
    out = jax.ShapeDtypeStruct((LANES, s), F32)
    col = pl.BlockSpec((LANES, tm), lambda i: (0, i))
    return pl.pallas_call(
        _rope_tables_t_kernel,
        grid=(s // tm,),
        in_specs=[pl.BlockSpec((1, tm), lambda i: (0, i)), pl.BlockSpec((LANES, 1), lambda i: (0, 0))],
        out_specs=[col, col, col],
        out_shape=[out, out, out],
        compiler_params=_params(1),
        name="rope_tables_t",
    )(positions, freq)


def _mla_in_kernel(h_ref, g_ref, w_ref, qn_ref, kvn_ref, c_ref, s1_ref, s2_ref, cq_ref, ckv_ref, kr_ref):
    a = _rms(h_ref[...], g_ref[...]).astype(BF16)
    proj = _dot(a, w_ref[...])
    cq_ref[...] = _rms(proj[:, :Q_LORA_RANK], qn_ref[...]).astype(BF16)
    ckv_ref[...] = _rms(proj[:, Q_LORA_RANK:Q_LORA_RANK + KV_LORA_RANK], kvn_ref[...]).astype(BF16)
    kr = proj[:, Q_LORA_RANK + KV_LORA_RANK:]
    kr_ref[...] = _rope(kr, c_ref[...], s1_ref[...], s2_ref[...]).astype(BF16)


def _mla_in(h, norm_g, w_pad, q_norm, kv_norm, tabs, tm=1024):
    s, d = h.shape
    n = w_pad.shape[1]
    row = lambda w: pl.BlockSpec((tm, w), lambda i: (i, 0))
    const = lambda r, w: pl.BlockSpec((r, w), lambda i: (0, 0))
    return pl.pallas_call(
        _mla_in_kernel,
        grid=(s // tm,),
        in_specs=[row(d), const(1, d), const(d, n), const(1, Q_LORA_RANK), const(1, KV_LORA_RANK),
                  row(LANES), row(LANES), row(LANES)],
        out_specs=[row(Q_LORA_RANK), row(KV_LORA_RANK), row(LANES)],
        out_shape=[jax.ShapeDtypeStruct((s, Q_LORA_RANK), BF16),
                   jax.ShapeDtypeStruct((s, KV_LORA_RANK), BF16),
                   jax.ShapeDtypeStruct((s, LANES), BF16)],
        compiler_params=_params(1),
        name="mla_in",
    )(h, norm_g[None, :], w_pad, q_norm[None, :], kv_norm[None, :], *tabs)


def _mla_q_kernel(w_ref, cq_ref, c_ref, s1_ref, s2_ref, q_ref, *, heads, scale):
    acc = _dot_nt(w_ref[...], cq_ref[...])
    c, s1, s2 = c_ref[...], s1_ref[...], s2_ref[...]
    half = QK_ROPE_DIM // 2
    tq = q_ref.shape[3]
    for h in range(heads):
        lo = h * QK_PAD_DIM
        nope = (acc[lo:lo + LANES, :] * scale).astype(BF16)
        t = acc[lo + LANES:lo + QK_PAD_DIM, :]
        down = jnp.concatenate([t[LANES - half:], t[:LANES - half]], axis=0)
        up = jnp.concatenate([t[half:], t[:half]], axis=0)
        rope = ((t * c + down * s1 + up * s2) * scale).astype(BF16)
        for tt in range(acc.shape[1] // tq):
            q_ref[h, tt, :LANES, :] = nope[:, tt * tq:(tt + 1) * tq]
            q_ref[h, tt, LANES:, :] = rope[:, tt * tq:(tt + 1) * tq]


def _mla_q(cq, w_uq_pad_t, tabs_t, scale, t, tm=1024, heads_per_tile=8):
    s, r = cq.shape
    n = w_uq_pad_t.shape[0]
    tn = heads_per_tile * QK_PAD_DIM
    col = pl.BlockSpec((LANES, tm), lambda j, i: (0, i))
    return pl.pallas_call(
        functools.partial(_mla_q_kernel, heads=heads_per_tile, scale=scale),
        grid=(n // tn, s // tm),
        in_specs=[pl.BlockSpec((tn, r), lambda j, i: (j, 0)), pl.BlockSpec((tm, r), lambda j, i: (i, 0)),
                  col, col, col],
        out_specs=pl.BlockSpec((heads_per_tile, tm // t, QK_PAD_DIM, t), lambda j, i: (j, i, 0, 0)),
        out_shape=jax.ShapeDtypeStruct((n // QK_PAD_DIM, s // t, QK_PAD_DIM, t), BF16),
        compiler_params=_params(2),
        name="mla_q",
    )(w_uq_pad_t, cq, *tabs_t)


V_ROWS = V_HEAD_DIM + 16


def _mla_kv_kernel(ckv_ref, wk_ref, wv_ref, kr_ref, k_ref, v_ref, *, heads, t):
    ckv = ckv_ref[...]
    kn = _dot(ckv, wk_ref[...])
    vt = _dot_nt(wv_ref[...], ckv)
    kr = kr_ref[...]
    ones = jnp.ones((V_ROWS - V_HEAD_DIM, t), BF16)
    for h in range(heads):
        k_ref[:, h * QK_PAD_DIM:h * QK_PAD_DIM + LANES] = kn[:, h * LANES:(h + 1) * LANES].astype(BF16)
        k_ref[:, h * QK_PAD_DIM + LANES:(h + 1) * QK_PAD_DIM] = kr
        for tt in range(ckv.shape[0] // t):
            v_ref[h, tt, :V_HEAD_DIM, :] = vt[h * V_HEAD_DIM:(h + 1) * V_HEAD_DIM, tt * t:(tt + 1) * t].astype(BF16)
            v_ref[h, tt, V_HEAD_DIM:, :] = ones


def _mla_kv(ckv, w_uk, w_uv_t, kr, t, tm=1024, heads_per_tile=8):
    s, r = ckv.shape
    n = w_uk.shape[1]
    heads = n // LANES
    tn = heads_per_tile * LANES
    return pl.pallas_call(
        functools.partial(_mla_kv_kernel, heads=heads_per_tile, t=t),
        grid=(s // tm, n // tn),
        in_specs=[pl.BlockSpec((tm, r), lambda i, j: (i, 0)),
                  pl.BlockSpec((r, tn), lambda i, j: (0, j)),
                  pl.BlockSpec((tn, r), lambda i, j: (j, 0)),
                  pl.BlockSpec((tm, LANES), lambda i, j: (i, 0))],
        out_specs=[pl.BlockSpec((tm, 2 * tn), lambda i, j: (i, j)),
                   pl.BlockSpec((heads_per_tile, tm // t, V_ROWS, t), lambda i, j: (j, i, 0, 0))],
        out_shape=[jax.ShapeDtypeStruct((s, 2 * n), BF16),
                   jax.ShapeDtypeStruct((heads, s // t, V_ROWS, t), BF16)],
        compiler_params=_params(2),
        name="mla_kv",
    )(ckv, w_uk, w_uv_t, kr)


def _mla_attn_kernel(q_ref, k_ref, v_ref, *rest, t, unrolls, n_cast):
    w_refs, o_ref, w_bf16_refs = rest[:n_cast], rest[n_cast], rest[n_cast + 1:2 * n_cast + 1]
    m_sc, acc_sc, s_even, s_odd = rest[2 * n_cast + 1:]
    for w_ref, w_bf16_ref in zip(w_refs, w_bf16_refs):
        w_bf16_ref[...] = w_ref[...].astype(BF16)

    def q_tile(qi, carry):
        _mla_attn_q_tile(pl.program_id(1) * q_ref.shape[0] + qi, q_ref[qi], k_ref, v_ref,
                         o_ref.at[pl.ds(pl.multiple_of(qi * t, t), t), :],
                         m_sc, acc_sc, s_even, s_odd, t=t, unrolls=unrolls)
        return carry

    lax.fori_loop(0, q_ref.shape[0], q_tile, 0)


def _mla_attn_q_tile(i, q, k_ref, v_ref, o_ref, m_sc, acc_sc, s_even, s_odd, *, t, unrolls):
    m_sc[...] = jnp.full(m_sc.shape, -jnp.inf, F32)
    acc_sc[...] = jnp.zeros(acc_sc.shape, F32)

    def scores(j):
        return _dot(k_ref[pl.ds(pl.multiple_of(j * t, t), t), :], q)

    def softmax_pv(s_ref, j, diagonal):
        def load():
            s = s_ref[...]
            if diagonal:
                key = lax.broadcasted_iota(jnp.int32, (t, t), 0)
                query = lax.broadcasted_iota(jnp.int32, (t, t), 1)
                s = jnp.where(key <= query, s, -jnp.inf)
            return s
        m_prev = m_sc[...]
        m_next = jnp.maximum(m_prev, jnp.max(load(), axis=0, keepdims=True))
        m_sc[...] = m_next
        p = jnp.exp2(load() - m_sc[...]).astype(BF16)
        alpha = jnp.exp2(m_prev - m_next)
        acc_sc[...] = alpha * acc_sc[...] + _dot(v_ref[j], p)

    s_even[...] = scores(0)

    def pairs(first, n_pairs):
        for d in range(n_pairs):
            j = first + 2 * d
            s_odd[...] = scores(j + 1)
            softmax_pv(s_even, j, False)
            s_even[...] = scores(j + 2)
            softmax_pv(s_odd, j + 1, False)

    done = 0
    for u in unrolls:
        n_groups = (i // 2 - done) // u

        def group(g, carry, u=u, done=done):
            pairs(2 * (done + g * u), u)
            return carry

        lax.fori_loop(0, n_groups, group, 0)
        done = done + n_groups * u

    @pl.when(i % 2 == 0)
    def _():
        softmax_pv(s_even, i, True)

    @pl.when(i % 2 == 1)
    def _():
        s_odd[...] = scores(i)
        softmax_pv(s_even, i - 1, False)
        softmax_pv(s_odd, i, True)

    out_t = acc_sc[:V_HEAD_DIM, :] / acc_sc[V_HEAD_DIM:V_HEAD_DIM + 1, :]
    o_ref[...] = out_t.T.astype(o_ref.dtype)


BF16_SUBLANES = 16


def _mla_attn(q_t, k, v_t, t, cast_weights=(), q_tiles_per_step=4, unrolls=(8, 4, 2, 1)):
    s = k.shape[0]
    h = MLA_HEADS
    g = q_tiles_per_step
    n_i = s // (t * g)
    n_steps = h * n_i
    assert unrolls[-1] == 1
    cast_specs = []
    for w in cast_weights:
        rows, cols = w.shape
        share = next(sh for sh in (1, 2, 4, 8) if rows % (n_steps // sh * BF16_SUBLANES) == 0)
        cast_specs.append(pl.BlockSpec((rows // (n_steps // share), cols),
                                       lambda hh, i, share=share: ((hh * n_i + i) // share, 0)))
    outs = pl.pallas_call(
        functools.partial(_mla_attn_kernel, t=t, unrolls=unrolls, n_cast=len(cast_weights)),
        grid=(h, n_i),
        in_specs=[pl.BlockSpec((None, g, QK_PAD_DIM, t), lambda hh, i: (hh, i, 0, 0)),
                  pl.BlockSpec((s, QK_PAD_DIM), lambda hh, i: (0, hh)),
                  pl.BlockSpec((None, s // t, V_ROWS, t), lambda hh, i: (hh, 0, 0, 0))] + cast_specs,
        out_specs=[pl.BlockSpec((g * t, V_HEAD_DIM), lambda hh, i: (i, hh))] + cast_specs,
        out_shape=[jax.ShapeDtypeStruct((s, h * V_HEAD_DIM), BF16)]
        + [jax.ShapeDtypeStruct(w.shape, BF16) for w in cast_weights],
        scratch_shapes=[pltpu.VMEM((1, t), F32),
                        pltpu.VMEM((V_ROWS, t), F32),
                        pltpu.VMEM((t, t), F32), pltpu.VMEM((t, t), F32)],
        compiler_params=_params(2),
        name="mla_attn",
    )(q_t, k, v_t, *cast_weights)
    return outs[0], outs[1:]


def _sb_attn_kernel(q_ref, k_ref, v_ref, o_ref, tri_sc, *bufs, t, group):
    n_tiles = q_ref.shape[0] // t
    n_groups = n_tiles // group
    w = 2 * t
    kinds = 6
    per_kind = 2 * group
    lom_sc, lb_sc, a_sc, wsum_sc, carry_sc, acc_sc = [
        [bufs[kind * per_kind + slot * group:kind * per_kind + (slot + 1) * group] for slot in range(2)]
        for kind in range(kinds)]
    tri_sc[...] = jnp.where(lax.broadcasted_iota(jnp.int32, (w, w), 0)
                            > lax.broadcasted_iota(jnp.int32, (w, w), 1), -1.0, 0.0).astype(BF16)

    def log_probs(q, start, width):
        z = _dot_nt(q, k_ref[pl.ds(start, width), :])
        neg_log_om = jnp.maximum(z, 0.0) + jnp.log2(1.0 + jnp.exp2(-jnp.abs(z)))
        return neg_log_om, z - neg_log_om

    def window_start(i):
        return pl.multiple_of(jnp.maximum(i - 1, 0) * t, t)

    def window_logits(slot, n, i, first_tile=False):
        i = jnp.minimum(i, n_tiles - 1)
        q = q_ref[pl.ds(pl.multiple_of(i * t, t), t), :]
        neg_log_om, log_beta = log_probs(q, window_start(i), w)
        lom, lb, wsum = lom_sc[slot][n], lb_sc[slot][n], wsum_sc[slot][n]
        strict = (lax.broadcasted_iota(jnp.int32, (t, t), 1)
                  < lax.broadcasted_iota(jnp.int32, (t, t), 0))
        if first_tile:
            halves = ((strict, neg_log_om[:, :t], log_beta[:, :t]), (None, None, None))
        else:
            halves = ((True, neg_log_om[:, :t], log_beta[:, :t]), (strict, neg_log_om[:, t:], log_beta[:, t:]))
        total = jnp.zeros((t, 1), F32)
        for half, (mask, nlo, lbeta) in enumerate(halves):
            cols = slice(half * t, (half + 1) * t)
            if mask is None:
                lom[:, cols] = jnp.zeros((t, t), BF16)
                lb[:, cols] = jnp.full((t, t), -jnp.inf, F32)
                continue
            if mask is not True:
                nlo = jnp.where(mask, nlo, 0.0)
                lbeta = jnp.where(mask, lbeta, -jnp.inf)
            lom[:, cols] = nlo.astype(BF16)
            lb[:, cols] = lbeta
            total = total + jnp.sum(nlo, axis=1, keepdims=True)
        wsum[...] = jnp.broadcast_to(-total, (t, LANES))

    def window_weights(slot, n):
        lom, lb, a = lom_sc[slot][n], lb_sc[slot][n], a_sc[slot][n]
        a[:, :t] = jnp.exp2(lb[:, :t] + _dot(lom[...], tri_sc[:, :t])).astype(BF16)
        a[:, t:] = jnp.exp2(lb[:, t:] + _dot(lom[:, t:], tri_sc[t:, t:])).astype(BF16)
        carry_sc[slot][n][...] = wsum_sc[slot][n][...]

    def window_values(slot, n, i):
        acc_sc[slot][n][...] = _dot(a_sc[slot][n][...], v_ref[pl.ds(window_start(i), w), :])

    def earlier_tiles(slot, n, i):
        carry, acc = carry_sc[slot][n], acc_sc[slot][n]
        rows = pl.ds(pl.multiple_of(i * t, t), t)
        q = q_ref[rows, :]

        def more(state):
            j, carry_max = state
            return (j >= 0) & (carry_max > SB_LOG2_WEIGHT_FLOOR)

        def body(state):
            j, _ = state
            start = pl.multiple_of(j * t, t)
            neg_log_om, log_beta = log_probs(q, start, t)
            after = _dot(neg_log_om.astype(BF16), tri_sc[:t, :t]) + jnp.tile(carry[...], (1, t // LANES))
            a = jnp.exp2(log_beta + after)
            acc[...] += _dot(a.astype(BF16), v_ref[pl.ds(start, t), :])
            carry[...] -= jnp.sum(neg_log_om, axis=1, keepdims=True)
            return j - 1, jnp.max(carry[...])

        lax.while_loop(more, body, (i - 2, jnp.max(carry[...])))
        o_ref[rows, :] = acc[...].astype(o_ref.dtype)

    def step(g, slot):
        for n in range(group):
            window_logits(1 - slot, n, (g + 1) * group + n)
        for n in range(group):
            window_weights(slot, n)
        for n in range(group):
            window_values(1 - slot, n, (g - 1) * group + n)
        for n in range(group):
            earlier_tiles(1 - slot, n, (g - 1) * group + n)

    def step_pair(p, _):
        step(2 * p + 1, 1)
        step(2 * p + 2, 0)
        return 0

    for n in range(group):
        window_logits(0, n, n, first_tile=(n == 0))
    for n in range(group):
        window_weights(0, n)
    for n in range(group):
        window_logits(1, n, group + n)
    lax.fori_loop(0, n_groups // 2, step_pair, 0)


def _sb_attn(qkv, t=256, group=1):
    s = qkv.shape[0]
    h, d = SB_HEADS, SB_HEAD_DIM
    assert s % (2 * group * t) == 0
    per_kind = 2 * group
    return pl.pallas_call(
        functools.partial(_sb_attn_kernel, t=t, group=group),
        grid=(h,),
        in_specs=[pl.BlockSpec((s, d), lambda hh: (0, hh)),
                  pl.BlockSpec((s, d), lambda hh: (0, h + hh)),
                  pl.BlockSpec((s, d), lambda hh: (0, 2 * h + hh))],
        out_specs=pl.BlockSpec((s, d), lambda hh: (0, hh)),
        out_shape=jax.ShapeDtypeStruct((s, h * d), BF16),
        scratch_shapes=[pltpu.VMEM((2 * t, 2 * t), BF16)]
        + per_kind * [pltpu.VMEM((t, 2 * t), BF16)]
        + per_kind * [pltpu.VMEM((t, 2 * t), F32)]
        + per_kind * [pltpu.VMEM((t, 2 * t), BF16)]
        + per_kind * [pltpu.VMEM((t, LANES), F32)]
        + per_kind * [pltpu.VMEM((t, LANES), F32)]
        + per_kind * [pltpu.VMEM((t, d), F32)],
        compiler_params=_params(1),
        name="sb_attn",
    )(qkv, qkv, qkv)


def _normalize_rows_once(h_ref, g_ref, a_sc):
    @pl.when(pl.program_id(1) == 0)
    def _():
        a_sc[...] = _rms(h_ref[...], g_ref[...]).astype(BF16)


def _norm_matmul_scale_kernel(h_ref, g_ref, w_ref, cs_ref, o_ref, a_sc):
    _normalize_rows_once(h_ref, g_ref, a_sc)
    o_ref[...] = (_dot(a_sc[...], w_ref[...].astype(BF16)) * cs_ref[...]).astype(o_ref.dtype)


def _norm_matmul_scale(h, norm_g, w, layer, col_scale, tm=1024, tn=1536):
    s, k = h.shape
    n = w.shape[2]
    return pl.pallas_call(
        _norm_matmul_scale_kernel,
        grid=(s // tm, n // tn),
        in_specs=[pl.BlockSpec((tm, k), lambda i, j: (i, 0)), pl.BlockSpec((1, k), lambda i, j: (0, 0)),
                  pl.BlockSpec((None, k, tn), lambda i, j: (layer, 0, j)),
                  pl.BlockSpec((1, tn), lambda i, j: (0, j))],
        out_specs=pl.BlockSpec((tm, tn), lambda i, j: (i, j)),
        out_shape=jax.ShapeDtypeStruct((s, n), BF16),
        scratch_shapes=[pltpu.VMEM((tm, k), BF16)],
        compiler_params=_params(2),
        name="norm_matmul_scale",
    )(h, norm_g[None, :], w, col_scale[None, :])


def _matmul_residual_kernel(a_ref, w_ref, r_ref, o_ref):
    o_ref[...] = r_ref[...] + _dot(a_ref[...], w_ref[...].astype(BF16))


def _matmul_residual(a, w, layer, res, tm, tn):
    s, k = a.shape
    n = w.shape[2]
    return pl.pallas_call(
        _matmul_residual_kernel,
        grid=(s // tm, n // tn),
        in_specs=[pl.BlockSpec((tm, k), lambda i, j: (i, 0)),
                  pl.BlockSpec((None, k, tn), lambda i, j: (layer, 0, j)),
                  pl.BlockSpec((tm, tn), lambda i, j: (i, j))],
        out_specs=pl.BlockSpec((tm, tn), lambda i, j: (i, j)),
        out_shape=jax.ShapeDtypeStruct((s, n), F32),
        compiler_params=_params(2),
        name="matmul_residual",
    )(a, w, res)


def _matmul_residual_norm_kernel(a_ref, w_ref, r_ref, g_ref, o_ref):
    kk = pl.program_id(1)

    @pl.when(kk == 0)
    def _():
        o_ref[...] = r_ref[...]

    o_ref[...] += _dot(a_ref[...], w_ref[...].astype(BF16))

    @pl.when(kk == pl.num_programs(1) - 1)
    def _():
        o_ref[...] = _rms(o_ref[...], g_ref[...])


def _matmul_residual_norm(a, w, layer, res, norm_g, tm=512, tk=2816):
    s, k = a.shape
    n = w.shape[2]
    return pl.pallas_call(
        _matmul_residual_norm_kernel,
        grid=(s // tm, k // tk),
        in_specs=[pl.BlockSpec((tm, tk), lambda i, kk: (i, kk)),
                  pl.BlockSpec((None, tk, n), lambda i, kk: (layer, kk, 0)),
                  pl.BlockSpec((tm, n), lambda i, kk: (i, 0)),
                  pl.BlockSpec((1, n), lambda i, kk: (0, 0))],
        out_specs=pl.BlockSpec((tm, n), lambda i, kk: (i, 0)),
        out_shape=jax.ShapeDtypeStruct((s, n), F32),
        compiler_params=_params(2),
        name="matmul_residual_norm",
    )(a, w, res, norm_g[None, :])


def _ffn_up_kernel(h_ref, g_ref, wg_ref, wu_ref, o_ref, a_sc):
    _normalize_rows_once(h_ref, g_ref, a_sc)
    a = a_sc[...]
    g = _dot(a, wg_ref[...].astype(BF16))
    u = _dot(a, wu_ref[...].astype(BF16))
    o_ref[...] = (g * jax.nn.sigmoid(g) * u).astype(o_ref.dtype)


def _ffn_up(h, norm_g, w_gate_up, layer, tm=1024, tn=512):
    s, k = h.shape
    d_ff = w_gate_up.shape[2] // 2
    nj = d_ff // tn
    return pl.pallas_call(
        _ffn_up_kernel,
        grid=(s // tm, nj),
        in_specs=[pl.BlockSpec((tm, k), lambda i, j: (i, 0)), pl.BlockSpec((1, k), lambda i, j: (0, 0)),
                  pl.BlockSpec((None, k, tn), lambda i, j: (layer, 0, j)),
                  pl.BlockSpec((None, k, tn), lambda i, j: (layer, 0, nj + j))],
        out_specs=pl.BlockSpec((tm, tn), lambda i, j: (i, j)),
        out_shape=jax.ShapeDtypeStruct((s, d_ff), BF16),
        scratch_shapes=[pltpu.VMEM((tm, k), BF16)],
        compiler_params=_params(2),
        name="ffn_up",
    )(h, norm_g[None, :], w_gate_up, w_gate_up)


def _ffn(h, norm_g, w_gate_up, w_down, layer, out_norm_g=None):
    act = _ffn_up(h, norm_g, w_gate_up, layer)
    if out_norm_g is not None:
        return _matmul_residual_norm(act, w_down, layer, h, out_norm_g)
    return _matmul_residual(act, w_down, layer, h, tm=1024, tn=512)


def kernel(x, positions, attn_norm, mla_w_in, mla_q_norm, mla_kv_norm, mla_w_uq, mla_w_ukv, mla_w_o,
           sb_w_in, sb_w_o, ffn_norm, ffn_w_gate_up, ffn_w_down, final_norm):
    b, s, d = x.shape
    assert b == 1 and d == D_MODEL
    h = x.reshape(s, d)

    attn_tile = 512
    tabs = _rope_tables(positions.reshape(s, 1))
    tabs_t = _rope_tables_t(positions.reshape(1, s))
    w_in_pad = jnp.pad(mla_w_in[0], ((0, 0), (0, LANES - QK_ROPE_DIM))).astype(BF16)
    qk_dim = QK_NOPE_DIM + QK_ROPE_DIM
    w_uq_pad = jnp.pad(mla_w_uq[0].reshape(Q_LORA_RANK, MLA_HEADS, qk_dim),
                       ((0, 0), (0, 0), (0, QK_PAD_DIM - qk_dim)))
    w_uq_pad_t = w_uq_pad.reshape(Q_LORA_RANK, MLA_HEADS * QK_PAD_DIM).T.astype(BF16)
    w_ukv = mla_w_ukv[0].reshape(KV_LORA_RANK, MLA_HEADS, QK_NOPE_DIM + V_HEAD_DIM)
    w_uk = w_ukv[:, :, :QK_NOPE_DIM].reshape(KV_LORA_RANK, MLA_HEADS * QK_NOPE_DIM).astype(BF16)
    w_uv_t = w_ukv[:, :, QK_NOPE_DIM:].reshape(KV_LORA_RANK, MLA_HEADS * V_HEAD_DIM).T.astype(BF16)

    cq, ckv, kr = _mla_in(h, attn_norm[0], w_in_pad, mla_q_norm[0], mla_kv_norm[0], tabs)
    q_t = _mla_q(cq, w_uq_pad_t, tabs_t, qk_dim ** -0.5 * LOG2_E, attn_tile)
    k, v_t = _mla_kv(ckv, w_uk, w_uv_t, kr, attn_tile)
    to_cast = (mla_w_o, sb_w_in, sb_w_o, ffn_w_gate_up, ffn_w_down)
    o, cast = _mla_attn(q_t, k, v_t, attn_tile, cast_weights=[w.reshape(-1, w.shape[2]) for w in to_cast])
    w_o, w_sb_in, w_sb_o, w_gate_up, w_down = [c.reshape(w.shape) for c, w in zip(cast, to_cast)]
    h = _matmul_residual(o, w_o, 0, h, tm=1024, tn=1024)
    h = _ffn(h, ffn_norm[0], w_gate_up, w_down, 0)

    n_q = SB_HEADS * SB_HEAD_DIM
    col_scale = jnp.concatenate([jnp.full((n_q,), SB_HEAD_DIM ** -0.5 * LOG2_E, F32), jnp.ones((2 * n_q,), F32)])
    qkv = _norm_matmul_scale(h, attn_norm[1], w_sb_in, 0, col_scale)
    o = _sb_attn(qkv)
    h = _matmul_residual(o, w_sb_o, 0, h, tm=1024, tn=1024)
    return _ffn(h, ffn_norm[1], w_gate_up, w_down, 1, out_norm_g=final_norm).reshape(b, s, d)
```

```python
import functools

import jax
import jax.numpy as jnp
from jax import lax
from jax.experimental import pallas as pl
from jax.experimental.pallas import tpu as pltpu

D_MODEL = 2048
MLA_HEADS = 16
Q_LORA_RANK = 512
KV_LORA_RANK = 512
QK_NOPE_DIM = 128
QK_ROPE_DIM = 64
V_HEAD_DIM = 128
ROPE_THETA = 10000.0
SB_HEADS = 16
SB_HEAD_DIM = 128
RMS_EPS = 1e-6

LANES = 128
QK_PAD_DIM = 2 * LANES
VMEM_LIMIT_BYTES = 56 * 1024 * 1024

LOG2_E = 1.4426950408889634
SB_LOG2_WEIGHT_FLOOR = -120.0 * LOG2_E

F32 = jnp.float32
BF16 = jnp.bfloat16


def _params(n_axes):
    return pltpu.CompilerParams(
        dimension_semantics=("arbitrary",) * n_axes, vmem_limit_bytes=VMEM_LIMIT_BYTES)


def _dot(a, b):
    return jnp.dot(a, b, preferred_element_type=F32)


def _dot_nt(a, b):
    return lax.dot_general(a, b, (((1,), (1,)), ((), ())), preferred_element_type=F32)


def _rms(x, g):
    return x * lax.rsqrt(jnp.mean(x * x, axis=-1, keepdims=True) + RMS_EPS) * g


def _rope(t, c, s1, s2):
    half = QK_ROPE_DIM // 2
    return t * c + pltpu.roll(t, half, 1) * s1 + pltpu.roll(t, LANES - half, 1) * s2


def _rope_tables_kernel(pos_ref, freq_ref, c_ref, s1_ref, s2_ref):
    half = QK_ROPE_DIM // 2
    ang = pos_ref[...].astype(F32) * freq_ref[...]
    lane = lax.broadcasted_iota(jnp.int32, ang.shape, 1)
    cos, sin = jnp.cos(ang), jnp.sin(ang)
    c_ref[...] = jnp.where(lane < 2 * half, cos, 0.0)
    s1_ref[...] = jnp.where((lane >= half) & (lane < 2 * half), sin, 0.0)
    s2_ref[...] = jnp.where(lane < half, -sin, 0.0)


def _rope_tables(positions, tm=2048):
    s = positions.shape[0]
    inv_freq = ROPE_THETA ** (-jnp.arange(0, QK_ROPE_DIM, 2, dtype=F32) / QK_ROPE_DIM)
    freq = jnp.concatenate([inv_freq, inv_freq, jnp.zeros((LANES - QK_ROPE_DIM,), F32)])[None, :]
    out = jax.ShapeDtypeStruct((s, LANES), F32)
    row = pl.BlockSpec((tm, LANES), lambda i: (i, 0))
    return pl.pallas_call(
        _rope_tables_kernel,
        grid=(s // tm,),
        in_specs=[pl.BlockSpec((tm, 1), lambda i: (i, 0)), pl.BlockSpec((1, LANES), lambda i: (0, 0))],
        out_specs=[row, row, row],
        out_shape=[out, out, out],
        compiler_params=_params(1),
        name="rope_tables",
    )(positions, freq)


def _rope_tables_t_kernel(pos_ref, freq_ref, c_ref, s1_ref, s2_ref):
    half = QK_ROPE_DIM // 2
    ang = freq_ref[...] * pos_ref[...].astype(F32)
    r = lax.broadcasted_iota(jnp.int32, ang.shape, 0)
    cos, sin = jnp.cos(ang), jnp.sin(ang)
    c_ref[...] = jnp.where(r < 2 * half, cos, 0.0)
    s1_ref[...] = jnp.where((r >= half) & (r < 2 * half), sin, 0.0)
    s2_ref[...] = jnp.where(r < half, -sin, 0.0)


def _rope_tables_t(positions, tm=2048):
    s = positions.shape[1]
    inv_freq = ROPE_THETA ** (-jnp.arange(0, QK_ROPE_DIM, 2, dtype=F32) / QK_ROPE_DIM)
    freq = jnp.concatenate([inv_freq, inv_freq, jnp.zeros((LANES - QK_ROPE_DIM,), F32)])[:, None]
    out = jax.ShapeDtypeStruct((LANES, s), F32)
    col = pl.BlockSpec((LANES, tm), lambda i: (0, i))
    return pl.pallas_call(
        _rope_tables_t_kernel,
        grid=(s // tm,),
        in_specs=[pl.BlockSpec((1, tm), lambda i: (0, i)), pl.BlockSpec((LANES, 1), lambda i: (0, 0))],
        out_specs=[col, col, col],
        out_shape=[out, out, out],
        compiler_params=_params(1),
        name="rope_tables_t",
    )(positions, freq)


def _mla_in_kernel(h_ref, g_ref, w_ref, qn_ref, kvn_ref, c_ref, s1_ref, s2_ref, cq_ref, ckv_ref, kr_ref):
    a = _rms(h_ref[...], g_ref[...]).astype(BF16)
    proj = _dot(a, w_ref[...])
    cq_ref[...] = _rms(proj[:, :Q_LORA_RANK], qn_ref[...]).astype(BF16)
    ckv_ref[...] = _rms(proj[:, Q_LORA_RANK:Q_LORA_RANK + KV_LORA_RANK], kvn_ref[...]).astype(BF16)
    kr = proj[:, Q_LORA_RANK + KV_LORA_RANK:]
    kr_ref[...] = _rope(kr, c_ref[...], s1_ref[...], s2_ref[...]).astype(BF16)


def _mla_in(h, norm_g, w_pad, q_norm, kv_norm, tabs, tm=1024):
    s, d = h.shape
    n = w_pad.shape[1]
    row = lambda w: pl.BlockSpec((tm, w), lambda i: (i, 0))
    const = lambda r, w: pl.BlockSpec((r, w), lambda i: (0, 0))
    return pl.pallas_call(
        _mla_in_kernel,
        grid=(s // tm,),
        in_specs=[row(d), const(1, d), const(d, n), const(1, Q_LORA_RANK), const(1, KV_LORA_RANK),
                  row(LANES), row(LANES), row(LANES)],
        out_specs=[row(Q_LORA_RANK), row(KV_LORA_RANK), row(LANES)],
        out_shape=[jax.ShapeDtypeStruct((s, Q_LORA_RANK), BF16),
                   jax.ShapeDtypeStruct((s, KV_LORA_RANK), BF16),
                   jax.ShapeDtypeStruct((s, LANES), BF16)],
        compiler_params=_params(1),
        name="mla_in",
    )(h, norm_g[None, :], w_pad, q_norm[None, :], kv_norm[None, :], *tabs)


def _mla_q_kernel(w_ref, cq_ref, c_ref, s1_ref, s2_ref, q_ref, *, heads, scale):
    acc = _dot_nt(w_ref[...], cq_ref[...])
    c, s1, s2 = c_ref[...], s1_ref[...], s2_ref[...]
    half = QK_ROPE_DIM // 2
    tq = q_ref.shape[3]
    for h in range(heads):
        lo = h * QK_PAD_DIM
        nope = (acc[lo:lo + LANES, :] * scale).astype(BF16)
        t = acc[lo + LANES:lo + QK_PAD_DIM, :]
        down = jnp.concatenate([t[LANES - half:], t[:LANES - half]], axis=0)
        up = jnp.concatenate([t[half:], t[:half]], axis=0)
        rope = ((t * c + down * s1 + up * s2) * scale).astype(BF16)
        for tt in range(acc.shape[1] // tq):
            q_ref[h, tt, :LANES, :] = nope[:, tt * tq:(tt + 1) * tq]
            q_ref[h, tt, LANES:, :] = rope[:, tt * tq:(tt + 1) * tq]


def _mla_q(cq, w_uq_pad_t, tabs_t, scale, t, tm=1024, heads_per_tile=8):
    s, r = cq.shape
    n = w_uq_pad_t.shape[0]
    tn = heads_per_tile * QK_PAD_DIM
    col = pl.BlockSpec((LANES, tm), lambda j, i: (0, i))
    return pl.pallas_call(
        functools.partial(_mla_q_kernel, heads=heads_per_tile, scale=scale),
        grid=(n // tn, s // tm),
        in_specs=[pl.BlockSpec((tn, r), lambda j, i: (j, 0)), pl.BlockSpec((tm, r), lambda j, i: (i, 0)),
                  col, col, col],
        out_specs=pl.BlockSpec((heads_per_tile, tm // t, QK_PAD_DIM, t), lambda j, i: (j, i, 0, 0)),
        out_shape=jax.ShapeDtypeStruct((n // QK_PAD_DIM, s // t, QK_PAD_DIM, t), BF16),
        compiler_params=_params(2),
        name="mla_q",
    )(w_uq_pad_t, cq, *tabs_t)


V_ROWS = V_HEAD_DIM + 16


def _mla_kv_kernel(ckv_ref, wk_ref, wv_ref, kr_ref, k_ref, v_ref, *, heads, t):
    ckv = ckv_ref[...]
    kn = _dot(ckv, wk_ref[...])
    vt = _dot_nt(wv_ref[...], ckv)
    kr = kr_ref[...]
    ones = jnp.ones((V_ROWS - V_HEAD_DIM, t), BF16)
    for h in range(heads):
        k_ref[:, h * QK_PAD_DIM:h * QK_PAD_DIM + LANES] = kn[:, h * LANES:(h + 1) * LANES].astype(BF16)
        k_ref[:, h * QK_PAD_DIM + LANES:(h + 1) * QK_PAD_DIM] = kr
        for tt in range(ckv.shape[0] // t):
            v_ref[h, tt, :V_HEAD_DIM, :] = vt[h * V_HEAD_DIM:(h + 1) * V_HEAD_DIM, tt * t:(tt + 1) * t].astype(BF16)
            v_ref[h, tt, V_HEAD_DIM:, :] = ones


def _mla_kv(ckv, w_uk, w_uv_t, kr, t, tm=1024, heads_per_tile=8):
    s, r = ckv.shape
    n = w_uk.shape[1]
    heads = n // LANES
    tn = heads_per_tile * LANES
    return pl.pallas_call(
        functools.partial(_mla_kv_kernel, heads=heads_per_tile, t=t),
        grid=(s // tm, n // tn),
        in_specs=[pl.BlockSpec((tm, r), lambda i, j: (i, 0)),
                  pl.BlockSpec((r, tn), lambda i, j: (0, j)),
                  pl.BlockSpec((tn, r), lambda i, j: (j, 0)),
                  pl.BlockSpec((tm, LANES), lambda i, j: (i, 0))],
        out_specs=[pl.BlockSpec((tm, 2 * tn), lambda i, j: (i, j)),
                   pl.BlockSpec((heads_per_tile, tm // t, V_ROWS, t), lambda i, j: (j, i, 0, 0))],
        out_shape=[jax.ShapeDtypeStruct((s, 2 * n), BF16),
                   jax.ShapeDtypeStruct((heads, s // t, V_ROWS, t), BF16)],
        compiler_params=_params(2),
        name="mla_kv",
    )(ckv, w_uk, w_uv_t, kr)


def _mla_attn_kernel(q_ref, k_ref, v_ref, *rest, t, unrolls, n_cast):
    w_refs, o_ref, w_bf16_refs = rest[:n_cast], rest[n_cast], rest[n_cast + 1:2 * n_cast + 1]
    m_sc, acc_sc, s_even, s_odd = rest[2 * n_cast + 1:]
    for w_ref, w_bf16_ref in zip(w_refs, w_bf16_refs):
        w_bf16_ref[...] = w_ref[...].astype(BF16)
    n_q = q_ref.shape[0]

    def first_scores(q):
        return _dot(k_ref[:t, :], q)

    def q_tile(qi, carry):
        _mla_attn_q_tile(pl.program_id(1) * n_q + qi, q_ref[qi], k_ref, v_ref,
                         o_ref.at[pl.ds(pl.multiple_of(qi * t, t), t), :],
                         m_sc, acc_sc, s_even, s_odd, t=t, unrolls=unrolls,
                         start_next=lambda: first_scores(q_ref[jnp.minimum(qi + 1, n_q - 1)]))
        return carry

    s_even[...] = first_scores(q_ref[0])
    lax.fori_loop(0, n_q, q_tile, 0)


def _mla_attn_q_tile(i, q, k_ref, v_ref, o_ref, m_sc, acc_sc, s_even, s_odd, *, t, unrolls, start_next):
    m_sc[...] = jnp.full(m_sc.shape, -jnp.inf, F32)
    acc_sc[...] = jnp.zeros(acc_sc.shape, F32)

    def scores(j):
        return _dot(k_ref[pl.ds(pl.multiple_of(j * t, t), t), :], q)

    def softmax_pv(s_ref, j, diagonal):
        def load():
            s = s_ref[...]
            if diagonal:
                key = lax.broadcasted_iota(jnp.int32, (t, t), 0)
                query = lax.broadcasted_iota(jnp.int32, (t, t), 1)
                s = jnp.where(key <= query, s, -jnp.inf)
            return s
        m_prev = m_sc[...]
        m_next = jnp.maximum(m_prev, jnp.max(load(), axis=0, keepdims=True))
        m_sc[...] = m_next
        p = jnp.exp2(load() - m_sc[...]).astype(BF16)
        alpha = jnp.exp2(m_prev - m_next)
        acc_sc[...] = alpha * acc_sc[...] + _dot(v_ref[j], p)

    def pairs(first, n_pairs):
        for d in range(n_pairs):
            j = first + 2 * d
            s_odd[...] = scores(j + 1)
            softmax_pv(s_even, j, False)
            s_even[...] = scores(j + 2)
            softmax_pv(s_odd, j + 1, False)

    done = 0
    for u in unrolls:
        n_groups = (i // 2 - done) // u

        def group(g, carry, u=u, done=done):
            pairs(2 * (done + g * u), u)
            return carry

        lax.fori_loop(0, n_groups, group, 0)
        done = done + n_groups * u

    @pl.when(i % 2 == 0)
    def _():
        softmax_pv(s_even, i, True)

    @pl.when(i % 2 == 1)
    def _():
        s_odd[...] = scores(i)
        softmax_pv(s_even, i - 1, False)
        softmax_pv(s_odd, i, True)

    s_even[...] = start_next()
    out_t = acc_sc[:V_HEAD_DIM, :] / acc_sc[V_HEAD_DIM:V_HEAD_DIM + 1, :]
    o_ref[...] = out_t.T.astype(o_ref.dtype)


BF16_SUBLANES = 16


def _mla_attn(q_t, k, v_t, t, cast_weights=(), q_tiles_per_step=4, unrolls=(8, 4, 2, 1)):
    s = k.shape[0]
    h = MLA_HEADS
    g = q_tiles_per_step
    n_i = s // (t * g)
    n_steps = h * n_i
    assert unrolls[-1] == 1
    cast_specs = []
    for w in cast_weights:
        rows, cols = w.shape
        share = next(sh for sh in (1, 2, 4, 8) if rows % (n_steps // sh * BF16_SUBLANES) == 0)
        cast_specs.append(pl.BlockSpec((rows // (n_steps // share), cols),
                                       lambda hh, i, share=share: ((hh * n_i + i) // share, 0)))
    outs = pl.pallas_call(
        functools.partial(_mla_attn_kernel, t=t, unrolls=unrolls, n_cast=len(cast_weights)),
        grid=(h, n_i),
        in_specs=[pl.BlockSpec((None, g, QK_PAD_DIM, t), lambda hh, i: (hh, i, 0, 0)),
                  pl.BlockSpec((s, QK_PAD_DIM), lambda hh, i: (0, hh)),
                  pl.BlockSpec((None, s // t, V_ROWS, t), lambda hh, i: (hh, 0, 0, 0))] + cast_specs,
        out_specs=[pl.BlockSpec((g * t, V_HEAD_DIM), lambda hh, i: (i, hh))] + cast_specs,
        out_shape=[jax.ShapeDtypeStruct((s, h * V_HEAD_DIM), BF16)]
        + [jax.ShapeDtypeStruct(w.shape, BF16) for w in cast_weights],
        scratch_shapes=[pltpu.VMEM((1, t), F32),
                        pltpu.VMEM((V_ROWS, t), F32),
                        pltpu.VMEM((t, t), F32), pltpu.VMEM((t, t), F32)],
        compiler_params=_params(2),
        name="mla_attn",
    )(q_t, k, v_t, *cast_weights)
    return outs[0], outs[1:]


def _sb_attn_kernel(q_ref, k_ref, v_ref, o_ref, tri_sc, *bufs, t, group):
    n_tiles = q_ref.shape[0] // t
    n_groups = n_tiles // group
    w = 2 * t
    kinds = 6
    per_kind = 2 * group
    lom_sc, lb_sc, a_sc, wsum_sc, carry_sc, acc_sc = [
        [bufs[kind * per_kind + slot * group:kind * per_kind + (slot + 1) * group] for slot in range(2)]
        for kind in range(kinds)]
    tri_sc[...] = jnp.where(lax.broadcasted_iota(jnp.int32, (w, w), 0)
                            > lax.broadcasted_iota(jnp.int32, (w, w), 1), -1.0, 0.0).astype(BF16)

    def log_probs(q, start, width):
        z = _dot_nt(q, k_ref[pl.ds(start, width), :])
        neg_log_om = jnp.maximum(z, 0.0) + jnp.log2(1.0 + jnp.exp2(-jnp.abs(z)))
        return neg_log_om, z - neg_log_om

    def window_start(i):
        return pl.multiple_of(jnp.maximum(i - 1, 0) * t, t)

    def window_logits(slot, n, i, first_tile=False):
        i = jnp.minimum(i, n_tiles - 1)
        q = q_ref[pl.ds(pl.multiple_of(i * t, t), t), :]
        neg_log_om, log_beta = log_probs(q, window_start(i), w)
        lom, lb, wsum = lom_sc[slot][n], lb_sc[slot][n], wsum_sc[slot][n]
        strict = (lax.broadcasted_iota(jnp.int32, (t, t), 1)
                  < lax.broadcasted_iota(jnp.int32, (t, t), 0))
        if first_tile:
            halves = ((strict, neg_log_om[:, :t], log_beta[:, :t]), (None, None, None))
        else:
            halves = ((True, neg_log_om[:, :t], log_beta[:, :t]), (strict, neg_log_om[:, t:], log_beta[:, t:]))
        total = jnp.zeros((t, 1), F32)
        for half, (mask, nlo, lbeta) in enumerate(halves):
            cols = slice(half * t, (half + 1) * t)
            if mask is None:
                lom[:, cols] = jnp.zeros((t, t), BF16)
                lb[:, cols] = jnp.full((t, t), -jnp.inf, F32)
                continue
            if mask is not True:
                nlo = jnp.where(mask, nlo, 0.0)
                lbeta = jnp.where(mask, lbeta, -jnp.inf)
            lom[:, cols] = nlo.astype(BF16)
            lb[:, cols] = lbeta
            total = total + jnp.sum(nlo, axis=1, keepdims=True)
        wsum[...] = jnp.broadcast_to(-total, (t, LANES))

    def window_weights(slot, n):
        lom, lb, a = lom_sc[slot][n], lb_sc[slot][n], a_sc[slot][n]
        a[:, :t] = jnp.exp2(lb[:, :t] + _dot(lom[...], tri_sc[:, :t])).astype(BF16)
        a[:, t:] = jnp.exp2(lb[:, t:] + _dot(lom[:, t:], tri_sc[t:, t:])).astype(BF16)
        carry_sc[slot][n][...] = wsum_sc[slot][n][...]

    def window_values(slot, n, i):
        acc_sc[slot][n][...] = _dot(a_sc[slot][n][...], v_ref[pl.ds(window_start(i), w), :])

    def earlier_tiles(slot, n, i):
        carry, acc = carry_sc[slot][n], acc_sc[slot][n]
        rows = pl.ds(pl.multiple_of(i * t, t), t)
        q = q_ref[rows, :]

        def more(state):
            j, carry_max = state
            return (j >= 0) & (carry_max > SB_LOG2_WEIGHT_FLOOR)

        def body(state):
            j, _ = state
            start = pl.multiple_of(j * t, t)
            neg_log_om, log_beta = log_probs(q, start, t)
            after = _dot(neg_log_om.astype(BF16), tri_sc[:t, :t]) + jnp.tile(carry[...], (1, t // LANES))
            a = jnp.exp2(log_beta + after)
            acc[...] += _dot(a.astype(BF16), v_ref[pl.ds(start, t), :])
            carry[...] -= jnp.sum(neg_log_om, axis=1, keepdims=True)
            return j - 1, jnp.max(carry[...])

        lax.while_loop(more, body, (i - 2, jnp.max(carry[...])))
        o_ref[rows, :] = acc[...].astype(o_ref.dtype)

    def step(g, slot):
        for n in range(group):
            window_logits(1 - slot, n, (g + 1) * group + n)
        for n in range(group):
            window_weights(slot, n)
        for n in range(group):
            window_values(1 - slot, n, (g - 1) * group + n)
        for n in range(group):
            earlier_tiles(1 - slot, n, (g - 1) * group + n)

    def step_pair(p, _):
        step(2 * p + 1, 1)
        step(2 * p + 2, 0)
        return 0

    for n in range(group):
        window_logits(0, n, n, first_tile=(n == 0))
    for n in range(group):
        window_weights(0, n)
    for n in range(group):
        window_logits(1, n, group + n)
    lax.fori_loop(0, n_groups // 2, step_pair, 0)


def _sb_attn(qkv, t=256, group=1):
    s = qkv.shape[0]
    h, d = SB_HEADS, SB_HEAD_DIM
    assert s % (2 * group * t) == 0
    per_kind = 2 * group
    return pl.pallas_call(
        functools.partial(_sb_attn_kernel, t=t, group=group),
        grid=(h,),
        in_specs=[pl.BlockSpec((s, d), lambda hh: (0, hh)),
                  pl.BlockSpec((s, d), lambda hh: (0, h + hh)),
                  pl.BlockSpec((s, d), lambda hh: (0, 2 * h + hh))],
        out_specs=pl.BlockSpec((s, d), lambda hh: (0, hh)),
        out_shape=jax.ShapeDtypeStruct((s, h * d), BF16),
        scratch_shapes=[pltpu.VMEM((2 * t, 2 * t), BF16)]
        + per_kind * [pltpu.VMEM((t, 2 * t), BF16)]
        + per_kind * [pltpu.VMEM((t, 2 * t), F32)]
        + per_kind * [pltpu.VMEM((t, 2 * t), BF16)]
        + per_kind * [pltpu.VMEM((t, LANES), F32)]
        + per_kind * [pltpu.VMEM((t, LANES), F32)]
        + per_kind * [pltpu.VMEM((t, d), F32)],
        compiler_params=_params(1),
        name="sb_attn",
    )(qkv, qkv, qkv)


def _normalize_rows_once(h_ref, g_ref, a_sc):
    @pl.when(pl.program_id(1) == 0)
    def _():
        a_sc[...] = _rms(h_ref[...], g_ref[...]).astype(BF16)


def _norm_matmul_scale_kernel(h_ref, g_ref, w_ref, cs_ref, o_ref, a_sc):
    _normalize_rows_once(h_ref, g_ref, a_sc)
    o_ref[...] = (_dot(a_sc[...], w_ref[...].astype(BF16)) * cs_ref[...]).astype(o_ref.dtype)


def _norm_matmul_scale(h, norm_g, w, layer, col_scale, tm=1024, tn=1536):
    s, k = h.shape
    n = w.shape[2]
    return pl.pallas_call(
        _norm_matmul_scale_kernel,
        grid=(s // tm, n // tn),
        in_specs=[pl.BlockSpec((tm, k), lambda i, j: (i, 0)), pl.BlockSpec((1, k), lambda i, j: (0, 0)),
                  pl.BlockSpec((None, k, tn), lambda i, j: (layer, 0, j)),
                  pl.BlockSpec((1, tn), lambda i, j: (0, j))],
        out_specs=pl.BlockSpec((tm, tn), lambda i, j: (i, j)),
        out_shape=jax.ShapeDtypeStruct((s, n), BF16),
        scratch_shapes=[pltpu.VMEM((tm, k), BF16)],
        compiler_params=_params(2),
        name="norm_matmul_scale",
    )(h, norm_g[None, :], w, col_scale[None, :])


def _matmul_residual_kernel(a_ref, w_ref, r_ref, o_ref):
    o_ref[...] = r_ref[...] + _dot(a_ref[...], w_ref[...].astype(BF16))


def _matmul_residual(a, w, layer, res, tm, tn):
    s, k = a.shape
    n = w.shape[2]
    return pl.pallas_call(
        _matmul_residual_kernel,
        grid=(s // tm, n // tn),
        in_specs=[pl.BlockSpec((tm, k), lambda i, j: (i, 0)),
                  pl.BlockSpec((None, k, tn), lambda i, j: (layer, 0, j)),
                  pl.BlockSpec((tm, tn), lambda i, j: (i, j))],
        out_specs=pl.BlockSpec((tm, tn), lambda i, j: (i, j)),
        out_shape=jax.ShapeDtypeStruct((s, n), F32),
        compiler_params=_params(2),
        name="matmul_residual",
    )(a, w, res)


def _matmul_residual_norm_kernel(a_ref, w_ref, r_ref, g_ref, o_ref):
    kk = pl.program_id(1)

    @pl.when(kk == 0)
    def _():
        o_ref[...] = r_ref[...]

    o_ref[...] += _dot(a_ref[...], w_ref[...].astype(BF16))

    @pl.when(kk == pl.num_programs(1) - 1)
    def _():
        o_ref[...] = _rms(o_ref[...], g_ref[...])


def _matmul_residual_norm(a, w, layer, res, norm_g, tm=512, tk=2816):
    s, k = a.shape
    n = w.shape[2]
    return pl.pallas_call(
        _matmul_residual_norm_kernel,
        grid=(s // tm, k // tk),
        in_specs=[pl.BlockSpec((tm, tk), lambda i, kk: (i, kk)),
                  pl.BlockSpec((None, tk, n), lambda i, kk: (layer, kk, 0)),
                  pl.BlockSpec((tm, n), lambda i, kk: (i, 0)),
                  pl.BlockSpec((1, n), lambda i, kk: (0, 0))],
        out_specs=pl.BlockSpec((tm, n), lambda i, kk: (i, 0)),
        out_shape=jax.ShapeDtypeStruct((s, n), F32),
        compiler_params=_params(2),
        name="matmul_residual_norm",
    )(a, w, res, norm_g[None, :])


def _ffn_up_kernel(h_ref, g_ref, wg_ref, wu_ref, o_ref, a_sc):
    _normalize_rows_once(h_ref, g_ref, a_sc)
    a = a_sc[...]
    g = _dot(a, wg_ref[...].astype(BF16))
    u = _dot(a, wu_ref[...].astype(BF16))
    o_ref[...] = (g * jax.nn.sigmoid(g) * u).astype(o_ref.dtype)


def _ffn_up(h, norm_g, w_gate_up, layer, tm=1024, tn=512):
    s, k = h.shape
    d_ff = w_gate_up.shape[2] // 2
    nj = d_ff // tn
    return pl.pallas_call(
        _ffn_up_kernel,
        grid=(s // tm, nj),
        in_specs=[pl.BlockSpec((tm, k), lambda i, j: (i, 0)), pl.BlockSpec((1, k), lambda i, j: (0, 0)),
                  pl.BlockSpec((None, k, tn), lambda i, j: (layer, 0, j)),
                  pl.BlockSpec((None, k, tn), lambda i, j: (layer, 0, nj + j))],
        out_specs=pl.BlockSpec((tm, tn), lambda i, j: (i, j)),
        out_shape=jax.ShapeDtypeStruct((s, d_ff), BF16),
        scratch_shapes=[pltpu.VMEM((tm, k), BF16)],
        compiler_params=_params(2),
        name="ffn_up",
    )(h, norm_g[None, :], w_gate_up, w_gate_up)


def _ffn(h, norm_g, w_gate_up, w_down, layer, out_norm_g=None):
    act = _ffn_up(h, norm_g, w_gate_up, layer)
    if out_norm_g is not None:
        return _matmul_residual_norm(act, w_down, layer, h, out_norm_g)
    return _matmul_residual(act, w_down, layer, h, tm=1024, tn=512)


def kernel(x, positions, attn_norm, mla_w_in, mla_q_norm, mla_kv_norm, mla_w_uq, mla_w_ukv, mla_w_o,
           sb_w_in, sb_w_o, ffn_norm, ffn_w_gate_up, ffn_w_down, final_norm):
    b, s, d = x.shape
    assert b == 1 and d == D_MODEL
    h = x.reshape(s, d)

    attn_tile = 512
    tabs = _rope_tables(positions.reshape(s, 1))
    tabs_t = _rope_tables_t(positions.reshape(1, s))
    w_in_pad = jnp.pad(mla_w_in[0], ((0, 0), (0, LANES - QK_ROPE_DIM))).astype(BF16)
    qk_dim = QK_NOPE_DIM + QK_ROPE_DIM
    w_uq_pad = jnp.pad(mla_w_uq[0].reshape(Q_LORA_RANK, MLA_HEADS, qk_dim),
                       ((0, 0), (0, 0), (0, QK_PAD_DIM - qk_dim)))
    w_uq_pad_t = w_uq_pad.reshape(Q_LORA_RANK, MLA_HEADS * QK_PAD_DIM).T.astype(BF16)
    w_ukv = mla_w_ukv[0].reshape(KV_LORA_RANK, MLA_HEADS, QK_NOPE_DIM + V_HEAD_DIM)
    w_uk = w_ukv[:, :, :QK_NOPE_DIM].reshape(KV_LORA_RANK, MLA_HEADS * QK_NOPE_DIM).astype(BF16)
    w_uv_t = w_ukv[:, :, QK_NOPE_DIM:].reshape(KV_LORA_RANK, MLA_HEADS * V_HEAD_DIM).T.astype(BF16)

    cq, ckv, kr = _mla_in(h, attn_norm[0], w_in_pad, mla_q_norm[0], mla_kv_norm[0], tabs)
    q_t = _mla_q(cq, w_uq_pad_t, tabs_t, qk_dim ** -0.5 * LOG2_E, attn_tile)
    k, v_t = _mla_kv(ckv, w_uk, w_uv_t, kr, attn_tile)
    to_cast = (mla_w_o, sb_w_in, sb_w_o, ffn_w_gate_up, ffn_w_down)
    o, cast = _mla_attn(q_t, k, v_t, attn_tile, cast_weights=[w.reshape(-1, w.shape[2]) for w in to_cast])
    w_o, w_sb_in, w_sb_o, w_gate_up, w_down = [c.reshape(w.shape) for c, w in zip(cast, to_cast)]
    h = _matmul_residual(o, w_o, 0, h, tm=1024, tn=1024)
    h = _ffn(h, ffn_norm[0], w_gate_up, w_down, 0)

    n_q = SB_HEADS * SB_HEAD_DIM
    col_scale = jnp.concatenate([jnp.full((n_q,), SB_HEAD_DIM ** -0.5 * LOG2_E, F32), jnp.ones((2 * n_q,), F32)])
    qkv = _norm_matmul_scale(h, attn_norm[1], w_sb_in, 0, col_scale)
    o = _sb_attn(qkv)
    h = _matmul_residual(o, w_sb_o, 0, h, tm=1024, tn=1024)
    return _ffn(h, ffn_norm[1], w_gate_up, w_down, 1, out_norm_g=final_norm).reshape(b, s, d)
```

```python
import functools

import jax
import jax.numpy as jnp
from jax import lax
from jax.experimental import pallas as pl
from jax.experimental.pallas import tpu as pltpu

D_MODEL = 2048
MLA_HEADS = 16
Q_LORA_RANK = 512
KV_LORA_RANK = 512
QK_NOPE_DIM = 128
QK_ROPE_DIM = 64
V_HEAD_DIM = 128
ROPE_THETA = 10000.0
SB_HEADS = 16
SB_HEAD_DIM = 128
RMS_EPS = 1e-6

LANES = 128
QK_PAD_DIM = 2 * LANES
VMEM_LIMIT_BYTES = 56 * 1024 * 1024

LOG2_E = 1.4426950408889634
SB_LOG2_WEIGHT_FLOOR = -120.0 * LOG2_E

F32 = jnp.float32
BF16 = jnp.bfloat16


def _params(n_axes):
    return pltpu.CompilerParams(
        dimension_semantics=("arbitrary",) * n_axes, vmem_limit_bytes=VMEM_LIMIT_BYTES)


def _dot(a, b):
    return jnp.dot(a, b, preferred_element_type=F32)


def _dot_nt(a, b):
    return lax.dot_general(a, b, (((1,), (1,)), ((), ())), preferred_element_type=F32)


def _rms(x, g):
    return x * lax.rsqrt(jnp.mean(x * x, axis=-1, keepdims=True) + RMS_EPS) * g


def _rope(t, c, s1, s2):
    half = QK_ROPE_DIM // 2
    return t * c + pltpu.roll(t, half, 1) * s1 + pltpu.roll(t, LANES - half, 1) * s2


def _rope_tables_kernel(pos_ref, freq_ref, c_ref, s1_ref, s2_ref):
    half = QK_ROPE_DIM // 2
    ang = pos_ref[...].astype(F32) * freq_ref[...]
    lane = lax.broadcasted_iota(jnp.int32, ang.shape, 1)
    cos, sin = jnp.cos(ang), jnp.sin(ang)
    c_ref[...] = jnp.where(lane < 2 * half, cos, 0.0)
    s1_ref[...] = jnp.where((lane >= half) & (lane < 2 * half), sin, 0.0)
    s2_ref[...] = jnp.where(lane < half, -sin, 0.0)


def _rope_tables(positions, tm=2048):
    s = positions.shape[0]
    inv_freq = ROPE_THETA ** (-jnp.arange(0, QK_ROPE_DIM, 2, dtype=F32) / QK_ROPE_DIM)
    freq = jnp.concatenate([inv_freq, inv_freq, jnp.zeros((LANES - QK_ROPE_DIM,), F32)])[None, :]
    out = jax.ShapeDtypeStruct((s, LANES), F32)
    row = pl.BlockSpec((tm, LANES), lambda i: (i, 0))
    return pl.pallas_call(
        _rope_tables_kernel,
        grid=(s // tm,),
        in_specs=[pl.BlockSpec((tm, 1), lambda i: (i, 0)), pl.BlockSpec((1, LANES), lambda i: (0, 0))],
        out_specs=[row, row, row],
        out_shape=[out, out, out],
        compiler_params=_params(1),
        name="rope_tables",
    )(positions, freq)


def _rope_tables_t_kernel(pos_ref, freq_ref, c_ref, s1_ref, s2_ref):
    half = QK_ROPE_DIM // 2
    ang = freq_ref[...] * pos_ref[...].astype(F32)
    r = lax.broadcasted_iota(jnp.int32, ang.shape, 0)
    cos, sin = jnp.cos(ang), jnp.sin(ang)
    c_ref[...] = jnp.where(r < 2 * half, cos, 0.0)
    s1_ref[...] = jnp.where((r >= half) & (r < 2 * half), sin, 0.0)
    s2_ref[...] = jnp.where(r < half, -sin, 0.0)


def _rope_tables_t(positions, tm=2048):
    s = positions.shape[1]
    inv_freq = ROPE_THETA ** (-jnp.arange(0, QK_ROPE_DIM, 2, dtype=F32) / QK_ROPE_DIM)
    freq = jnp.concatenate([inv_freq, inv_freq, jnp.zeros((LANES - QK_ROPE_DIM,), F32)])[:, None]
    out = jax.ShapeDtypeStruct((LANES, s), F32)
    col = pl.BlockSpec((LANES, tm), lambda i: (0, i))
    return pl.pallas_call(
        _rope_tables_t_kernel,
        grid=(s // tm,),
        in_specs=[pl.BlockSpec((1, tm), lambda i: (0, i)), pl.BlockSpec((LANES, 1), lambda i: (0, 0))],
        out_specs=[col, col, col],
        out_shape=[out, out, out],
        compiler_params=_params(1),
        name="rope_tables_t",
    )(positions, freq)


def _mla_in_kernel(h_ref, g_ref, w_ref, qn_ref, kvn_ref, c_ref, s1_ref, s2_ref, cq_ref, ckv_ref, kr_ref):
    a = _rms(h_ref[...], g_ref[...]).astype(BF16)
    proj = _dot(a, w_ref[...])
    cq_ref[...] = _rms(proj[:, :Q_LORA_RANK], qn_ref[...]).astype(BF16)
    ckv_ref[...] = _rms(proj[:, Q_LORA_RANK:Q_LORA_RANK + KV_LORA_RANK], kvn_ref[...]).astype(BF16)
    kr = proj[:, Q_LORA_RANK + KV_LORA_RANK:]
    kr_ref[...] = _rope(kr, c_ref[...], s1_ref[...], s2_ref[...]).astype(BF16)


def _mla_in(h, norm_g, w_pad, q_norm, kv_norm, tabs, tm=1024):
    s, d = h.shape
    n = w_pad.shape[1]
    row = lambda w: pl.BlockSpec((tm, w), lambda i: (i, 0))
    const = lambda r, w: pl.BlockSpec((r, w), lambda i: (0, 0))
    return pl.pallas_call(
        _mla_in_kernel,
        grid=(s // tm,),
        in_specs=[row(d), const(1, d), const(d, n), const(1, Q_LORA_RANK), const(1, KV_LORA_RANK),
                  row(LANES), row(LANES), row(LANES)],
        out_specs=[row(Q_LORA_RANK), row(KV_LORA_RANK), row(LANES)],
        out_shape=[jax.ShapeDtypeStruct((s, Q_LORA_RANK), BF16),
                   jax.ShapeDtypeStruct((s, KV_LORA_RANK), BF16),
                   jax.ShapeDtypeStruct((s, LANES), BF16)],
        compiler_params=_params(1),
        name="mla_in",
    )(h, norm_g[None, :], w_pad, q_norm[None, :], kv_norm[None, :], *tabs)


def _mla_q_kernel(w_ref, cq_ref, c_ref, s1_ref, s2_ref, q_ref, *, heads, scale):
    acc = _dot_nt(w_ref[...], cq_ref[...])
    c, s1, s2 = c_ref[...], s1_ref[...], s2_ref[...]
    half = QK_ROPE_DIM // 2
    tq = q_ref.shape[3]
    for h in range(heads):
        lo = h * QK_PAD_DIM
        nope = (acc[lo:lo + LANES, :] * scale).astype(BF16)
        t = acc[lo + LANES:lo + QK_PAD_DIM, :]
        down = jnp.concatenate([t[LANES - half:], t[:LANES - half]], axis=0)
        up = jnp.concatenate([t[half:], t[:half]], axis=0)
        rope = ((t * c + down * s1 + up * s2) * scale).astype(BF16)
        for tt in range(acc.shape[1] // tq):
            q_ref[h, tt, :LANES, :] = nope[:, tt * tq:(tt + 1) * tq]
            q_ref[h, tt, LANES:, :] = rope[:, tt * tq:(tt + 1) * tq]


def _mla_q(cq, w_uq_pad_t, tabs_t, scale, t, tm=1024, heads_per_tile=8):
    s, r = cq.shape
    n = w_uq_pad_t.shape[0]
    tn = heads_per_tile * QK_PAD_DIM
    col = pl.BlockSpec((LANES, tm), lambda j, i: (0, i))
    return pl.pallas_call(
        functools.partial(_mla_q_kernel, heads=heads_per_tile, scale=scale),
        grid=(n // tn, s // tm),
        in_specs=[pl.BlockSpec((tn, r), lambda j, i: (j, 0)), pl.BlockSpec((tm, r), lambda j, i: (i, 0)),
                  col, col, col],
        out_specs=pl.BlockSpec((heads_per_tile, tm // t, QK_PAD_DIM, t), lambda j, i: (j, i, 0, 0)),
        out_shape=jax.ShapeDtypeStruct((n // QK_PAD_DIM, s // t, QK_PAD_DIM, t), BF16),
        compiler_params=_params(2),
        name="mla_q",
    )(w_uq_pad_t, cq, *tabs_t)


V_ROWS = V_HEAD_DIM + 16


def _mla_kv_kernel(ckv_ref, wk_ref, wv_ref, kr_ref, k_ref, v_ref, *, heads, t):
    ckv = ckv_ref[...]
    kn = _dot(ckv, wk_ref[...])
    vt = _dot_nt(wv_ref[...], ckv)
    kr = kr_ref[...]
    ones = jnp.ones((V_ROWS - V_HEAD_DIM, t), BF16)
    for h in range(heads):
        k_ref[:, h * QK_PAD_DIM:h * QK_PAD_DIM + LANES] = kn[:, h * LANES:(h + 1) * LANES].astype(BF16)
        k_ref[:, h * QK_PAD_DIM + LANES:(h + 1) * QK_PAD_DIM] = kr
        for tt in range(ckv.shape[0] // t):
            v_ref[h, tt, :V_HEAD_DIM, :] = vt[h * V_HEAD_DIM:(h + 1) * V_HEAD_DIM, tt * t:(tt + 1) * t].astype(BF16)
            v_ref[h, tt, V_HEAD_DIM:, :] = ones


def _mla_kv(ckv, w_uk, w_uv_t, kr, t, tm=1024, heads_per_tile=8):
    s, r = ckv.shape
    n = w_uk.shape[1]
    heads = n // LANES
    tn = heads_per_tile * LANES
    return pl.pallas_call(
        functools.partial(_mla_kv_kernel, heads=heads_per_tile, t=t),
        grid=(s // tm, n // tn),
        in_specs=[pl.BlockSpec((tm, r), lambda i, j: (i, 0)),
                  pl.BlockSpec((r, tn), lambda i, j: (0, j)),
                  pl.BlockSpec((tn, r), lambda i, j: (j, 0)),
                  pl.BlockSpec((tm, LANES), lambda i, j: (i, 0))],
        out_specs=[pl.BlockSpec((tm, 2 * tn), lambda i, j: (i, j)),
                   pl.BlockSpec((heads_per_tile, tm // t, V_ROWS, t), lambda i, j: (j, i, 0, 0))],
        out_shape=[jax.ShapeDtypeStruct((s, 2 * n), BF16),
                   jax.ShapeDtypeStruct((heads, s // t, V_ROWS, t), BF16)],
        compiler_params=_params(2),
        name="mla_kv",
    )(ckv, w_uk, w_uv_t, kr)


def _mla_attn_kernel(q_ref, k_ref, v_ref, *rest, t, unrolls, n_cast):
    w_refs, o_ref, w_bf16_refs = rest[:n_cast], rest[n_cast], rest[n_cast + 1:2 * n_cast + 1]
    m_sc, acc_sc, s_even, s_odd = rest[2 * n_cast + 1:]
    for w_ref, w_bf16_ref in zip(w_refs, w_bf16_refs):
        w_bf16_ref[...] = w_ref[...].astype(BF16)

    def q_tile(qi, carry):
        _mla_attn_q_tile(pl.program_id(1) * q_ref.shape[0] + qi, q_ref[qi], k_ref, v_ref,
                         o_ref.at[pl.ds(pl.multiple_of(qi * t, t), t), :],
                         m_sc, acc_sc, s_even, s_odd, t=t, unrolls=unrolls)
        return carry

    lax.fori_loop(0, q_ref.shape[0], q_tile, 0)


def _mla_attn_q_tile(i, q, k_ref, v_ref, o_ref, m_sc, acc_sc, s_even, s_odd, *, t, unrolls):
    m_sc[...] = jnp.full(m_sc.shape, -jnp.inf, F32)
    acc_sc[...] = jnp.zeros(acc_sc.shape, F32)

    def scores(j):
        return _dot(k_ref[pl.ds(pl.multiple_of(j * t, t), t), :], q)

    def softmax_pv(s_ref, j, diagonal):
        def load():
            s = s_ref[...]
            if diagonal:
                key = lax.broadcasted_iota(jnp.int32, (t, t), 0)
                query = lax.broadcasted_iota(jnp.int32, (t, t), 1)
                s = jnp.where(key <= query, s, -jnp.inf)
            return s
        m_prev = m_sc[...]
        m_next = jnp.maximum(m_prev, jnp.max(load(), axis=0, keepdims=True))
        m_sc[...] = m_next
        p = jnp.exp2(load() - m_sc[...]).astype(BF16)
        alpha = jnp.exp2(m_prev - m_next)
        acc_sc[...] = alpha * acc_sc[...] + _dot(v_ref[j], p)

    s_even[...] = scores(0)

    def pairs(first, n_pairs):
        for d in range(n_pairs):
            j = first + 2 * d
            s_odd[...] = scores(j + 1)
            softmax_pv(s_even, j, False)
            s_even[...] = scores(j + 2)
            softmax_pv(s_odd, j + 1, False)

    done = 0
    for u in unrolls:
        n_groups = (i // 2 - done) // u

        def group(g, carry, u=u, done=done):
            pairs(2 * (done + g * u), u)
            return carry

        lax.fori_loop(0, n_groups, group, 0)
        done = done + n_groups * u

    @pl.when(i % 2 == 0)
    def _():
        softmax_pv(s_even, i, True)

    @pl.when(i % 2 == 1)
    def _():
        s_odd[...] = scores(i)
        softmax_pv(s_even, i - 1, False)
        softmax_pv(s_odd, i, True)

    out_t = acc_sc[:V_HEAD_DIM, :] / acc_sc[V_HEAD_DIM:V_HEAD_DIM + 1, :]
    o_ref[...] = out_t.T.astype(o_ref.dtype)


BF16_SUBLANES = 16


def _mla_attn(q_t, k, v_t, t, cast_weights=(), q_tiles_per_step=4, unrolls=(8, 4, 2, 1)):
    s = k.shape[0]
    h = MLA_HEADS
    g = q_tiles_per_step
    n_i = s // (t * g)
    n_steps = h * n_i
    assert unrolls[-1] == 1
    cast_specs = []
    for w in cast_weights:
        rows, cols = w.shape
        share = next(sh for sh in (1, 2, 4, 8) if rows % (n_steps // sh * BF16_SUBLANES) == 0)
        cast_specs.append(pl.BlockSpec((rows // (n_steps // share), cols),
                                       lambda hh, i, share=share: ((hh * n_i + i) // share, 0)))
    outs = pl.pallas_call(
        functools.partial(_mla_attn_kernel, t=t, unrolls=unrolls, n_cast=len(cast_weights)),
        grid=(h, n_i),
        in_specs=[pl.BlockSpec((None, g, QK_PAD_DIM, t), lambda hh, i: (hh, i, 0, 0)),
                  pl.BlockSpec((s, QK_PAD_DIM), lambda hh, i: (0, hh)),
                  pl.BlockSpec((None, s // t, V_ROWS, t), lambda hh, i: (hh, 0, 0, 0))] + cast_specs,
        out_specs=[pl.BlockSpec((g * t, V_HEAD_DIM), lambda hh, i: (i, hh))] + cast_specs,
        out_shape=[jax.ShapeDtypeStruct((s, h * V_HEAD_DIM), BF16)]
        + [jax.ShapeDtypeStruct(w.shape, BF16) for w in cast_weights],
        scratch_shapes=[pltpu.VMEM((1, t), F32),
                        pltpu.VMEM((V_ROWS, t), F32),
                        pltpu.VMEM((t, t), F32), pltpu.VMEM((t, t), F32)],
        compiler_params=_params(2),
        name="mla_attn",
    )(q_t, k, v_t, *cast_weights)
    return outs[0], outs[1:]


def _sb_attn_kernel(q_ref, k_ref, v_ref, o_ref, tri_sc, *bufs, t):
    n_tiles = q_ref.shape[0] // t
    w = 2 * t
    z_sc, lom_sc, lb_sc, a_sc, wsum_sc, carry_sc, acc_sc = [bufs[2 * kind:2 * kind + 2] for kind in range(7)]
    tri_sc[...] = jnp.where(lax.broadcasted_iota(jnp.int32, (w, w), 0)
                            > lax.broadcasted_iota(jnp.int32, (w, w), 1), -1.0, 0.0).astype(BF16)

    def log_probs(z):
        neg_log_om = jnp.maximum(z, 0.0) + jnp.log2(1.0 + jnp.exp2(-jnp.abs(z)))
        return neg_log_om, z - neg_log_om

    def tile_rows(i):
        return pl.ds(pl.multiple_of(i * t, t), t)

    def window_start(i):
        return pl.multiple_of(jnp.maximum(i - 1, 0) * t, t)

    def window_scores(slot, i):
        i = jnp.minimum(i, n_tiles - 1)
        z_sc[slot][...] = _dot_nt(q_ref[tile_rows(i), :], k_ref[pl.ds(window_start(i), w), :])

    def window_log_probs(slot, first_tile=False):
        neg_log_om, log_beta = log_probs(z_sc[slot][...])
        lom, lb, wsum = lom_sc[slot], lb_sc[slot], wsum_sc[slot]
        strict = (lax.broadcasted_iota(jnp.int32, (t, t), 1)
                  < lax.broadcasted_iota(jnp.int32, (t, t), 0))
        if first_tile:
            halves = ((strict, neg_log_om[:, :t], log_beta[:, :t]), (None, None, None))
        else:
            halves = ((True, neg_log_om[:, :t], log_beta[:, :t]), (strict, neg_log_om[:, t:], log_beta[:, t:]))
        total = jnp.zeros((t, 1), F32)
        for half, (mask, nlo, lbeta) in enumerate(halves):
            cols = slice(half * t, (half + 1) * t)
            if mask is None:
                lom[:, cols] = jnp.zeros((t, t), BF16)
                lb[:, cols] = jnp.full((t, t), -jnp.inf, F32)
                continue
            if mask is not True:
                nlo = jnp.where(mask, nlo, 0.0)
                lbeta = jnp.where(mask, lbeta, -jnp.inf)
            lom[:, cols] = nlo.astype(BF16)
            lb[:, cols] = lbeta
            total = total + jnp.sum(nlo, axis=1, keepdims=True)
        wsum[...] = jnp.broadcast_to(-total, (t, LANES))

    def window_weights(slot):
        lom, lb, a = lom_sc[slot], lb_sc[slot], a_sc[slot]
        a[:, :t] = jnp.exp2(lb[:, :t] + _dot(lom[...], tri_sc[:, :t])).astype(BF16)
        a[:, t:] = jnp.exp2(lb[:, t:] + _dot(lom[:, t:], tri_sc[t:, t:])).astype(BF16)
        carry_sc[slot][...] = wsum_sc[slot][...]

    def window_values(slot, i):
        acc_sc[slot][...] = _dot(a_sc[slot][...], v_ref[pl.ds(window_start(i), w), :])

    def earlier_tiles(slot, i):
        carry, acc = carry_sc[slot], acc_sc[slot]
        q = q_ref[tile_rows(i), :]

        def more(state):
            j, carry_max = state
            return (j >= 0) & (carry_max > SB_LOG2_WEIGHT_FLOOR)

        def body(state):
            j, _ = state
            neg_log_om, log_beta = log_probs(_dot_nt(q, k_ref[tile_rows(j), :]))
            after = _dot(neg_log_om.astype(BF16), tri_sc[:t, :t]) + jnp.tile(carry[...], (1, t // LANES))
            a = jnp.exp2(log_beta + after)
            acc[...] += _dot(a.astype(BF16), v_ref[tile_rows(j), :])
            carry[...] -= jnp.sum(neg_log_om, axis=1, keepdims=True)
            return j - 1, jnp.max(carry[...])

        lax.while_loop(more, body, (i - 2, jnp.max(carry[...])))
        o_ref[tile_rows(i), :] = acc[...].astype(o_ref.dtype)

    def step(s, slot):
        window_scores(slot, s + 2)
        window_log_probs(1 - slot)
        window_weights(slot)
        window_values(1 - slot, s - 1)
        earlier_tiles(1 - slot, s - 1)

    def step_pair(p, _):
        step(2 * p + 1, 1)
        step(2 * p + 2, 0)
        return 0

    window_scores(0, 0)
    window_log_probs(0, first_tile=True)
    window_scores(1, 1)
    window_weights(0)
    window_log_probs(1)
    window_scores(0, 2)
    lax.fori_loop(0, n_tiles // 2, step_pair, 0)


def _sb_attn(qkv, t=256):
    s = qkv.shape[0]
    h, d = SB_HEADS, SB_HEAD_DIM
    assert s % (2 * t) == 0 and s >= 3 * t
    return pl.pallas_call(
        functools.partial(_sb_attn_kernel, t=t),
        grid=(h,),
        in_specs=[pl.BlockSpec((s, d), lambda hh: (0, hh)),
                  pl.BlockSpec((s, d), lambda hh: (0, h + hh)),
                  pl.BlockSpec((s, d), lambda hh: (0, 2 * h + hh))],
        out_specs=pl.BlockSpec((s, d), lambda hh: (0, hh)),
        out_shape=jax.ShapeDtypeStruct((s, h * d), BF16),
        scratch_shapes=[pltpu.VMEM((2 * t, 2 * t), BF16)]
        + 2 * [pltpu.VMEM((t, 2 * t), F32)]
        + 2 * [pltpu.VMEM((t, 2 * t), BF16)]
        + 2 * [pltpu.VMEM((t, 2 * t), F32)]
        + 2 * [pltpu.VMEM((t, 2 * t), BF16)]
        + 2 * [pltpu.VMEM((t, LANES), F32)]
        + 2 * [pltpu.VMEM((t, LANES), F32)]
        + 2 * [pltpu.VMEM((t, d), F32)],
        compiler_params=_params(1),
        name="sb_attn",
    )(qkv, qkv, qkv)


def _normalize_rows_once(h_ref, g_ref, a_sc):
    @pl.when(pl.program_id(1) == 0)
    def _():
        a_sc[...] = _rms(h_ref[...], g_ref[...]).astype(BF16)


def _norm_matmul_scale_kernel(h_ref, g_ref, w_ref, cs_ref, o_ref, a_sc):
    _normalize_rows_once(h_ref, g_ref, a_sc)
    o_ref[...] = (_dot(a_sc[...], w_ref[...].astype(BF16)) * cs_ref[...]).astype(o_ref.dtype)


def _norm_matmul_scale(h, norm_g, w, layer, col_scale, tm=1024, tn=1536):
    s, k = h.shape
    n = w.shape[2]
    return pl.pallas_call(
        _norm_matmul_scale_kernel,
        grid=(s // tm, n // tn),
        in_specs=[pl.BlockSpec((tm, k), lambda i, j: (i, 0)), pl.BlockSpec((1, k), lambda i, j: (0, 0)),
                  pl.BlockSpec((None, k, tn), lambda i, j: (layer, 0, j)),
                  pl.BlockSpec((1, tn), lambda i, j: (0, j))],
        out_specs=pl.BlockSpec((tm, tn), lambda i, j: (i, j)),
        out_shape=jax.ShapeDtypeStruct((s, n), BF16),
        scratch_shapes=[pltpu.VMEM((tm, k), BF16)],
        compiler_params=_params(2),
        name="norm_matmul_scale",
    )(h, norm_g[None, :], w, col_scale[None, :])


def _matmul_residual_kernel(a_ref, w_ref, r_ref, o_ref):
    o_ref[...] = r_ref[...] + _dot(a_ref[...], w_ref[...].astype(BF16))


def _matmul_residual(a, w, layer, res, tm, tn):
    s, k = a.shape
    n = w.shape[2]
    return pl.pallas_call(
        _matmul_residual_kernel,
        grid=(s // tm, n // tn),
        in_specs=[pl.BlockSpec((tm, k), lambda i, j: (i, 0)),
                  pl.BlockSpec((None, k, tn), lambda i, j: (layer, 0, j)),
                  pl.BlockSpec((tm, tn), lambda i, j: (i, j))],
        out_specs=pl.BlockSpec((tm, tn), lambda i, j: (i, j)),
        out_shape=jax.ShapeDtypeStruct((s, n), F32),
        compiler_params=_params(2),
        name="matmul_residual",
    )(a, w, res)


def _matmul_residual_norm_kernel(a_ref, w_ref, r_ref, g_ref, o_ref):
    kk = pl.program_id(1)

    @pl.when(kk == 0)
    def _():
        o_ref[...] = r_ref[...]

    o_ref[...] += _dot(a_ref[...], w_ref[...].astype(BF16))

    @pl.when(kk == pl.num_programs(1) - 1)
    def _():
        o_ref[...] = _rms(o_ref[...], g_ref[...])


def _matmul_residual_norm(a, w, layer, res, norm_g, tm=512, tk=2816):
    s, k = a.shape
    n = w.shape[2]
    return pl.pallas_call(
        _matmul_residual_norm_kernel,
        grid=(s // tm, k // tk),
        in_specs=[pl.BlockSpec((tm, tk), lambda i, kk: (i, kk)),
                  pl.BlockSpec((None, tk, n), lambda i, kk: (layer, kk, 0)),
                  pl.BlockSpec((tm, n), lambda i, kk: (i, 0)),
                  pl.BlockSpec((1, n), lambda i, kk: (0, 0))],
        out_specs=pl.BlockSpec((tm, n), lambda i, kk: (i, 0)),
        out_shape=jax.ShapeDtypeStruct((s, n), F32),
        compiler_params=_params(2),
        name="matmul_residual_norm",
    )(a, w, res, norm_g[None, :])


def _ffn_up_kernel(h_ref, g_ref, wg_ref, wu_ref, o_ref, a_sc):
    _normalize_rows_once(h_ref, g_ref, a_sc)
    a = a_sc[...]
    g = _dot(a, wg_ref[...].astype(BF16))
    u = _dot(a, wu_ref[...].astype(BF16))
    o_ref[...] = (g * jax.nn.sigmoid(g) * u).astype(o_ref.dtype)


def _ffn_up(h, norm_g, w_gate_up, layer, tm=1024, tn=512):
    s, k = h.shape
    d_ff = w_gate_up.shape[2] // 2
    nj = d_ff // tn
    return pl.pallas_call(
        _ffn_up_kernel,
        grid=(s // tm, nj),
        in_specs=[pl.BlockSpec((tm, k), lambda i, j: (i, 0)), pl.BlockSpec((1, k), lambda i, j: (0, 0)),
                  pl.BlockSpec((None, k, tn), lambda i, j: (layer, 0, j)),
                  pl.BlockSpec((None, k, tn), lambda i, j: (layer, 0, nj + j))],
        out_specs=pl.BlockSpec((tm, tn), lambda i, j: (i, j)),
        out_shape=jax.ShapeDtypeStruct((s, d_ff), BF16),
        scratch_shapes=[pltpu.VMEM((tm, k), BF16)],
        compiler_params=_params(2),
        name="ffn_up",
    )(h, norm_g[None, :], w_gate_up, w_gate_up)


def _ffn(h, norm_g, w_gate_up, w_down, layer, out_norm_g=None):
    act = _ffn_up(h, norm_g, w_gate_up, layer)
    if out_norm_g is not None:
        return _matmul_residual_norm(act, w_down, layer, h, out_norm_g)
    return _matmul_residual(act, w_down, layer, h, tm=1024, tn=512)


def kernel(x, positions, attn_norm, mla_w_in, mla_q_norm, mla_kv_norm, mla_w_uq, mla_w_ukv, mla_w_o,
           sb_w_in, sb_w_o, ffn_norm, ffn_w_gate_up, ffn_w_down, final_norm):
    b, s, d = x.shape
    assert b == 1 and d == D_MODEL
    h = x.reshape(s, d)

    attn_tile = 512
    tabs = _rope_tables(positions.reshape(s, 1))
    tabs_t = _rope_tables_t(positions.reshape(1, s))
    w_in_pad = jnp.pad(mla_w_in[0], ((0, 0), (0, LANES - QK_ROPE_DIM))).astype(BF16)
    qk_dim = QK_NOPE_DIM + QK_ROPE_DIM
    w_uq_pad = jnp.pad(mla_w_uq[0].reshape(Q_LORA_RANK, MLA_HEADS, qk_dim),
                       ((0, 0), (0, 0), (0, QK_PAD_DIM - qk_dim)))
    w_uq_pad_t = w_uq_pad.reshape(Q_LORA_RANK, MLA_HEADS * QK_PAD_DIM).T.astype(BF16)
    w_ukv = mla_w_ukv[0].reshape(KV_LORA_RANK, MLA_HEADS, QK_NOPE_DIM + V_HEAD_DIM)
    w_uk = w_ukv[:, :, :QK_NOPE_DIM].reshape(KV_LORA_RANK, MLA_HEADS * QK_NOPE_DIM).astype(BF16)
    w_uv_t = w_ukv[:, :, QK_NOPE_DIM:].reshape(KV_LORA_RANK, MLA_HEADS * V_HEAD_DIM).T.astype(BF16)

    cq, ckv, kr = _mla_in(h, attn_norm[0], w_in_pad, mla_q_norm[0], mla_kv_norm[0], tabs)
    q_t = _mla_q(cq, w_uq_pad_t, tabs_t, qk_dim ** -0.5 * LOG2_E, attn_tile)
    k, v_t = _mla_kv(ckv, w_uk, w_uv_t, kr, attn_tile)
    to_cast = (mla_w_o, sb_w_in, sb_w_o, ffn_w_gate_up, ffn_w_down)
    o, cast = _mla_attn(q_t, k, v_t, attn_tile, cast_weights=[w.reshape(-1, w.shape[2]) for w in to_cast])
    w_o, w_sb_in, w_sb_o, w_gate_up, w_down = [c.reshape(w.shape) for c, w in zip(cast, to_cast)]
    h = _matmul_residual(o, w_o, 0, h, tm=1024, tn=1024)
    h = _ffn(h, ffn_norm[0], w_gate_up, w_down, 0)

    n_q = SB_HEADS * SB_HEAD_DIM
    col_scale = jnp.concatenate([jnp.full((n_q,), SB_HEAD_DIM ** -0.5 * LOG2_E, F32), jnp.ones((2 * n_q,), F32)])
    qkv = _norm_matmul_scale(h, attn_norm[1], w_sb_in, 0, col_scale)
    o = _sb_attn(qkv)
    h = _matmul_residual(o, w_sb_o, 0, h, tm=1024, tn=1024)
    return _ffn(h, ffn_norm[1], w_gate_up, w_down, 1, out_norm_g=final_norm).reshape(b, s, d)
```

```python
import functools

import jax
import jax.numpy as jnp
from jax import lax
from jax.experimental import pallas as pl
from jax.experimental.pallas import tpu as pltpu

D_MODEL = 2048
MLA_HEADS = 16
Q_LORA_RANK = 512
KV_LORA_RANK = 512
QK_NOPE_DIM = 128
QK_ROPE_DIM = 64
V_HEAD_DIM = 128
ROPE_THETA = 10000.0
SB_HEADS = 16
SB_HEAD_DIM = 128
RMS_EPS = 1e-6

LANES = 128
QK_PAD_DIM = 2 * LANES
VMEM_LIMIT_BYTES = 56 * 1024 * 1024

LOG2_E = 1.4426950408889634
SB_LOG2_WEIGHT_FLOOR = -120.0 * LOG2_E

F32 = jnp.float32
BF16 = jnp.bfloat16


def _params(n_axes):
    return pltpu.CompilerParams(
        dimension_semantics=("arbitrary",) * n_axes, vmem_limit_bytes=VMEM_LIMIT_BYTES)


def _dot(a, b):
    return jnp.dot(a, b, preferred_element_type=F32)


def _dot_nt(a, b):
    return lax.dot_general(a, b, (((1,), (1,)), ((), ())), preferred_element_type=F32)


def _rms(x, g):
    return x * lax.rsqrt(jnp.mean(x * x, axis=-1, keepdims=True) + RMS_EPS) * g


def _rope(t, c, s1, s2):
    half = QK_ROPE_DIM // 2
    return t * c + pltpu.roll(t, half, 1) * s1 + pltpu.roll(t, LANES - half, 1) * s2


def _rope_tables_kernel(pos_ref, freq_ref, c_ref, s1_ref, s2_ref):
    half = QK_ROPE_DIM // 2
    ang = pos_ref[...].astype(F32) * freq_ref[...]
    lane = lax.broadcasted_iota(jnp.int32, ang.shape, 1)
    cos, sin = jnp.cos(ang), jnp.sin(ang)
    c_ref[...] = jnp.where(lane < 2 * half, cos, 0.0)
    s1_ref[...] = jnp.where((lane >= half) & (lane < 2 * half), sin, 0.0)
    s2_ref[...] = jnp.where(lane < half, -sin, 0.0)


def _rope_tables(positions, tm=2048):
    s = positions.shape[0]
    inv_freq = ROPE_THETA ** (-jnp.arange(0, QK_ROPE_DIM, 2, dtype=F32) / QK_ROPE_DIM)
    freq = jnp.concatenate([inv_freq, inv_freq, jnp.zeros((LANES - QK_ROPE_DIM,), F32)])[None, :]
    out = jax.ShapeDtypeStruct((s, LANES), F32)
    row = pl.BlockSpec((tm, LANES), lambda i: (i, 0))
    return pl.pallas_call(
        _rope_tables_kernel,
        grid=(s // tm,),
        in_specs=[pl.BlockSpec((tm, 1), lambda i: (i, 0)), pl.BlockSpec((1, LANES), lambda i: (0, 0))],
        out_specs=[row, row, row],
        out_shape=[out, out, out],
        compiler_params=_params(1),
        name="rope_tables",
    )(positions, freq)


def _rope_tables_t_kernel(pos_ref, freq_ref, c_ref, s1_ref, s2_ref):
    half = QK_ROPE_DIM // 2
    ang = freq_ref[...] * pos_ref[...].astype(F32)
    r = lax.broadcasted_iota(jnp.int32, ang.shape, 0)
    cos, sin = jnp.cos(ang), jnp.sin(ang)
    c_ref[...] = jnp.where(r < 2 * half, cos, 0.0)
    s1_ref[...] = jnp.where((r >= half) & (r < 2 * half), sin, 0.0)
    s2_ref[...] = jnp.where(r < half, -sin, 0.0)


def _rope_tables_t(positions, tm=2048):
    s = positions.shape[1]
    inv_freq = ROPE_THETA ** (-jnp.arange(0, QK_ROPE_DIM, 2, dtype=F32) / QK_ROPE_DIM)
    freq = jnp.concatenate([inv_freq, inv_freq, jnp.zeros((LANES - QK_ROPE_DIM,), F32)])[:, None]
    out = jax.ShapeDtypeStruct((LANES, s), F32)
    col = pl.BlockSpec((LANES, tm), lambda i: (0, i))
    return pl.pallas_call(
        _rope_tables_t_kernel,
        grid=(s // tm,),
        in_specs=[pl.BlockSpec((1, tm), lambda i: (0, i)), pl.BlockSpec((LANES, 1), lambda i: (0, 0))],
        out_specs=[col, col, col],
        out_shape=[out, out, out],
        compiler_params=_params(1),
        name="rope_tables_t",
    )(positions, freq)


def _mla_in_kernel(h_ref, g_ref, w_ref, qn_ref, kvn_ref, c_ref, s1_ref, s2_ref, cq_ref, ckv_ref, kr_ref):
    a = _rms(h_ref[...], g_ref[...]).astype(BF16)
    proj = _dot(a, w_ref[...])
    cq_ref[...] = _rms(proj[:, :Q_LORA_RANK], qn_ref[...]).astype(BF16)
    ckv_ref[...] = _rms(proj[:, Q_LORA_RANK:Q_LORA_RANK + KV_LORA_RANK], kvn_ref[...]).astype(BF16)
    kr = proj[:, Q_LORA_RANK + KV_LORA_RANK:]
    kr_ref[...] = _rope(kr, c_ref[...], s1_ref[...], s2_ref[...]).astype(BF16)


def _mla_in(h, norm_g, w_pad, q_norm, kv_norm, tabs, tm=1024):
    s, d = h.shape
    n = w_pad.shape[1]
    row = lambda w: pl.BlockSpec((tm, w), lambda i: (i, 0))
    const = lambda r, w: pl.BlockSpec((r, w), lambda i: (0, 0))
    return pl.pallas_call(
        _mla_in_kernel,
        grid=(s // tm,),
        in_specs=[row(d), const(1, d), const(d, n), const(1, Q_LORA_RANK), const(1, KV_LORA_RANK),
                  row(LANES), row(LANES), row(LANES)],
        out_specs=[row(Q_LORA_RANK), row(KV_LORA_RANK), row(LANES)],
        out_shape=[jax.ShapeDtypeStruct((s, Q_LORA_RANK), BF16),
                   jax.ShapeDtypeStruct((s, KV_LORA_RANK), BF16),
                   jax.ShapeDtypeStruct((s, LANES), BF16)],
        compiler_params=_params(1),
        name="mla_in",
    )(h, norm_g[None, :], w_pad, q_norm[None, :], kv_norm[None, :], *tabs)


def _mla_q_kernel(w_ref, cq_ref, c_ref, s1_ref, s2_ref, q_ref, *, heads, scale):
    acc = _dot_nt(w_ref[...], cq_ref[...])
    c, s1, s2 = c_ref[...], s1_ref[...], s2_ref[...]
    half = QK_ROPE_DIM // 2
    tq = q_ref.shape[3]
    for h in range(heads):
        lo = h * QK_PAD_DIM
        nope = (acc[lo:lo + LANES, :] * scale).astype(BF16)
        t = acc[lo + LANES:lo + QK_PAD_DIM, :]
        down = jnp.concatenate([t[LANES - half:], t[:LANES - half]], axis=0)
        up = jnp.concatenate([t[half:], t[:half]], axis=0)
        rope = ((t * c + down * s1 + up * s2) * scale).astype(BF16)
        for tt in range(acc.shape[1] // tq):
            q_ref[h, tt, :LANES, :] = nope[:, tt * tq:(tt + 1) * tq]
            q_ref[h, tt, LANES:, :] = rope[:, tt * tq:(tt + 1) * tq]


def _mla_q(cq, w_uq_pad_t, tabs_t, scale, t, tm=1024, heads_per_tile=8):
    s, r = cq.shape
    n = w_uq_pad_t.shape[0]
    tn = heads_per_tile * QK_PAD_DIM
    col = pl.BlockSpec((LANES, tm), lambda j, i: (0, i))
    return pl.pallas_call(
        functools.partial(_mla_q_kernel, heads=heads_per_tile, scale=scale),
        grid=(n // tn, s // tm),
        in_specs=[pl.BlockSpec((tn, r), lambda j, i: (j, 0)), pl.BlockSpec((tm, r), lambda j, i: (i, 0)),
                  col, col, col],
        out_specs=pl.BlockSpec((heads_per_tile, tm // t, QK_PAD_DIM, t), lambda j, i: (j, i, 0, 0)),
        out_shape=jax.ShapeDtypeStruct((n // QK_PAD_DIM, s // t, QK_PAD_DIM, t), BF16),
        compiler_params=_params(2),
        name="mla_q",
    )(w_uq_pad_t, cq, *tabs_t)


V_ROWS = V_HEAD_DIM + 16


def _mla_kv_kernel(ckv_ref, wk_ref, wv_ref, kr_ref, k_ref, v_ref, *, heads, t):
    ckv = ckv_ref[...]
    kn = _dot(ckv, wk_ref[...])
    vt = _dot_nt(wv_ref[...], ckv)
    kr = kr_ref[...]
    ones = jnp.ones((V_ROWS - V_HEAD_DIM, t), BF16)
    for h in range(heads):
        k_ref[:, h * QK_PAD_DIM:h * QK_PAD_DIM + LANES] = kn[:, h * LANES:(h + 1) * LANES].astype(BF16)
        k_ref[:, h * QK_PAD_DIM + LANES:(h + 1) * QK_PAD_DIM] = kr
        for tt in range(ckv.shape[0] // t):
            v_ref[h, tt, :V_HEAD_DIM, :] = vt[h * V_HEAD_DIM:(h + 1) * V_HEAD_DIM, tt * t:(tt + 1) * t].astype(BF16)
            v_ref[h, tt, V_HEAD_DIM:, :] = ones


def _mla_kv(ckv, w_uk, w_uv_t, kr, t, tm=1024, heads_per_tile=8):
    s, r = ckv.shape
    n = w_uk.shape[1]
    heads = n // LANES
    tn = heads_per_tile * LANES
    return pl.pallas_call(
        functools.partial(_mla_kv_kernel, heads=heads_per_tile, t=t),
        grid=(s // tm, n // tn),
        in_specs=[pl.BlockSpec((tm, r), lambda i, j: (i, 0)),
                  pl.BlockSpec((r, tn), lambda i, j: (0, j)),
                  pl.BlockSpec((tn, r), lambda i, j: (j, 0)),
                  pl.BlockSpec((tm, LANES), lambda i, j: (i, 0))],
        out_specs=[pl.BlockSpec((tm, 2 * tn), lambda i, j: (i, j)),
                   pl.BlockSpec((heads_per_tile, tm // t, V_ROWS, t), lambda i, j: (j, i, 0, 0))],
        out_shape=[jax.ShapeDtypeStruct((s, 2 * n), BF16),
                   jax.ShapeDtypeStruct((heads, s // t, V_ROWS, t), BF16)],
        compiler_params=_params(2),
        name="mla_kv",
    )(ckv, w_uk, w_uv_t, kr)


def _mla_attn_kernel(q_ref, k_ref, v_ref, *rest, t, unrolls, n_cast):
    w_refs, o_ref, w_bf16_refs = rest[:n_cast], rest[n_cast], rest[n_cast + 1:2 * n_cast + 1]
    m_sc, acc_sc, s_even, s_odd = rest[2 * n_cast + 1:]
    for w_ref, w_bf16_ref in zip(w_refs, w_bf16_refs):
        w_bf16_ref[...] = w_ref[...].astype(BF16)

    def q_tile(qi, carry):
        _mla_attn_q_tile(pl.program_id(1) * q_ref.shape[0] + qi, q_ref[qi], k_ref, v_ref,
                         o_ref.at[pl.ds(pl.multiple_of(qi * t, t), t), :],
                         m_sc, acc_sc, s_even, s_odd, t=t, unrolls=unrolls)
        return carry

    lax.fori_loop(0, q_ref.shape[0], q_tile, 0)


def _mla_attn_q_tile(i, q, k_ref, v_ref, o_ref, m_sc, acc_sc, s_even, s_odd, *, t, unrolls):
    m_sc[...] = jnp.full(m_sc.shape, -jnp.inf, F32)
    acc_sc[...] = jnp.zeros(acc_sc.shape, F32)

    def scores(j):
        return _dot(k_ref[pl.ds(pl.multiple_of(j * t, t), t), :], q)

    def softmax_pv(s_ref, j, diagonal):
        def load():
            s = s_ref[...]
            if diagonal:
                key = lax.broadcasted_iota(jnp.int32, (t, t), 0)
                query = lax.broadcasted_iota(jnp.int32, (t, t), 1)
                s = jnp.where(key <= query, s, -jnp.inf)
            return s
        m_prev = m_sc[...]
        m_next = jnp.maximum(m_prev, jnp.max(load(), axis=0, keepdims=True))
        m_sc[...] = m_next
        p = jnp.exp2(load() - m_sc[...]).astype(BF16)
        alpha = jnp.exp2(m_prev - m_next)
        acc_sc[...] = alpha * acc_sc[...] + _dot(v_ref[j], p)

    s_even[...] = scores(0)

    def pairs(first, n_pairs):
        for d in range(n_pairs):
            j = first + 2 * d
            s_odd[...] = scores(j + 1)
            softmax_pv(s_even, j, False)
            s_even[...] = scores(j + 2)
            softmax_pv(s_odd, j + 1, False)

    done = 0
    for u in unrolls:
        n_groups = (i // 2 - done) // u

        def group(g, carry, u=u, done=done):
            pairs(2 * (done + g * u), u)
            return carry

        lax.fori_loop(0, n_groups, group, 0)
        done = done + n_groups * u

    @pl.when(i % 2 == 0)
    def _():
        softmax_pv(s_even, i, True)

    @pl.when(i % 2 == 1)
    def _():
        s_odd[...] = scores(i)
        softmax_pv(s_even, i - 1, False)
        softmax_pv(s_odd, i, True)

    out_t = acc_sc[:V_HEAD_DIM, :] / acc_sc[V_HEAD_DIM:V_HEAD_DIM + 1, :]
    o_ref[...] = out_t.T.astype(o_ref.dtype)


BF16_SUBLANES = 16


def _mla_attn(q_t, k, v_t, t, cast_weights=(), q_tiles_per_step=4, unrolls=(8, 4, 2, 1)):
    s = k.shape[0]
    h = MLA_HEADS
    g = q_tiles_per_step
    n_i = s // (t * g)
    n_steps = h * n_i
    assert unrolls[-1] == 1
    cast_specs = []
    for w in cast_weights:
        rows, cols = w.shape
        share = next(sh for sh in (1, 2, 4, 8) if rows % (n_steps // sh * BF16_SUBLANES) == 0)
        cast_specs.append(pl.BlockSpec((rows // (n_steps // share), cols),
                                       lambda hh, i, share=share: ((hh * n_i + i) // share, 0)))
    outs = pl.pallas_call(
        functools.partial(_mla_attn_kernel, t=t, unrolls=unrolls, n_cast=len(cast_weights)),
        grid=(h, n_i),
        in_specs=[pl.BlockSpec((None, g, QK_PAD_DIM, t), lambda hh, i: (hh, i, 0, 0)),
                  pl.BlockSpec((s, QK_PAD_DIM), lambda hh, i: (0, hh)),
                  pl.BlockSpec((None, s // t, V_ROWS, t), lambda hh, i: (hh, 0, 0, 0))] + cast_specs,
        out_specs=[pl.BlockSpec((g * t, V_HEAD_DIM), lambda hh, i: (i, hh))] + cast_specs,
        out_shape=[jax.ShapeDtypeStruct((s, h * V_HEAD_DIM), BF16)]
        + [jax.ShapeDtypeStruct(w.shape, BF16) for w in cast_weights],
        scratch_shapes=[pltpu.VMEM((1, t), F32),
                        pltpu.VMEM((V_ROWS, t), F32),
                        pltpu.VMEM((t, t), F32), pltpu.VMEM((t, t), F32)],
        compiler_params=_params(2),
        name="mla_attn",
    )(q_t, k, v_t, *cast_weights)
    return outs[0], outs[1:]


def _sb_attn_kernel(q_ref, k_ref, v_ref, o_ref, tri_sc, cmax_sm, *bufs, t):
    n_tiles = q_ref.shape[0] // t
    w = 2 * t
    z_sc, lom_sc, lb_sc, a_sc, wsum_sc, carry_sc, acc_sc = [bufs[2 * kind:2 * kind + 2] for kind in range(7)]
    tri_sc[...] = jnp.where(lax.broadcasted_iota(jnp.int32, (w, w), 0)
                            > lax.broadcasted_iota(jnp.int32, (w, w), 1), -1.0, 0.0).astype(BF16)

    def log_probs(z):
        neg_log_om = jnp.maximum(z, 0.0) + jnp.log2(1.0 + jnp.exp2(-jnp.abs(z)))
        return neg_log_om, z - neg_log_om

    def tile_rows(i):
        return pl.ds(pl.multiple_of(i * t, t), t)

    def window_start(i):
        return pl.multiple_of(jnp.maximum(i - 1, 0) * t, t)

    def window_scores(slot, i):
        i = jnp.minimum(i, n_tiles - 1)
        z_sc[slot][...] = _dot_nt(q_ref[tile_rows(i), :], k_ref[pl.ds(window_start(i), w), :])

    def window_log_probs(slot, first_tile=False):
        neg_log_om, log_beta = log_probs(z_sc[slot][...])
        lom, lb, wsum = lom_sc[slot], lb_sc[slot], wsum_sc[slot]
        strict = (lax.broadcasted_iota(jnp.int32, (t, t), 1)
                  < lax.broadcasted_iota(jnp.int32, (t, t), 0))
        if first_tile:
            halves = ((strict, neg_log_om[:, :t], log_beta[:, :t]), (None, None, None))
        else:
            halves = ((True, neg_log_om[:, :t], log_beta[:, :t]), (strict, neg_log_om[:, t:], log_beta[:, t:]))
        total = jnp.zeros((t, 1), F32)
        for half, (mask, nlo, lbeta) in enumerate(halves):
            cols = slice(half * t, (half + 1) * t)
            if mask is None:
                lom[:, cols] = jnp.zeros((t, t), BF16)
                lb[:, cols] = jnp.full((t, t), -jnp.inf, F32)
                continue
            if mask is not True:
                nlo = jnp.where(mask, nlo, 0.0)
                lbeta = jnp.where(mask, lbeta, -jnp.inf)
            lom[:, cols] = nlo.astype(BF16)
            lb[:, cols] = lbeta
            total = total + jnp.sum(nlo, axis=1, keepdims=True)
        wsum[...] = jnp.broadcast_to(-total, (t, LANES))
        cmax_sm[slot] = jnp.max(-total)

    def window_weights(slot):
        lom, lb, a = lom_sc[slot], lb_sc[slot], a_sc[slot]
        a[:, :t] = jnp.exp2(lb[:, :t] + _dot(lom[...], tri_sc[:, :t])).astype(BF16)
        a[:, t:] = jnp.exp2(lb[:, t:] + _dot(lom[:, t:], tri_sc[t:, t:])).astype(BF16)
        carry_sc[slot][...] = wsum_sc[slot][...]
        cmax_sm[2 + slot] = cmax_sm[slot]

    def window_values(slot, i):
        acc_sc[slot][...] = _dot(a_sc[slot][...], v_ref[pl.ds(window_start(i), w), :])

    def earlier_tiles(slot, i):
        carry, acc = carry_sc[slot], acc_sc[slot]
        q = q_ref[tile_rows(i), :]

        def more(state):
            j, carry_max = state
            return (j >= 0) & (carry_max > SB_LOG2_WEIGHT_FLOOR)

        def body(state):
            j, _ = state
            neg_log_om, log_beta = log_probs(_dot_nt(q, k_ref[tile_rows(j), :]))
            after = _dot(neg_log_om.astype(BF16), tri_sc[:t, :t]) + jnp.tile(carry[...], (1, t // LANES))
            a = jnp.exp2(log_beta + after)
            acc[...] += _dot(a.astype(BF16), v_ref[tile_rows(j), :])
            carry[...] -= jnp.sum(neg_log_om, axis=1, keepdims=True)
            return j - 1, jnp.max(carry[...])

        lax.while_loop(more, body, (i - 2, cmax_sm[2 + slot]))
        o_ref[tile_rows(i), :] = acc[...].astype(o_ref.dtype)

    def step(s, slot):
        window_scores(slot, s + 2)
        window_log_probs(1 - slot)
        window_weights(slot)
        window_values(1 - slot, s - 1)
        earlier_tiles(1 - slot, s - 1)

    def step_pair(p, _):
        step(2 * p + 1, 1)
        step(2 * p + 2, 0)
        return 0

    window_scores(0, 0)
    window_log_probs(0, first_tile=True)
    window_scores(1, 1)
    window_weights(0)
    window_log_probs(1)
    window_scores(0, 2)
    lax.fori_loop(0, n_tiles // 2, step_pair, 0)


def _sb_attn(qkv, t=256):
    s = qkv.shape[0]
    h, d = SB_HEADS, SB_HEAD_DIM
    assert s % (2 * t) == 0 and s >= 3 * t
    return pl.pallas_call(
        functools.partial(_sb_attn_kernel, t=t),
        grid=(h,),
        in_specs=[pl.BlockSpec((s, d), lambda hh: (0, hh)),
                  pl.BlockSpec((s, d), lambda hh: (0, h + hh)),
                  pl.BlockSpec((s, d), lambda hh: (0, 2 * h + hh))],
        out_specs=pl.BlockSpec((s, d), lambda hh: (0, hh)),
        out_shape=jax.ShapeDtypeStruct((s, h * d), BF16),
        scratch_shapes=[pltpu.VMEM((2 * t, 2 * t), BF16),
                        pltpu.SMEM((4,), F32)]
        + 2 * [pltpu.VMEM((t, 2 * t), F32)]
        + 2 * [pltpu.VMEM((t, 2 * t), BF16)]
        + 2 * [pltpu.VMEM((t, 2 * t), F32)]
        + 2 * [pltpu.VMEM((t, 2 * t), BF16)]
        + 2 * [pltpu.VMEM((t, LANES), F32)]
        + 2 * [pltpu.VMEM((t, LANES), F32)]
        + 2 * [pltpu.VMEM((t, d), F32)],
        compiler_params=_params(1),
        name="sb_attn",
    )(qkv, qkv, qkv)


def _normalize_rows_once(h_ref, g_ref, a_sc):
    @pl.when(pl.program_id(1) == 0)
    def _():
        a_sc[...] = _rms(h_ref[...], g_ref[...]).astype(BF16)


def _norm_matmul_scale_kernel(h_ref, g_ref, w_ref, cs_ref, o_ref, a_sc):
    _normalize_rows_once(h_ref, g_ref, a_sc)
    o_ref[...] = (_dot(a_sc[...], w_ref[...].astype(BF16)) * cs_ref[...]).astype(o_ref.dtype)


def _norm_matmul_scale(h, norm_g, w, layer, col_scale, tm=1024, tn=1536):
    s, k = h.shape
    n = w.shape[2]
    return pl.pallas_call(
        _norm_matmul_scale_kernel,
        grid=(s // tm, n // tn),
        in_specs=[pl.BlockSpec((tm, k), lambda i, j: (i, 0)), pl.BlockSpec((1, k), lambda i, j: (0, 0)),
                  pl.BlockSpec((None, k, tn), lambda i, j: (layer, 0, j)),
                  pl.BlockSpec((1, tn), lambda i, j: (0, j))],
        out_specs=pl.BlockSpec((tm, tn), lambda i, j: (i, j)),
        out_shape=jax.ShapeDtypeStruct((s, n), BF16),
        scratch_shapes=[pltpu.VMEM((tm, k), BF16)],
        compiler_params=_params(2),
        name="norm_matmul_scale",
    )(h, norm_g[None, :], w, col_scale[None, :])


def _matmul_residual_kernel(a_ref, w_ref, r_ref, o_ref):
    o_ref[...] = r_ref[...] + _dot(a_ref[...], w_ref[...].astype(BF16))


def _matmul_residual(a, w, layer, res, tm, tn):
    s, k = a.shape
    n = w.shape[2]
    return pl.pallas_call(
        _matmul_residual_kernel,
        grid=(s // tm, n // tn),
        in_specs=[pl.BlockSpec((tm, k), lambda i, j: (i, 0)),
                  pl.BlockSpec((None, k, tn), lambda i, j: (layer, 0, j)),
                  pl.BlockSpec((tm, tn), lambda i, j: (i, j))],
        out_specs=pl.BlockSpec((tm, tn), lambda i, j: (i, j)),
        out_shape=jax.ShapeDtypeStruct((s, n), F32),
        compiler_params=_params(2),
        name="matmul_residual",
    )(a, w, res)


def _matmul_residual_norm_kernel(a_ref, w_ref, r_ref, g_ref, o_ref):
    kk = pl.program_id(1)

    @pl.when(kk == 0)
    def _():
        o_ref[...] = r_ref[...]

    o_ref[...] += _dot(a_ref[...], w_ref[...].astype(BF16))

    @pl.when(kk == pl.num_programs(1) - 1)
    def _():
        o_ref[...] = _rms(o_ref[...], g_ref[...])


def _matmul_residual_norm(a, w, layer, res, norm_g, tm=512, tk=2816):
    s, k = a.shape
    n = w.shape[2]
    return pl.pallas_call(
        _matmul_residual_norm_kernel,
        grid=(s // tm, k // tk),
        in_specs=[pl.BlockSpec((tm, tk), lambda i, kk: (i, kk)),
                  pl.BlockSpec((None, tk, n), lambda i, kk: (layer, kk, 0)),
                  pl.BlockSpec((tm, n), lambda i, kk: (i, 0)),
                  pl.BlockSpec((1, n), lambda i, kk: (0, 0))],
        out_specs=pl.BlockSpec((tm, n), lambda i, kk: (i, 0)),
        out_shape=jax.ShapeDtypeStruct((s, n), F32),
        compiler_params=_params(2),
        name="matmul_residual_norm",
    )(a, w, res, norm_g[None, :])


def _ffn_up_kernel(h_ref, g_ref, wg_ref, wu_ref, o_ref, a_sc):
    _normalize_rows_once(h_ref, g_ref, a_sc)
    a = a_sc[...]
    g = _dot(a, wg_ref[...].astype(BF16))
    u = _dot(a, wu_ref[...].astype(BF16))
    o_ref[...] = (g * jax.nn.sigmoid(g) * u).astype(o_ref.dtype)


def _ffn_up(h, norm_g, w_gate_up, layer, tm=1024, tn=512):
    s, k = h.shape
    d_ff = w_gate_up.shape[2] // 2
    nj = d_ff // tn
    return pl.pallas_call(
        _ffn_up_kernel,
        grid=(s // tm, nj),
        in_specs=[pl.BlockSpec((tm, k), lambda i, j: (i, 0)), pl.BlockSpec((1, k), lambda i, j: (0, 0)),
                  pl.BlockSpec((None, k, tn), lambda i, j: (layer, 0, j)),
                  pl.BlockSpec((None, k, tn), lambda i, j: (layer, 0, nj + j))],
        out_specs=pl.BlockSpec((tm, tn), lambda i, j: (i, j)),
        out_shape=jax.ShapeDtypeStruct((s, d_ff), BF16),
        scratch_shapes=[pltpu.VMEM((tm, k), BF16)],
        compiler_params=_params(2),
        name="ffn_up",
    )(h, norm_g[None, :], w_gate_up, w_gate_up)


def _ffn(h, norm_g, w_gate_up, w_down, layer, out_norm_g=None):
    act = _ffn_up(h, norm_g, w_gate_up, layer)
    if out_norm_g is not None:
        return _matmul_residual_norm(act, w_down, layer, h, out_norm_g)
    return _matmul_residual(act, w_down, layer, h, tm=1024, tn=512)


def kernel(x, positions, attn_norm, mla_w_in, mla_q_norm, mla_kv_norm, mla_w_uq, mla_w_ukv, mla_w_o,
           sb_w_in, sb_w_o, ffn_norm, ffn_w_gate_up, ffn_w_down, final_norm):
    b, s, d = x.shape
    assert b == 1 and d == D_MODEL
    h = x.reshape(s, d)

    attn_tile = 512
    tabs = _rope_tables(positions.reshape(s, 1))
    tabs_t = _rope_tables_t(positions.reshape(1, s))
    w_in_pad = jnp.pad(mla_w_in[0], ((0, 0), (0, LANES - QK_ROPE_DIM))).astype(BF16)
    qk_dim = QK_NOPE_DIM + QK_ROPE_DIM
    w_uq_pad = jnp.pad(mla_w_uq[0].reshape(Q_LORA_RANK, MLA_HEADS, qk_dim),
                       ((0, 0), (0, 0), (0, QK_PAD_DIM - qk_dim)))
    w_uq_pad_t = w_uq_pad.reshape(Q_LORA_RANK, MLA_HEADS * QK_PAD_DIM).T.astype(BF16)
    w_ukv = mla_w_ukv[0].reshape(KV_LORA_RANK, MLA_HEADS, QK_NOPE_DIM + V_HEAD_DIM)
    w_uk = w_ukv[:, :, :QK_NOPE_DIM].reshape(KV_LORA_RANK, MLA_HEADS * QK_NOPE_DIM).astype(BF16)
    w_uv_t = w_ukv[:, :, QK_NOPE_DIM:].reshape(KV_LORA_RANK, MLA_HEADS * V_HEAD_DIM).T.astype(BF16)

    cq, ckv, kr = _mla_in(h, attn_norm[0], w_in_pad, mla_q_norm[0], mla_kv_norm[0], tabs)
    q_t = _mla_q(cq, w_uq_pad_t, tabs_t, qk_dim ** -0.5 * LOG2_E, attn_tile)
    k, v_t = _mla_kv(ckv, w_uk, w_uv_t, kr, attn_tile)
    to_cast = (mla_w_o, sb_w_in, sb_w_o, ffn_w_gate_up, ffn_w_down)
    o, cast = _mla_attn(q_t, k, v_t, attn_tile, cast_weights=[w.reshape(-1, w.shape[2]) for w in to_cast])
    w_o, w_sb_in, w_sb_o, w_gate_up, w_down = [c.reshape(w.shape) for c, w in zip(cast, to_cast)]
    h = _matmul_residual(o, w_o, 0, h, tm=1024, tn=1024)
    h = _ffn(h, ffn_norm[0], w_gate_up, w_down, 0)

    n_q = SB_HEADS * SB_HEAD_DIM
    col_scale = jnp.concatenate([jnp.full((n_q,), SB_HEAD_DIM ** -0.5 * LOG2_E, F32), jnp.ones((2 * n_q,), F32)])
    qkv = _norm_matmul_scale(h, attn_norm[1], w_sb_in, 0, col_scale)
    o = _sb_attn(qkv)
    h = _matmul_residual(o, w_sb_o, 0, h, tm=1024, tn=1024)
    return _ffn(h, ffn_norm[1], w_gate_up, w_down, 1, out_norm_g=final_norm).reshape(b, s, d)
```

```python
import functools

import jax
import jax.numpy as jnp
from jax import lax
from jax.experimental import pallas as pl
from jax.experimental.pallas import tpu as pltpu

D_MODEL = 2048
MLA_HEADS = 16
Q_LORA_RANK = 512
KV_LORA_RANK = 512
QK_NOPE_DIM = 128
QK_ROPE_DIM = 64
V_HEAD_DIM = 128
ROPE_THETA = 10000.0
SB_HEADS = 16
SB_HEAD_DIM = 128
RMS_EPS = 1e-6

LANES = 128
QK_PAD_DIM = 2 * LANES
VMEM_LIMIT_BYTES = 56 * 1024 * 1024

LOG2_E = 1.4426950408889634
SB_LOG2_WEIGHT_FLOOR = -120.0 * LOG2_E

F32 = jnp.float32
BF16 = jnp.bfloat16


def _params(n_axes):
    return pltpu.CompilerParams(
        dimension_semantics=("arbitrary",) * n_axes, vmem_limit_bytes=VMEM_LIMIT_BYTES)


def _dot(a, b):
    return jnp.dot(a, b, preferred_element_type=F32)


def _dot_nt(a, b):
    return lax.dot_general(a, b, (((1,), (1,)), ((), ())), preferred_element_type=F32)


def _rms(x, g):
    return x * lax.rsqrt(jnp.mean(x * x, axis=-1, keepdims=True) + RMS_EPS) * g


def _rope(t, c, s1, s2):
    half = QK_ROPE_DIM // 2
    return t * c + pltpu.roll(t, half, 1) * s1 + pltpu.roll(t, LANES - half, 1) * s2


def _rope_tables_kernel(pos_ref, freq_ref, c_ref, s1_ref, s2_ref):
    half = QK_ROPE_DIM // 2
    ang = pos_ref[...].astype(F32) * freq_ref[...]
    lane = lax.broadcasted_iota(jnp.int32, ang.shape, 1)
    cos, sin = jnp.cos(ang), jnp.sin(ang)
    c_ref[...] = jnp.where(lane < 2 * half, cos, 0.0)
    s1_ref[...] = jnp.where((lane >= half) & (lane < 2 * half), sin, 0.0)
    s2_ref[...] = jnp.where(lane < half, -sin, 0.0)


def _rope_tables(positions, tm=2048):
    s = positions.shape[0]
    inv_freq = ROPE_THETA ** (-jnp.arange(0, QK_ROPE_DIM, 2, dtype=F32) / QK_ROPE_DIM)
    freq = jnp.concatenate([inv_freq, inv_freq, jnp.zeros((LANES - QK_ROPE_DIM,), F32)])[None, :]
    out = jax.ShapeDtypeStruct((s, LANES), F32)
    row = pl.BlockSpec((tm, LANES), lambda i: (i, 0))
    return pl.pallas_call(
        _rope_tables_kernel,
        grid=(s // tm,),
        in_specs=[pl.BlockSpec((tm, 1), lambda i: (i, 0)), pl.BlockSpec((1, LANES), lambda i: (0, 0))],
        out_specs=[row, row, row],
        out_shape=[out, out, out],
        compiler_params=_params(1),
        name="rope_tables",
    )(positions, freq)


def _rope_tables_t_kernel(pos_ref, freq_ref, c_ref, s1_ref, s2_ref):
    half = QK_ROPE_DIM // 2
    ang = freq_ref[...] * pos_ref[...].astype(F32)
    r = lax.broadcasted_iota(jnp.int32, ang.shape, 0)
    cos, sin = jnp.cos(ang), jnp.sin(ang)
    c_ref[...] = jnp.where(r < 2 * half, cos, 0.0)
    s1_ref[...] = jnp.where((r >= half) & (r < 2 * half), sin, 0.0)
    s2_ref[...] = jnp.where(r < half, -sin, 0.0)


def _rope_tables_t(positions, tm=2048):
    s = positions.shape[1]
    inv_freq = ROPE_THETA ** (-jnp.arange(0, QK_ROPE_DIM, 2, dtype=F32) / QK_ROPE_DIM)
    freq = jnp.concatenate([inv_freq, inv_freq, jnp.zeros((LANES - QK_ROPE_DIM,), F32)])[:, None]
    out = jax.ShapeDtypeStruct((LANES, s), F32)
    col = pl.BlockSpec((LANES, tm), lambda i: (0, i))
    return pl.pallas_call(
        _rope_tables_t_kernel,
        grid=(s // tm,),
        in_specs=[pl.BlockSpec((1, tm), lambda i: (0, i)), pl.BlockSpec((LANES, 1), lambda i: (0, 0))],
        out_specs=[col, col, col],
        out_shape=[out, out, out],
        compiler_params=_params(1),
        name="rope_tables_t",
    )(positions, freq)


def _mla_in_kernel(h_ref, g_ref, w_ref, qn_ref, kvn_ref, c_ref, s1_ref, s2_ref, cq_ref, ckv_ref, kr_ref):
    a = _rms(h_ref[...], g_ref[...]).astype(BF16)
    proj = _dot(a, w_ref[...])
    cq_ref[...] = _rms(proj[:, :Q_LORA_RANK], qn_ref[...]).astype(BF16)
    ckv_ref[...] = _rms(proj[:, Q_LORA_RANK:Q_LORA_RANK + KV_LORA_RANK], kvn_ref[...]).astype(BF16)
    kr = proj[:, Q_LORA_RANK + KV_LORA_RANK:]
    kr_ref[...] = _rope(kr, c_ref[...], s1_ref[...], s2_ref[...]).astype(BF16)


def _mla_in(h, norm_g, w_pad, q_norm, kv_norm, tabs, tm=1024):
    s, d = h.shape
    n = w_pad.shape[1]
    row = lambda w: pl.BlockSpec((tm, w), lambda i: (i, 0))
    const = lambda r, w: pl.BlockSpec((r, w), lambda i: (0, 0))
    return pl.pallas_call(
        _mla_in_kernel,
        grid=(s // tm,),
        in_specs=[row(d), const(1, d), const(d, n), const(1, Q_LORA_RANK), const(1, KV_LORA_RANK),
                  row(LANES), row(LANES), row(LANES)],
        out_specs=[row(Q_LORA_RANK), row(KV_LORA_RANK), row(LANES)],
        out_shape=[jax.ShapeDtypeStruct((s, Q_LORA_RANK), BF16),
                   jax.ShapeDtypeStruct((s, KV_LORA_RANK), BF16),
                   jax.ShapeDtypeStruct((s, LANES), BF16)],
        compiler_params=_params(1),
        name="mla_in",
    )(h, norm_g[None, :], w_pad, q_norm[None, :], kv_norm[None, :], *tabs)


def _mla_q_kernel(w_ref, cq_ref, c_ref, s1_ref, s2_ref, q_ref, *, heads, scale):
    acc = _dot_nt(w_ref[...], cq_ref[...])
    c, s1, s2 = c_ref[...], s1_ref[...], s2_ref[...]
    half = QK_ROPE_DIM // 2
    tq = q_ref.shape[3]
    for h in range(heads):
        lo = h * QK_PAD_DIM
        nope = (acc[lo:lo + LANES, :] * scale).astype(BF16)
        t = acc[lo + LANES:lo + QK_PAD_DIM, :]
        down = jnp.concatenate([t[LANES - half:], t[:LANES - half]], axis=0)
        up = jnp.concatenate([t[half:], t[:half]], axis=0)
        rope = ((t * c + down * s1 + up * s2) * scale).astype(BF16)
        for tt in range(acc.shape[1] // tq):
            q_ref[h, tt, :LANES, :] = nope[:, tt * tq:(tt + 1) * tq]
            q_ref[h, tt, LANES:, :] = rope[:, tt * tq:(tt + 1) * tq]


def _mla_q(cq, w_uq_pad_t, tabs_t, scale, t, tm=1024, heads_per_tile=8):
    s, r = cq.shape
    n = w_uq_pad_t.shape[0]
    tn = heads_per_tile * QK_PAD_DIM
    col = pl.BlockSpec((LANES, tm), lambda j, i: (0, i))
    return pl.pallas_call(
        functools.partial(_mla_q_kernel, heads=heads_per_tile, scale=scale),
        grid=(n // tn, s // tm),
        in_specs=[pl.BlockSpec((tn, r), lambda j, i: (j, 0)), pl.BlockSpec((tm, r), lambda j, i: (i, 0)),
                  col, col, col],
        out_specs=pl.BlockSpec((heads_per_tile, tm // t, QK_PAD_DIM, t), lambda j, i: (j, i, 0, 0)),
        out_shape=jax.ShapeDtypeStruct((n // QK_PAD_DIM, s // t, QK_PAD_DIM, t), BF16),
        compiler_params=_params(2),
        name="mla_q",
    )(w_uq_pad_t, cq, *tabs_t)


V_ROWS = V_HEAD_DIM + 16


def _mla_kv_kernel(ckv_ref, wk_ref, wv_ref, kr_ref, k_ref, v_ref, *, heads, t):
    ckv = ckv_ref[...]
    kn = _dot(ckv, wk_ref[...])
    vt = _dot_nt(wv_ref[...], ckv)
    kr = kr_ref[...]
    ones = jnp.ones((V_ROWS - V_HEAD_DIM, t), BF16)
    for h in range(heads):
        k_ref[:, h * QK_PAD_DIM:h * QK_PAD_DIM + LANES] = kn[:, h * LANES:(h + 1) * LANES].astype(BF16)
        k_ref[:, h * QK_PAD_DIM + LANES:(h + 1) * QK_PAD_DIM] = kr
        for tt in range(ckv.shape[0] // t):
            v_ref[h, tt, :V_HEAD_DIM, :] = vt[h * V_HEAD_DIM:(h + 1) * V_HEAD_DIM, tt * t:(tt + 1) * t].astype(BF16)
            v_ref[h, tt, V_HEAD_DIM:, :] = ones


def _mla_kv(ckv, w_uk, w_uv_t, kr, t, tm=1024, heads_per_tile=8):
    s, r = ckv.shape
    n = w_uk.shape[1]
    heads = n // LANES
    tn = heads_per_tile * LANES
    return pl.pallas_call(
        functools.partial(_mla_kv_kernel, heads=heads_per_tile, t=t),
        grid=(s // tm, n // tn),
        in_specs=[pl.BlockSpec((tm, r), lambda i, j: (i, 0)),
                  pl.BlockSpec((r, tn), lambda i, j: (0, j)),
                  pl.BlockSpec((tn, r), lambda i, j: (j, 0)),
                  pl.BlockSpec((tm, LANES), lambda i, j: (i, 0))],
        out_specs=[pl.BlockSpec((tm, 2 * tn), lambda i, j: (i, j)),
                   pl.BlockSpec((heads_per_tile, tm // t, V_ROWS, t), lambda i, j: (j, i, 0, 0))],
        out_shape=[jax.ShapeDtypeStruct((s, 2 * n), BF16),
                   jax.ShapeDtypeStruct((heads, s // t, V_ROWS, t), BF16)],
        compiler_params=_params(2),
        name="mla_kv",
    )(ckv, w_uk, w_uv_t, kr)


def _mla_attn_kernel(q_ref, k_ref, v_ref, *rest, t, unrolls, n_cast):
    w_refs, o_ref, w_bf16_refs = rest[:n_cast], rest[n_cast], rest[n_cast + 1:2 * n_cast + 1]
    m_sc, acc_sc, s_even, s_odd = rest[2 * n_cast + 1:]
    for w_ref, w_bf16_ref in zip(w_refs, w_bf16_refs):
        w_bf16_ref[...] = w_ref[...].astype(BF16)

    def q_tile(qi, carry):
        _mla_attn_q_tile(pl.program_id(1) * q_ref.shape[0] + qi, q_ref[qi], k_ref, v_ref,
                         o_ref.at[pl.ds(pl.multiple_of(qi * t, t), t), :],
                         m_sc, acc_sc, s_even, s_odd, t=t, unrolls=unrolls)
        return carry

    lax.fori_loop(0, q_ref.shape[0], q_tile, 0)


def _mla_attn_q_tile(i, q, k_ref, v_ref, o_ref, m_sc, acc_sc, s_even, s_odd, *, t, unrolls):
    m_sc[...] = jnp.full(m_sc.shape, -jnp.inf, F32)
    acc_sc[...] = jnp.zeros(acc_sc.shape, F32)

    def scores(j):
        return _dot(k_ref[pl.ds(pl.multiple_of(j * t, t), t), :], q)

    def softmax_pv(s_ref, j, diagonal):
        def load():
            s = s_ref[...]
            if diagonal:
                key = lax.broadcasted_iota(jnp.int32, (t, t), 0)
                query = lax.broadcasted_iota(jnp.int32, (t, t), 1)
                s = jnp.where(key <= query, s, -jnp.inf)
            return s
        m_prev = m_sc[...]
        m_next = jnp.maximum(m_prev, jnp.max(load(), axis=0, keepdims=True))
        m_sc[...] = m_next
        p = jnp.exp2(load() - m_sc[...]).astype(BF16)
        alpha = jnp.exp2(m_prev - m_next)
        acc_sc[...] = alpha * acc_sc[...] + _dot(v_ref[j], p)

    s_even[...] = scores(0)

    def pairs(first, n_pairs):
        for d in range(n_pairs):
            j = first + 2 * d
            s_odd[...] = scores(j + 1)
            softmax_pv(s_even, j, False)
            s_even[...] = scores(j + 2)
            softmax_pv(s_odd, j + 1, False)

    done = 0
    for u in unrolls:
        n_groups = (i // 2 - done) // u

        def group(g, carry, u=u, done=done):
            pairs(2 * (done + g * u), u)
            return carry

        lax.fori_loop(0, n_groups, group, 0)
        done = done + n_groups * u

    @pl.when(i % 2 == 0)
    def _():
        softmax_pv(s_even, i, True)

    @pl.when(i % 2 == 1)
    def _():
        s_odd[...] = scores(i)
        softmax_pv(s_even, i - 1, False)
        softmax_pv(s_odd, i, True)

    out_t = acc_sc[:V_HEAD_DIM, :] / acc_sc[V_HEAD_DIM:V_HEAD_DIM + 1, :]
    o_ref[...] = out_t.T.astype(o_ref.dtype)


BF16_SUBLANES = 16


def _mla_attn(q_t, k, v_t, t, cast_weights=(), q_tiles_per_step=8, unrolls=(8, 4, 2, 1)):
    s = k.shape[0]
    h = MLA_HEADS
    g = q_tiles_per_step
    n_i = s // (t * g)
    n_steps = h * n_i
    assert unrolls[-1] == 1
    cast_specs = []
    for w in cast_weights:
        rows, cols = w.shape
        share = next(sh for sh in (1, 2, 4, 8) if rows % (n_steps // sh * BF16_SUBLANES) == 0)
        cast_specs.append(pl.BlockSpec((rows // (n_steps // share), cols),
                                       lambda hh, i, share=share: ((hh * n_i + i) // share, 0)))
    outs = pl.pallas_call(
        functools.partial(_mla_attn_kernel, t=t, unrolls=unrolls, n_cast=len(cast_weights)),
        grid=(h, n_i),
        in_specs=[pl.BlockSpec((None, g, QK_PAD_DIM, t), lambda hh, i: (hh, i, 0, 0)),
                  pl.BlockSpec((s, QK_PAD_DIM), lambda hh, i: (0, hh)),
                  pl.BlockSpec((None, s // t, V_ROWS, t), lambda hh, i: (hh, 0, 0, 0))] + cast_specs,
        out_specs=[pl.BlockSpec((g * t, V_HEAD_DIM), lambda hh, i: (i, hh))] + cast_specs,
        out_shape=[jax.ShapeDtypeStruct((s, h * V_HEAD_DIM), BF16)]
        + [jax.ShapeDtypeStruct(w.shape, BF16) for w in cast_weights],
        scratch_shapes=[pltpu.VMEM((1, t), F32),
                        pltpu.VMEM((V_ROWS, t), F32),
                        pltpu.VMEM((t, t), F32), pltpu.VMEM((t, t), F32)],
        compiler_params=_params(2),
        name="mla_attn",
    )(q_t, k, v_t, *cast_weights)
    return outs[0], outs[1:]


def _sb_attn_kernel(q_ref, k_ref, v_ref, o_ref, tri_sc, *bufs, t):
    n_tiles = q_ref.shape[0] // t
    w = 2 * t
    z_sc, lom_sc, lb_sc, a_sc, wsum_sc, carry_sc, acc_sc = [bufs[2 * kind:2 * kind + 2] for kind in range(7)]
    tri_sc[...] = jnp.where(lax.broadcasted_iota(jnp.int32, (w, w), 0)
                            > lax.broadcasted_iota(jnp.int32, (w, w), 1), -1.0, 0.0).astype(BF16)

    def log_probs(z):
        neg_log_om = jnp.maximum(z, 0.0) + jnp.log2(1.0 + jnp.exp2(-jnp.abs(z)))
        return neg_log_om, z - neg_log_om

    def tile_rows(i):
        return pl.ds(pl.multiple_of(i * t, t), t)

    def window_start(i):
        return pl.multiple_of(jnp.maximum(i - 1, 0) * t, t)

    def window_scores(slot, i):
        i = jnp.minimum(i, n_tiles - 1)
        z_sc[slot][...] = _dot_nt(q_ref[tile_rows(i), :], k_ref[pl.ds(window_start(i), w), :])

    def window_log_probs(slot, first_tile=False):
        neg_log_om, log_beta = log_probs(z_sc[slot][...])
        lom, lb, wsum = lom_sc[slot], lb_sc[slot], wsum_sc[slot]
        strict = (lax.broadcasted_iota(jnp.int32, (t, t), 1)
                  < lax.broadcasted_iota(jnp.int32, (t, t), 0))
        if first_tile:
            halves = ((strict, neg_log_om[:, :t], log_beta[:, :t]), (None, None, None))
        else:
            halves = ((True, neg_log_om[:, :t], log_beta[:, :t]), (strict, neg_log_om[:, t:], log_beta[:, t:]))
        total = jnp.zeros((t, 1), F32)
        for half, (mask, nlo, lbeta) in enumerate(halves):
            cols = slice(half * t, (half + 1) * t)
            if mask is None:
                lom[:, cols] = jnp.zeros((t, t), BF16)
                lb[:, cols] = jnp.full((t, t), -jnp.inf, F32)
                continue
            if mask is not True:
                nlo = jnp.where(mask, nlo, 0.0)
                lbeta = jnp.where(mask, lbeta, -jnp.inf)
            lom[:, cols] = nlo.astype(BF16)
            lb[:, cols] = lbeta
            total = total + jnp.sum(nlo, axis=1, keepdims=True)
        wsum[...] = jnp.broadcast_to(-total, (t, LANES))

    def window_weights(slot):
        lom, lb, a = lom_sc[slot], lb_sc[slot], a_sc[slot]
        a[:, :t] = jnp.exp2(lb[:, :t] + _dot(lom[...], tri_sc[:, :t])).astype(BF16)
        a[:, t:] = jnp.exp2(lb[:, t:] + _dot(lom[:, t:], tri_sc[t:, t:])).astype(BF16)
        carry_sc[slot][...] = wsum_sc[slot][...]

    def window_values(slot, i):
        acc_sc[slot][...] = _dot(a_sc[slot][...], v_ref[pl.ds(window_start(i), w), :])

    def earlier_tiles(slot, i):
        carry, acc = carry_sc[slot], acc_sc[slot]
        q = q_ref[tile_rows(i), :]

        def more(state):
            j, carry_max = state
            return (j >= 0) & (carry_max > SB_LOG2_WEIGHT_FLOOR)

        def body(state):
            j, _ = state
            neg_log_om, log_beta = log_probs(_dot_nt(q, k_ref[tile_rows(j), :]))
            after = _dot(neg_log_om.astype(BF16), tri_sc[:t, :t]) + jnp.tile(carry[...], (1, t // LANES))
            a = jnp.exp2(log_beta + after)
            acc[...] += _dot(a.astype(BF16), v_ref[tile_rows(j), :])
            carry[...] -= jnp.sum(neg_log_om, axis=1, keepdims=True)
            return j - 1, jnp.max(carry[...])

        lax.while_loop(more, body, (i - 2, jnp.max(carry[...])))
        o_ref[tile_rows(i), :] = acc[...].astype(o_ref.dtype)

    def step(s, slot):
        window_scores(slot, s + 2)
        window_log_probs(1 - slot)
        window_weights(slot)
        window_values(1 - slot, s - 1)
        earlier_tiles(1 - slot, s - 1)

    def step_pair(p, _):
        step(2 * p + 1, 1)
        step(2 * p + 2, 0)
        return 0

    window_scores(0, 0)
    window_log_probs(0, first_tile=True)
    window_scores(1, 1)
    window_weights(0)
    window_log_probs(1)
    window_scores(0, 2)
    lax.fori_loop(0, n_tiles // 2, step_pair, 0)


def _sb_attn(qkv, t=256):
    s = qkv.shape[0]
    h, d = SB_HEADS, SB_HEAD_DIM
    assert s % (2 * t) == 0 and s >= 3 * t
    return pl.pallas_call(
        functools.partial(_sb_attn_kernel, t=t),
        grid=(h,),
        in_specs=[pl.BlockSpec((s, d), lambda hh: (0, hh)),
                  pl.BlockSpec((s, d), lambda hh: (0, h + hh)),
                  pl.BlockSpec((s, d), lambda hh: (0, 2 * h + hh))],
        out_specs=pl.BlockSpec((s, d), lambda hh: (0, hh)),
        out_shape=jax.ShapeDtypeStruct((s, h * d), BF16),
        scratch_shapes=[pltpu.VMEM((2 * t, 2 * t), BF16)]
        + 2 * [pltpu.VMEM((t, 2 * t), F32)]
        + 2 * [pltpu.VMEM((t, 2 * t), BF16)]
        + 2 * [pltpu.VMEM((t, 2 * t), F32)]
        + 2 * [pltpu.VMEM((t, 2 * t), BF16)]
        + 2 * [pltpu.VMEM((t, LANES), F32)]
        + 2 * [pltpu.VMEM((t, LANES), F32)]
        + 2 * [pltpu.VMEM((t, d), F32)],
        compiler_params=_params(1),
        name="sb_attn",
    )(qkv, qkv, qkv)


def _normalize_rows_once(h_ref, g_ref, a_sc):
    @pl.when(pl.program_id(1) == 0)
    def _():
        a_sc[...] = _rms(h_ref[...], g_ref[...]).astype(BF16)


def _norm_matmul_scale_kernel(h_ref, g_ref, w_ref, cs_ref, o_ref, a_sc):
    _normalize_rows_once(h_ref, g_ref, a_sc)
    o_ref[...] = (_dot(a_sc[...], w_ref[...].astype(BF16)) * cs_ref[...]).astype(o_ref.dtype)


def _norm_matmul_scale(h, norm_g, w, layer, col_scale, tm=1024, tn=1536):
    s, k = h.shape
    n = w.shape[2]
    return pl.pallas_call(
        _norm_matmul_scale_kernel,
        grid=(s // tm, n // tn),
        in_specs=[pl.BlockSpec((tm, k), lambda i, j: (i, 0)), pl.BlockSpec((1, k), lambda i, j: (0, 0)),
                  pl.BlockSpec((None, k, tn), lambda i, j: (layer, 0, j)),
                  pl.BlockSpec((1, tn), lambda i, j: (0, j))],
        out_specs=pl.BlockSpec((tm, tn), lambda i, j: (i, j)),
        out_shape=jax.ShapeDtypeStruct((s, n), BF16),
        scratch_shapes=[pltpu.VMEM((tm, k), BF16)],
        compiler_params=_params(2),
        name="norm_matmul_scale",
    )(h, norm_g[None, :], w, col_scale[None, :])


def _matmul_residual_kernel(a_ref, w_ref, r_ref, o_ref):
    o_ref[...] = r_ref[...] + _dot(a_ref[...], w_ref[...].astype(BF16))


def _matmul_residual(a, w, layer, res, tm, tn):
    s, k = a.shape
    n = w.shape[2]
    return pl.pallas_call(
        _matmul_residual_kernel,
        grid=(s // tm, n // tn),
        in_specs=[pl.BlockSpec((tm, k), lambda i, j: (i, 0)),
                  pl.BlockSpec((None, k, tn), lambda i, j: (layer, 0, j)),
                  pl.BlockSpec((tm, tn), lambda i, j: (i, j))],
        out_specs=pl.BlockSpec((tm, tn), lambda i, j: (i, j)),
        out_shape=jax.ShapeDtypeStruct((s, n), F32),
        compiler_params=_params(2),
        name="matmul_residual",
    )(a, w, res)


def _matmul_residual_norm_kernel(a_ref, w_ref, r_ref, g_ref, o_ref):
    kk = pl.program_id(1)

    @pl.when(kk == 0)
    def _():
        o_ref[...] = r_ref[...]

    o_ref[...] += _dot(a_ref[...], w_ref[...].astype(BF16))

    @pl.when(kk == pl.num_programs(1) - 1)
    def _():
        o_ref[...] = _rms(o_ref[...], g_ref[...])


def _matmul_residual_norm(a, w, layer, res, norm_g, tm=512, tk=2816):
    s, k = a.shape
    n = w.shape[2]
    return pl.pallas_call(
        _matmul_residual_norm_kernel,
        grid=(s // tm, k // tk),
        in_specs=[pl.BlockSpec((tm, tk), lambda i, kk: (i, kk)),
                  pl.BlockSpec((None, tk, n), lambda i, kk: (layer, kk, 0)),
                  pl.BlockSpec((tm, n), lambda i, kk: (i, 0)),
                  pl.BlockSpec((1, n), lambda i, kk: (0, 0))],
        out_specs=pl.BlockSpec((tm, n), lambda i, kk: (i, 0)),
        out_shape=jax.ShapeDtypeStruct((s, n), F32),
        compiler_params=_params(2),
        name="matmul_residual_norm",
    )(a, w, res, norm_g[None, :])


def _ffn_up_kernel(h_ref, g_ref, wg_ref, wu_ref, o_ref, a_sc):
    _normalize_rows_once(h_ref, g_ref, a_sc)
    a = a_sc[...]
    g = _dot(a, wg_ref[...].astype(BF16))
    u = _dot(a, wu_ref[...].astype(BF16))
    o_ref[...] = (g * jax.nn.sigmoid(g) * u).astype(o_ref.dtype)


def _ffn_up(h, norm_g, w_gate_up, layer, tm=1024, tn=512):
    s, k = h.shape
    d_ff = w_gate_up.shape[2] // 2
    nj = d_ff // tn
    return pl.pallas_call(
        _ffn_up_kernel,
        grid=(s // tm, nj),
        in_specs=[pl.BlockSpec((tm, k), lambda i, j: (i, 0)), pl.BlockSpec((1, k), lambda i, j: (0, 0)),
                  pl.BlockSpec((None, k, tn), lambda i, j: (layer, 0, j)),
                  pl.BlockSpec((None, k, tn), lambda i, j: (layer, 0, nj + j))],
        out_specs=pl.BlockSpec((tm, tn), lambda i, j: (i, j)),
        out_shape=jax.ShapeDtypeStruct((s, d_ff), BF16),
        scratch_shapes=[pltpu.VMEM((tm, k), BF16)],
        compiler_params=_params(2),
        name="ffn_up",
    )(h, norm_g[None, :], w_gate_up, w_gate_up)


def _ffn(h, norm_g, w_gate_up, w_down, layer, out_norm_g=None):
    act = _ffn_up(h, norm_g, w_gate_up, layer)
    if out_norm_g is not None:
        return _matmul_residual_norm(act, w_down, layer, h, out_norm_g)
    return _matmul_residual(act, w_down, layer, h, tm=1024, tn=512)


def kernel(x, positions, attn_norm, mla_w_in, mla_q_norm, mla_kv_norm, mla_w_uq, mla_w_ukv, mla_w_o,
           sb_w_in, sb_w_o, ffn_norm, ffn_w_gate_up, ffn_w_down, final_norm):
    b, s, d = x.shape
    assert b == 1 and d == D_MODEL
    h = x.reshape(s, d)

    attn_tile = 512
    tabs = _rope_tables(positions.reshape(s, 1))
    tabs_t = _rope_tables_t(positions.reshape(1, s))
    w_in_pad = jnp.pad(mla_w_in[0], ((0, 0), (0, LANES - QK_ROPE_DIM))).astype(BF16)
    qk_dim = QK_NOPE_DIM + QK_ROPE_DIM
    w_uq_pad = jnp.pad(mla_w_uq[0].reshape(Q_LORA_RANK, MLA_HEADS, qk_dim),
                       ((0, 0), (0, 0), (0, QK_PAD_DIM - qk_dim)))
    w_uq_pad_t = w_uq_pad.reshape(Q_LORA_RANK, MLA_HEADS * QK_PAD_DIM).T.astype(BF16)
    w_ukv = mla_w_ukv[0].reshape(KV_LORA_RANK, MLA_HEADS, QK_NOPE_DIM + V_HEAD_DIM)
    w_uk = w_ukv[:, :, :QK_NOPE_DIM].reshape(KV_LORA_RANK, MLA_HEADS * QK_NOPE_DIM).astype(BF16)
    w_uv_t = w_ukv[:, :, QK_NOPE_DIM:].reshape(KV_LORA_RANK, MLA_HEADS * V_HEAD_DIM).T.astype(BF16)

    cq, ckv, kr = _mla_in(h, attn_norm[0], w_in_pad, mla_q_norm[0], mla_kv_norm[0], tabs)
    q_t = _mla_q(cq, w_uq_pad_t, tabs_t, qk_dim ** -0.5 * LOG2_E, attn_tile)
    k, v_t = _mla_kv(ckv, w_uk, w_uv_t, kr, attn_tile)
    to_cast = (mla_w_o, sb_w_in, sb_w_o, ffn_w_gate_up, ffn_w_down)
    o, cast = _mla_attn(q_t, k, v_t, attn_tile, cast_weights=[w.reshape(-1, w.shape[2]) for w in to_cast])
    w_o, w_sb_in, w_sb_o, w_gate_up, w_down = [c.reshape(w.shape) for c, w in zip(cast, to_cast)]
    h = _matmul_residual(o, w_o, 0, h, tm=1024, tn=1024)
    h = _ffn(h, ffn_norm[0], w_gate_up, w_down, 0)

    n_q = SB_HEADS * SB_HEAD_DIM
    col_scale = jnp.concatenate([jnp.full((n_q,), SB_HEAD_DIM ** -0.5 * LOG2_E, F32), jnp.ones((2 * n_q,), F32)])
    qkv = _norm_matmul_scale(h, attn_norm[1], w_sb_in, 0, col_scale)
    o = _sb_attn(qkv)
    h = _matmul_residual(o, w_sb_o, 0, h, tm=1024, tn=1024)
    return _ffn(h, ffn_norm[1], w_gate_up, w_down, 1, out_norm_g=final_norm).reshape(b, s, d)
```

```python
import functools

import jax
import jax.numpy as jnp
from jax import lax
from jax.experimental import pallas as pl
from jax.experimental.pallas import tpu as pltpu

D_MODEL = 2048
MLA_HEADS = 16
Q_LORA_RANK = 512
KV_LORA_RANK = 512
QK_NOPE_DIM = 128
QK_ROPE_DIM = 64
V_HEAD_DIM = 128
ROPE_THETA = 10000.0
SB_HEADS = 16
SB_HEAD_DIM = 128
RMS_EPS = 1e-6

LANES = 128
QK_PAD_DIM = 2 * LANES
VMEM_LIMIT_BYTES = 56 * 1024 * 1024

LOG2_E = 1.4426950408889634
SB_LOG2_WEIGHT_FLOOR = -120.0 * LOG2_E

F32 = jnp.float32
BF16 = jnp.bfloat16


def _params(n_axes):
    return pltpu.CompilerParams(
        dimension_semantics=("arbitrary",) * n_axes, vmem_limit_bytes=VMEM_LIMIT_BYTES)


def _dot(a, b):
    return jnp.dot(a, b, preferred_element_type=F32)


def _dot_nt(a, b):
    return lax.dot_general(a, b, (((1,), (1,)), ((), ())), preferred_element_type=F32)


def _rms(x, g):
    return x * lax.rsqrt(jnp.mean(x * x, axis=-1, keepdims=True) + RMS_EPS) * g


def _rope(t, c, s1, s2):
    half = QK_ROPE_DIM // 2
    return t * c + pltpu.roll(t, half, 1) * s1 + pltpu.roll(t, LANES - half, 1) * s2


def _rope_tables_kernel(pos_ref, freq_ref, c_ref, s1_ref, s2_ref):
    half = QK_ROPE_DIM // 2
    ang = pos_ref[...].astype(F32) * freq_ref[...]
    lane = lax.broadcasted_iota(jnp.int32, ang.shape, 1)
    cos, sin = jnp.cos(ang), jnp.sin(ang)
    c_ref[...] = jnp.where(lane < 2 * half, cos, 0.0)
    s1_ref[...] = jnp.where((lane >= half) & (lane < 2 * half), sin, 0.0)
    s2_ref[...] = jnp.where(lane < half, -sin, 0.0)


def _rope_tables(positions, tm=2048):
    s = positions.shape[0]
    inv_freq = ROPE_THETA ** (-jnp.arange(0, QK_ROPE_DIM, 2, dtype=F32) / QK_ROPE_DIM)
    freq = jnp.concatenate([inv_freq, inv_freq, jnp.zeros((LANES - QK_ROPE_DIM,), F32)])[None, :]
    out = jax.ShapeDtypeStruct((s, LANES), F32)
    row = pl.BlockSpec((tm, LANES), lambda i: (i, 0))
    return pl.pallas_call(
        _rope_tables_kernel,
        grid=(s // tm,),
        in_specs=[pl.BlockSpec((tm, 1), lambda i: (i, 0)), pl.BlockSpec((1, LANES), lambda i: (0, 0))],
        out_specs=[row, row, row],
        out_shape=[out, out, out],
        compiler_params=_params(1),
        name="rope_tables",
    )(positions, freq)


def _rope_tables_t_kernel(pos_ref, freq_ref, c_ref, s1_ref, s2_ref):
    half = QK_ROPE_DIM // 2
    ang = freq_ref[...] * pos_ref[...].astype(F32)
    r = lax.broadcasted_iota(jnp.int32, ang.shape, 0)
    cos, sin = jnp.cos(ang), jnp.sin(ang)
    c_ref[...] = jnp.where(r < 2 * half, cos, 0.0)
    s1_ref[...] = jnp.where((r >= half) & (r < 2 * half), sin, 0.0)
    s2_ref[...] = jnp.where(r < half, -sin, 0.0)


def _rope_tables_t(positions, tm=2048):
    s = positions.shape[1]
    inv_freq = ROPE_THETA ** (-jnp.arange(0, QK_ROPE_DIM, 2, dtype=F32) / QK_ROPE_DIM)
    freq = jnp.concatenate([inv_freq, inv_freq, jnp.zeros((LANES - QK_ROPE_DIM,), F32)])[:, None]
    out = jax.ShapeDtypeStruct((LANES, s), F32)
    col = pl.BlockSpec((LANES, tm), lambda i: (0, i))
    return pl.pallas_call(
        _rope_tables_t_kernel,
        grid=(s // tm,),
        in_specs=[pl.BlockSpec((1, tm), lambda i: (0, i)), pl.BlockSpec((LANES, 1), lambda i: (0, 0))],
        out_specs=[col, col, col],
        out_shape=[out, out, out],
        compiler_params=_params(1),
        name="rope_tables_t",
    )(positions, freq)


def _mla_in_kernel(h_ref, g_ref, w_ref, qn_ref, kvn_ref, c_ref, s1_ref, s2_ref, cq_ref, ckv_ref, kr_ref):
    chunk = h_ref.shape[0] // 4
    for r in range(4):
        rows = slice(r * chunk, (r + 1) * chunk)
        a = _rms(h_ref[rows, :], g_ref[...]).astype(BF16)
        proj = _dot(a, w_ref[...])
        cq_ref[rows, :] = _rms(proj[:, :Q_LORA_RANK], qn_ref[...]).astype(BF16)
        ckv_ref[rows, :] = _rms(proj[:, Q_LORA_RANK:Q_LORA_RANK + KV_LORA_RANK], kvn_ref[...]).astype(BF16)
        kr = proj[:, Q_LORA_RANK + KV_LORA_RANK:]
        kr_ref[rows, :] = _rope(kr, c_ref[rows, :], s1_ref[rows, :], s2_ref[rows, :]).astype(BF16)


def _mla_in(h, norm_g, w_pad, q_norm, kv_norm, tabs, tm=1024):
    s, d = h.shape
    n = w_pad.shape[1]
    row = lambda w: pl.BlockSpec((tm, w), lambda i: (i, 0))
    const = lambda r, w: pl.BlockSpec((r, w), lambda i: (0, 0))
    return pl.pallas_call(
        _mla_in_kernel,
        grid=(s // tm,),
        in_specs=[row(d), const(1, d), const(d, n), const(1, Q_LORA_RANK), const(1, KV_LORA_RANK),
                  row(LANES), row(LANES), row(LANES)],
        out_specs=[row(Q_LORA_RANK), row(KV_LORA_RANK), row(LANES)],
        out_shape=[jax.ShapeDtypeStruct((s, Q_LORA_RANK), BF16),
                   jax.ShapeDtypeStruct((s, KV_LORA_RANK), BF16),
                   jax.ShapeDtypeStruct((s, LANES), BF16)],
        compiler_params=_params(1),
        name="mla_in",
    )(h, norm_g[None, :], w_pad, q_norm[None, :], kv_norm[None, :], *tabs)


def _mla_q_kernel(w_ref, cq_ref, c_ref, s1_ref, s2_ref, q_ref, *, heads, scale):
    c, s1, s2 = c_ref[...], s1_ref[...], s2_ref[...]
    cq = cq_ref[...]
    half = QK_ROPE_DIM // 2
    tq = q_ref.shape[3]
    for h in range(heads):
        acc = _dot_nt(w_ref[h * QK_PAD_DIM:(h + 1) * QK_PAD_DIM, :], cq)
        nope = (acc[:LANES, :] * scale).astype(BF16)
        t = acc[LANES:, :]
        down = jnp.concatenate([t[LANES - half:], t[:LANES - half]], axis=0)
        up = jnp.concatenate([t[half:], t[:half]], axis=0)
        rope = ((t * c + down * s1 + up * s2) * scale).astype(BF16)
        for tt in range(acc.shape[1] // tq):
            q_ref[h, tt, :LANES, :] = nope[:, tt * tq:(tt + 1) * tq]
            q_ref[h, tt, LANES:, :] = rope[:, tt * tq:(tt + 1) * tq]


def _mla_q(cq, w_uq_pad_t, tabs_t, scale, t, tm=1024, heads_per_tile=8):
    s, r = cq.shape
    n = w_uq_pad_t.shape[0]
    tn = heads_per_tile * QK_PAD_DIM
    col = pl.BlockSpec((LANES, tm), lambda j, i: (0, i))
    return pl.pallas_call(
        functools.partial(_mla_q_kernel, heads=heads_per_tile, scale=scale),
        grid=(n // tn, s // tm),
        in_specs=[pl.BlockSpec((tn, r), lambda j, i: (j, 0)), pl.BlockSpec((tm, r), lambda j, i: (i, 0)),
                  col, col, col],
        out_specs=pl.BlockSpec((heads_per_tile, tm // t, QK_PAD_DIM, t), lambda j, i: (j, i, 0, 0)),
        out_shape=jax.ShapeDtypeStruct((n // QK_PAD_DIM, s // t, QK_PAD_DIM, t), BF16),
        compiler_params=_params(2),
        name="mla_q",
    )(w_uq_pad_t, cq, *tabs_t)


V_ROWS = V_HEAD_DIM + 16


def _mla_kv_kernel(ckv_ref, wk_ref, wv_ref, kr_ref, k_ref, v_ref, *, heads, t):
    ckv = ckv_ref[...]
    kn = _dot(ckv, wk_ref[...])
    vt = _dot_nt(wv_ref[...], ckv)
    kr = kr_ref[...]
    ones = jnp.ones((V_ROWS - V_HEAD_DIM, t), BF16)
    for h in range(heads):
        k_ref[:, h * QK_PAD_DIM:h * QK_PAD_DIM + LANES] = kn[:, h * LANES:(h + 1) * LANES].astype(BF16)
        k_ref[:, h * QK_PAD_DIM + LANES:(h + 1) * QK_PAD_DIM] = kr
        for tt in range(ckv.shape[0] // t):
            v_ref[h, tt, :V_HEAD_DIM, :] = vt[h * V_HEAD_DIM:(h + 1) * V_HEAD_DIM, tt * t:(tt + 1) * t].astype(BF16)
            v_ref[h, tt, V_HEAD_DIM:, :] = ones


def _mla_kv(ckv, w_uk, w_uv_t, kr, t, tm=1024, heads_per_tile=8):
    s, r = ckv.shape
    n = w_uk.shape[1]
    heads = n // LANES
    tn = heads_per_tile * LANES
    return pl.pallas_call(
        functools.partial(_mla_kv_kernel, heads=heads_per_tile, t=t),
        grid=(s // tm, n // tn),
        in_specs=[pl.BlockSpec((tm, r), lambda i, j: (i, 0)),
                  pl.BlockSpec((r, tn), lambda i, j: (0, j)),
                  pl.BlockSpec((tn, r), lambda i, j: (j, 0)),
                  pl.BlockSpec((tm, LANES), lambda i, j: (i, 0))],
        out_specs=[pl.BlockSpec((tm, 2 * tn), lambda i, j: (i, j)),
                   pl.BlockSpec((heads_per_tile, tm // t, V_ROWS, t), lambda i, j: (j, i, 0, 0))],
        out_shape=[jax.ShapeDtypeStruct((s, 2 * n), BF16),
                   jax.ShapeDtypeStruct((heads, s // t, V_ROWS, t), BF16)],
        compiler_params=_params(2),
        name="mla_kv",
    )(ckv, w_uk, w_uv_t, kr)


def _mla_attn_kernel(q_ref, k_ref, v_ref, *rest, t, unrolls, n_cast):
    w_refs, o_ref, w_bf16_refs = rest[:n_cast], rest[n_cast], rest[n_cast + 1:2 * n_cast + 1]
    m_sc, acc_sc, s_even, s_odd = rest[2 * n_cast + 1:]
    for w_ref, w_bf16_ref in zip(w_refs, w_bf16_refs):
        w_bf16_ref[...] = w_ref[...].astype(BF16)

    def q_tile(qi, carry):
        _mla_attn_q_tile(pl.program_id(1) * q_ref.shape[0] + qi, q_ref[qi], k_ref, v_ref,
                         o_ref.at[pl.ds(pl.multiple_of(qi * t, t), t), :],
                         m_sc, acc_sc, s_even, s_odd, t=t, unrolls=unrolls)
        return carry

    lax.fori_loop(0, q_ref.shape[0], q_tile, 0)


def _mla_attn_q_tile(i, q, k_ref, v_ref, o_ref, m_sc, acc_sc, s_even, s_odd, *, t, unrolls):
    m_sc[...] = jnp.full(m_sc.shape, -jnp.inf, F32)
    acc_sc[...] = jnp.zeros(acc_sc.shape, F32)

    def scores(j):
        return _dot(k_ref[pl.ds(pl.multiple_of(j * t, t), t), :], q)

    def softmax_pv(s_ref, j, diagonal):
        def load():
            s = s_ref[...]
            if diagonal:
                key = lax.broadcasted_iota(jnp.int32, (t, t), 0)
                query = lax.broadcasted_iota(jnp.int32, (t, t), 1)
                s = jnp.where(key <= query, s, -jnp.inf)
            return s
        m_prev = m_sc[...]
        m_next = jnp.maximum(m_prev, jnp.max(load(), axis=0, keepdims=True))
        m_sc[...] = m_next
        p = jnp.exp2(load() - m_sc[...]).astype(BF16)
        alpha = jnp.exp2(m_prev - m_next)
        acc_sc[...] = alpha * acc_sc[...] + _dot(v_ref[j], p)

    s_even[...] = scores(0)

    def pairs(first, n_pairs):
        for d in range(n_pairs):
            j = first + 2 * d
            s_odd[...] = scores(j + 1)
            softmax_pv(s_even, j, False)
            s_even[...] = scores(j + 2)
            softmax_pv(s_odd, j + 1, False)

    done = 0
    for u in unrolls:
        n_groups = (i // 2 - done) // u

        def group(g, carry, u=u, done=done):
            pairs(2 * (done + g * u), u)
            return carry

        lax.fori_loop(0, n_groups, group, 0)
        done = done + n_groups * u

    @pl.when(i % 2 == 0)
    def _():
        softmax_pv(s_even, i, True)

    @pl.when(i % 2 == 1)
    def _():
        s_odd[...] = scores(i)
        softmax_pv(s_even, i - 1, False)
        softmax_pv(s_odd, i, True)

    out_t = acc_sc[:V_HEAD_DIM, :] / acc_sc[V_HEAD_DIM:V_HEAD_DIM + 1, :]
    o_ref[...] = out_t.T.astype(o_ref.dtype)


BF16_SUBLANES = 16


def _mla_attn(q_t, k, v_t, t, cast_weights=(), q_tiles_per_step=4, unrolls=(8, 4, 2, 1)):
    s = k.shape[0]
    h = MLA_HEADS
    g = q_tiles_per_step
    n_i = s // (t * g)
    n_steps = h * n_i
    assert unrolls[-1] == 1
    cast_specs = []
    for w in cast_weights:
        rows, cols = w.shape
        share = next(sh for sh in (1, 2, 4, 8) if rows % (n_steps // sh * BF16_SUBLANES) == 0)
        cast_specs.append(pl.BlockSpec((rows // (n_steps // share), cols),
                                       lambda hh, i, share=share: ((hh * n_i + i) // share, 0)))
    outs = pl.pallas_call(
        functools.partial(_mla_attn_kernel, t=t, unrolls=unrolls, n_cast=len(cast_weights)),
        grid=(h, n_i),
        in_specs=[pl.BlockSpec((None, g, QK_PAD_DIM, t), lambda hh, i: (hh, i, 0, 0)),
                  pl.BlockSpec((s, QK_PAD_DIM), lambda hh, i: (0, hh)),
                  pl.BlockSpec((None, s // t, V_ROWS, t), lambda hh, i: (hh, 0, 0, 0))] + cast_specs,
        out_specs=[pl.BlockSpec((g * t, V_HEAD_DIM), lambda hh, i: (i, hh))] + cast_specs,
        out_shape=[jax.ShapeDtypeStruct((s, h * V_HEAD_DIM), BF16)]
        + [jax.ShapeDtypeStruct(w.shape, BF16) for w in cast_weights],
        scratch_shapes=[pltpu.VMEM((1, t), F32),
                        pltpu.VMEM((V_ROWS, t), F32),
                        pltpu.VMEM((t, t), F32), pltpu.VMEM((t, t), F32)],
        compiler_params=_params(2),
        name="mla_attn",
    )(q_t, k, v_t, *cast_weights)
    return outs[0], outs[1:]


def _sb_attn_kernel(q_ref, k_ref, v_ref, o_ref, tri_sc, *bufs, t):
    n_tiles = q_ref.shape[0] // t
    w = 2 * t
    z_sc, lom_sc, lb_sc, a_sc, wsum_sc, carry_sc, acc_sc = [bufs[2 * kind:2 * kind + 2] for kind in range(7)]
    tri_sc[...] = jnp.where(lax.broadcasted_iota(jnp.int32, (w, w), 0)
                            > lax.broadcasted_iota(jnp.int32, (w, w), 1), -1.0, 0.0).astype(BF16)

    def log_probs(z):
        neg_log_om = jnp.maximum(z, 0.0) + jnp.log2(1.0 + jnp.exp2(-jnp.abs(z)))
        return neg_log_om, z - neg_log_om

    def tile_rows(i):
        return pl.ds(pl.multiple_of(i * t, t), t)

    def window_start(i):
        return pl.multiple_of(jnp.maximum(i - 1, 0) * t, t)

    def window_scores(slot, i):
        i = jnp.minimum(i, n_tiles - 1)
        z_sc[slot][...] = _dot_nt(q_ref[tile_rows(i), :], k_ref[pl.ds(window_start(i), w), :])

    def window_log_probs(slot, first_tile=False):
        neg_log_om, log_beta = log_probs(z_sc[slot][...])
        lom, lb, wsum = lom_sc[slot], lb_sc[slot], wsum_sc[slot]
        strict = (lax.broadcasted_iota(jnp.int32, (t, t), 1)
                  < lax.broadcasted_iota(jnp.int32, (t, t), 0))
        if first_tile:
            halves = ((strict, neg_log_om[:, :t], log_beta[:, :t]), (None, None, None))
        else:
            halves = ((True, neg_log_om[:, :t], log_beta[:, :t]), (strict, neg_log_om[:, t:], log_beta[:, t:]))
        total = jnp.zeros((t, 1), F32)
        for half, (mask, nlo, lbeta) in enumerate(halves):
            cols = slice(half * t, (half + 1) * t)
            if mask is None:
                lom[:, cols] = jnp.zeros((t, t), BF16)
                lb[:, cols] = jnp.full((t, t), -jnp.inf, F32)
                continue
            if mask is not True:
                nlo = jnp.where(mask, nlo, 0.0)
                lbeta = jnp.where(mask, lbeta, -jnp.inf)
            lom[:, cols] = nlo.astype(BF16)
            lb[:, cols] = lbeta
            total = total + jnp.sum(nlo, axis=1, keepdims=True)
        wsum[...] = jnp.broadcast_to(-total, (t, LANES))

    def window_weights(slot):
        lom, lb, a = lom_sc[slot], lb_sc[slot], a_sc[slot]
        a[:, :t] = jnp.exp2(lb[:, :t] + _dot(lom[...], tri_sc[:, :t])).astype(BF16)
        a[:, t:] = jnp.exp2(lb[:, t:] + _dot(lom[:, t:], tri_sc[t:, t:])).astype(BF16)
        carry_sc[slot][...] = wsum_sc[slot][...]

    def window_values(slot, i):
        acc_sc[slot][...] = _dot(a_sc[slot][...], v_ref[pl.ds(window_start(i), w), :])

    def earlier_tiles(slot, i):
        carry, acc = carry_sc[slot], acc_sc[slot]
        q = q_ref[tile_rows(i), :]

        def more(state):
            j, carry_max = state
            return (j >= 0) & (carry_max > SB_LOG2_WEIGHT_FLOOR)

        def body(state):
            j, _ = state
            neg_log_om, log_beta = log_probs(_dot_nt(q, k_ref[tile_rows(j), :]))
            after = _dot(neg_log_om.astype(BF16), tri_sc[:t, :t]) + jnp.tile(carry[...], (1, t // LANES))
            a = jnp.exp2(log_beta + after)
            acc[...] += _dot(a.astype(BF16), v_ref[tile_rows(j), :])
            carry[...] -= jnp.sum(neg_log_om, axis=1, keepdims=True)
            return j - 1, jnp.max(carry[...])

        lax.while_loop(more, body, (i - 2, jnp.max(carry[...])))
        o_ref[tile_rows(i), :] = acc[...].astype(o_ref.dtype)

    def step(s, slot):
        window_scores(slot, s + 2)
        window_log_probs(1 - slot)
        window_weights(slot)
        window_values(1 - slot, s - 1)
        earlier_tiles(1 - slot, s - 1)

    def step_pair(p, _):
        step(2 * p + 1, 1)
        step(2 * p + 2, 0)
        return 0

    window_scores(0, 0)
    window_log_probs(0, first_tile=True)
    window_scores(1, 1)
    window_weights(0)
    window_log_probs(1)
    window_scores(0, 2)
    lax.fori_loop(0, n_tiles // 2, step_pair, 0)


def _sb_attn(qkv, t=256):
    s = qkv.shape[0]
    h, d = SB_HEADS, SB_HEAD_DIM
    assert s % (2 * t) == 0 and s >= 3 * t
    return pl.pallas_call(
        functools.partial(_sb_attn_kernel, t=t),
        grid=(h,),
        in_specs=[pl.BlockSpec((s, d), lambda hh: (0, hh)),
                  pl.BlockSpec((s, d), lambda hh: (0, h + hh)),
                  pl.BlockSpec((s, d), lambda hh: (0, 2 * h + hh))],
        out_specs=pl.BlockSpec((s, d), lambda hh: (0, hh)),
        out_shape=jax.ShapeDtypeStruct((s, h * d), BF16),
        scratch_shapes=[pltpu.VMEM((2 * t, 2 * t), BF16)]
        + 2 * [pltpu.VMEM((t, 2 * t), F32)]
        + 2 * [pltpu.VMEM((t, 2 * t), BF16)]
        + 2 * [pltpu.VMEM((t, 2 * t), F32)]
        + 2 * [pltpu.VMEM((t, 2 * t), BF16)]
        + 2 * [pltpu.VMEM((t, LANES), F32)]
        + 2 * [pltpu.VMEM((t, LANES), F32)]
        + 2 * [pltpu.VMEM((t, d), F32)],
        compiler_params=_params(1),
        name="sb_attn",
    )(qkv, qkv, qkv)


def _normalize_rows_once(h_ref, g_ref, a_sc):
    @pl.when(pl.program_id(1) == 0)
    def _():
        a_sc[...] = _rms(h_ref[...], g_ref[...]).astype(BF16)


def _norm_matmul_scale_kernel(h_ref, g_ref, w_ref, cs_ref, o_ref, a_sc):
    _normalize_rows_once(h_ref, g_ref, a_sc)
    o_ref[...] = (_dot(a_sc[...], w_ref[...].astype(BF16)) * cs_ref[...]).astype(o_ref.dtype)


def _norm_matmul_scale(h, norm_g, w, layer, col_scale, tm=1024, tn=1536):
    s, k = h.shape
    n = w.shape[2]
    return pl.pallas_call(
        _norm_matmul_scale_kernel,
        grid=(s // tm, n // tn),
        in_specs=[pl.BlockSpec((tm, k), lambda i, j: (i, 0)), pl.BlockSpec((1, k), lambda i, j: (0, 0)),
                  pl.BlockSpec((None, k, tn), lambda i, j: (layer, 0, j)),
                  pl.BlockSpec((1, tn), lambda i, j: (0, j))],
        out_specs=pl.BlockSpec((tm, tn), lambda i, j: (i, j)),
        out_shape=jax.ShapeDtypeStruct((s, n), BF16),
        scratch_shapes=[pltpu.VMEM((tm, k), BF16)],
        compiler_params=_params(2),
        name="norm_matmul_scale",
    )(h, norm_g[None, :], w, col_scale[None, :])


def _matmul_residual_kernel(a_ref, w_ref, r_ref, o_ref):
    o_ref[...] = r_ref[...] + _dot(a_ref[...], w_ref[...].astype(BF16))


def _matmul_residual(a, w, layer, res, tm, tn):
    s, k = a.shape
    n = w.shape[2]
    return pl.pallas_call(
        _matmul_residual_kernel,
        grid=(s // tm, n // tn),
        in_specs=[pl.BlockSpec((tm, k), lambda i, j: (i, 0)),
                  pl.BlockSpec((None, k, tn), lambda i, j: (layer, 0, j)),
                  pl.BlockSpec((tm, tn), lambda i, j: (i, j))],
        out_specs=pl.BlockSpec((tm, tn), lambda i, j: (i, j)),
        out_shape=jax.ShapeDtypeStruct((s, n), F32),
        compiler_params=_params(2),
        name="matmul_residual",
    )(a, w, res)


def _matmul_residual_norm_kernel(a_ref, w_ref, r_ref, g_ref, o_ref):
    kk = pl.program_id(1)

    @pl.when(kk == 0)
    def _():
        o_ref[...] = r_ref[...]

    o_ref[...] += _dot(a_ref[...], w_ref[...].astype(BF16))

    @pl.when(kk == pl.num_programs(1) - 1)
    def _():
        o_ref[...] = _rms(o_ref[...], g_ref[...])


def _matmul_residual_norm(a, w, layer, res, norm_g, tm=512, tk=2816):
    s, k = a.shape
    n = w.shape[2]
    return pl.pallas_call(
        _matmul_residual_norm_kernel,
        grid=(s // tm, k // tk),
        in_specs=[pl.BlockSpec((tm, tk), lambda i, kk: (i, kk)),
                  pl.BlockSpec((None, tk, n), lambda i, kk: (layer, kk, 0)),
                  pl.BlockSpec((tm, n), lambda i, kk: (i, 0)),
                  pl.BlockSpec((1, n), lambda i, kk: (0, 0))],
        out_specs=pl.BlockSpec((tm, n), lambda i, kk: (i, 0)),
        out_shape=jax.ShapeDtypeStruct((s, n), F32),
        compiler_params=_params(2),
        name="matmul_residual_norm",
    )(a, w, res, norm_g[None, :])


def _ffn_up_kernel(h_ref, g_ref, wg_ref, wu_ref, o_ref, a_sc):
    _normalize_rows_once(h_ref, g_ref, a_sc)
    a = a_sc[...]
    g = _dot(a, wg_ref[...].astype(BF16))
    u = _dot(a, wu_ref[...].astype(BF16))
    o_ref[...] = (g * jax.nn.sigmoid(g) * u).astype(o_ref.dtype)


def _ffn_up(h, norm_g, w_gate_up, layer, tm=1024, tn=512):
    s, k = h.shape
    d_ff = w_gate_up.shape[2] // 2
    nj = d_ff // tn
    return pl.pallas_call(
        _ffn_up_kernel,
        grid=(s // tm, nj),
        in_specs=[pl.BlockSpec((tm, k), lambda i, j: (i, 0)), pl.BlockSpec((1, k), lambda i, j: (0, 0)),
                  pl.BlockSpec((None, k, tn), lambda i, j: (layer, 0, j)),
                  pl.BlockSpec((None, k, tn), lambda i, j: (layer, 0, nj + j))],
        out_specs=pl.BlockSpec((tm, tn), lambda i, j: (i, j)),
        out_shape=jax.ShapeDtypeStruct((s, d_ff), BF16),
        scratch_shapes=[pltpu.VMEM((tm, k), BF16)],
        compiler_params=_params(2),
        name="ffn_up",
    )(h, norm_g[None, :], w_gate_up, w_gate_up)


def _ffn(h, norm_g, w_gate_up, w_down, layer, out_norm_g=None):
    act = _ffn_up(h, norm_g, w_gate_up, layer)
    if out_norm_g is not None:
        return _matmul_residual_norm(act, w_down, layer, h, out_norm_g)
    return _matmul_residual(act, w_down, layer, h, tm=1024, tn=512)


def kernel(x, positions, attn_norm, mla_w_in, mla_q_norm, mla_kv_norm, mla_w_uq, mla_w_ukv, mla_w_o,
           sb_w_in, sb_w_o, ffn_norm, ffn_w_gate_up, ffn_w_down, final_norm):
    b, s, d = x.shape
    assert b == 1 and d == D_MODEL
    h = x.reshape(s, d)

    attn_tile = 512
    tabs = _rope_tables(positions.reshape(s, 1))
    tabs_t = _rope_tables_t(positions.reshape(1, s))
    w_in_pad = jnp.pad(mla_w_in[0], ((0, 0), (0, LANES - QK_ROPE_DIM))).astype(BF16)
    qk_dim = QK_NOPE_DIM + QK_ROPE_DIM
    w_uq_pad = jnp.pad(mla_w_uq[0].reshape(Q_LORA_RANK, MLA_HEADS, qk_dim),
                       ((0, 0), (0, 0), (0, QK_PAD_DIM - qk_dim)))
    w_uq_pad_t = w_uq_pad.reshape(Q_LORA_RANK, MLA_HEADS * QK_PAD_DIM).T.astype(BF16)
    w_ukv = mla_w_ukv[0].reshape(KV_LORA_RANK, MLA_HEADS, QK_NOPE_DIM + V_HEAD_DIM)
    w_uk = w_ukv[:, :, :QK_NOPE_DIM].reshape(KV_LORA_RANK, MLA_HEADS * QK_NOPE_DIM).astype(BF16)
    w_uv_t = w_ukv[:, :, QK_NOPE_DIM:].reshape(KV_LORA_RANK, MLA_HEADS * V_HEAD_DIM).T.astype(BF16)

    cq, ckv, kr = _mla_in(h, attn_norm[0], w_in_pad, mla_q_norm[0], mla_kv_norm[0], tabs)
    q_t = _mla_q(cq, w_uq_pad_t, tabs_t, qk_dim ** -0.5 * LOG2_E, attn_tile)
    k, v_t = _mla_kv(ckv, w_uk, w_uv_t, kr, attn_tile)
    to_cast = (mla_w_o, sb_w_in, sb_w_o, ffn_w_gate_up, ffn_w_down)
    o, cast = _mla_attn(q_t, k, v_t, attn_tile, cast_weights=[w.reshape(-1, w.shape[2]) for w in to_cast])
    w_o, w_sb_in, w_sb_o, w_gate_up, w_down = [c.reshape(w.shape) for c, w in zip(cast, to_cast)]
    h = _matmul_residual(o, w_o, 0, h, tm=1024, tn=1024)
    h = _ffn(h, ffn_norm[0], w_gate_up, w_down, 0)

    n_q = SB_HEADS * SB_HEAD_DIM
    col_scale = jnp.concatenate([jnp.full((n_q,), SB_HEAD_DIM ** -0.5 * LOG2_E, F32), jnp.ones((2 * n_q,), F32)])
    qkv = _norm_matmul_scale(h, attn_norm[1], w_sb_in, 0, col_scale)
    o = _sb_attn(qkv)
    h = _matmul_residual(o, w_sb_o, 0, h, tm=1024, tn=1024)
    return _ffn(h, ffn_norm[1], w_gate_up, w_down, 1, out_norm_g=final_norm).reshape(b, s, d)
```

```python
import functools

import jax
import jax.numpy as jnp
from jax import lax
from jax.experimental import pallas as pl
from jax.experimental.pallas import tpu as pltpu

D_MODEL = 2048
MLA_HEADS = 16
Q_LORA_RANK = 512
KV_LORA_RANK = 512
QK_NOPE_DIM = 128
QK_ROPE_DIM = 64
V_HEAD_DIM = 128
ROPE_THETA = 10000.0
SB_HEADS = 16
SB_HEAD_DIM = 128
RMS_EPS = 1e-6

LANES = 128
QK_PAD_DIM = 2 * LANES
VMEM_LIMIT_BYTES = 56 * 1024 * 1024

LOG2_E = 1.4426950408889634
SB_LOG2_WEIGHT_FLOOR = -120.0 * LOG2_E

F32 = jnp.float32
BF16 = jnp.bfloat16


def _params(n_axes):
    return pltpu.CompilerParams(
        dimension_semantics=("arbitrary",) * n_axes, vmem_limit_bytes=VMEM_LIMIT_BYTES)


def _dot(a, b):
    return jnp.dot(a, b, preferred_element_type=F32)


def _dot_nt(a, b):
    return lax.dot_general(a, b, (((1,), (1,)), ((), ())), preferred_element_type=F32)


def _rms(x, g):
    return x * lax.rsqrt(jnp.mean(x * x, axis=-1, keepdims=True) + RMS_EPS) * g


def _rope(t, c, s1, s2):
    half = QK_ROPE_DIM // 2
    return t * c + pltpu.roll(t, half, 1) * s1 + pltpu.roll(t, LANES - half, 1) * s2


def _rope_tables_kernel(pos_ref, freq_ref, c_ref, s1_ref, s2_ref):
    half = QK_ROPE_DIM // 2
    ang = pos_ref[...].astype(F32) * freq_ref[...]
    lane = lax.broadcasted_iota(jnp.int32, ang.shape, 1)
    cos, sin = jnp.cos(ang), jnp.sin(ang)
    c_ref[...] = jnp.where(lane < 2 * half, cos, 0.0)
    s1_ref[...] = jnp.where((lane >= half) & (lane < 2 * half), sin, 0.0)
    s2_ref[...] = jnp.where(lane < half, -sin, 0.0)


def _rope_tables(positions, tm=2048):
    s = positions.shape[0]
    inv_freq = ROPE_THETA ** (-jnp.arange(0, QK_ROPE_DIM, 2, dtype=F32) / QK_ROPE_DIM)
    freq = jnp.concatenate([inv_freq, inv_freq, jnp.zeros((LANES - QK_ROPE_DIM,), F32)])[None, :]
    out = jax.ShapeDtypeStruct((s, LANES), F32)
    row = pl.BlockSpec((tm, LANES), lambda i: (i, 0))
    return pl.pallas_call(
        _rope_tables_kernel,
        grid=(s // tm,),
        in_specs=[pl.BlockSpec((tm, 1), lambda i: (i, 0)), pl.BlockSpec((1, LANES), lambda i: (0, 0))],
        out_specs=[row, row, row],
        out_shape=[out, out, out],
        compiler_params=_params(1),
        name="rope_tables",
    )(positions, freq)


def _rope_tables_t_kernel(pos_ref, freq_ref, c_ref, s1_ref, s2_ref):
    half = QK_ROPE_DIM // 2
    ang = freq_ref[...] * pos_ref[...].astype(F32)
    r = lax.broadcasted_iota(jnp.int32, ang.shape, 0)
    cos, sin = jnp.cos(ang), jnp.sin(ang)
    c_ref[...] = jnp.where(r < 2 * half, cos, 0.0)
    s1_ref[...] = jnp.where((r >= half) & (r < 2 * half), sin, 0.0)
    s2_ref[...] = jnp.where(r < half, -sin, 0.0)


def _rope_tables_t(positions, tm=2048):
    s = positions.shape[1]
    inv_freq = ROPE_THETA ** (-jnp.arange(0, QK_ROPE_DIM, 2, dtype=F32) / QK_ROPE_DIM)
    freq = jnp.concatenate([inv_freq, inv_freq, jnp.zeros((LANES - QK_ROPE_DIM,), F32)])[:, None]
    out = jax.ShapeDtypeStruct((LANES, s), F32)
    col = pl.BlockSpec((LANES, tm), lambda i: (0, i))
    return pl.pallas_call(
        _rope_tables_t_kernel,
        grid=(s // tm,),
        in_specs=[pl.BlockSpec((1, tm), lambda i: (0, i)), pl.BlockSpec((LANES, 1), lambda i: (0, 0))],
        out_specs=[col, col, col],
        out_shape=[out, out, out],
        compiler_params=_params(1),
        name="rope_tables_t",
    )(positions, freq)


def _mla_in_kernel(h_ref, g_ref, w_ref, qn_ref, kvn_ref, c_ref, s1_ref, s2_ref, cq_ref, ckv_ref, kr_ref):
    chunk = h_ref.shape[0] // 4
    for r in range(4):
        rows = slice(r * chunk, (r + 1) * chunk)
        a = _rms(h_ref[rows, :], g_ref[...]).astype(BF16)
        proj = _dot(a, w_ref[...])
        cq_ref[rows, :] = _rms(proj[:, :Q_LORA_RANK], qn_ref[...]).astype(BF16)
        ckv_ref[rows, :] = _rms(proj[:, Q_LORA_RANK:Q_LORA_RANK + KV_LORA_RANK], kvn_ref[...]).astype(BF16)
        kr = proj[:, Q_LORA_RANK + KV_LORA_RANK:]
        kr_ref[rows, :] = _rope(kr, c_ref[rows, :], s1_ref[rows, :], s2_ref[rows, :]).astype(BF16)


def _mla_in(h, norm_g, w_pad, q_norm, kv_norm, tabs, tm=1024):
    s, d = h.shape
    n = w_pad.shape[1]
    row = lambda w: pl.BlockSpec((tm, w), lambda i: (i, 0))
    const = lambda r, w: pl.BlockSpec((r, w), lambda i: (0, 0))
    return pl.pallas_call(
        _mla_in_kernel,
        grid=(s // tm,),
        in_specs=[row(d), const(1, d), const(d, n), const(1, Q_LORA_RANK), const(1, KV_LORA_RANK),
                  row(LANES), row(LANES), row(LANES)],
        out_specs=[row(Q_LORA_RANK), row(KV_LORA_RANK), row(LANES)],
        out_shape=[jax.ShapeDtypeStruct((s, Q_LORA_RANK), BF16),
                   jax.ShapeDtypeStruct((s, KV_LORA_RANK), BF16),
                   jax.ShapeDtypeStruct((s, LANES), BF16)],
        compiler_params=_params(1),
        name="mla_in",
    )(h, norm_g[None, :], w_pad, q_norm[None, :], kv_norm[None, :], *tabs)


def _mla_q_kernel(w_ref, cq_ref, c_ref, s1_ref, s2_ref, q_ref, *, heads, scale):
    c, s1, s2 = c_ref[...], s1_ref[...], s2_ref[...]
    cq = cq_ref[...]
    half = QK_ROPE_DIM // 2
    tq = q_ref.shape[3]
    for h in range(heads):
        acc = _dot_nt(w_ref[h * QK_PAD_DIM:(h + 1) * QK_PAD_DIM, :], cq)
        nope = (acc[:LANES, :] * scale).astype(BF16)
        t = acc[LANES:, :]
        down = jnp.concatenate([t[LANES - half:], t[:LANES - half]], axis=0)
        up = jnp.concatenate([t[half:], t[:half]], axis=0)
        rope = ((t * c + down * s1 + up * s2) * scale).astype(BF16)
        for tt in range(acc.shape[1] // tq):
            q_ref[h, tt, :LANES, :] = nope[:, tt * tq:(tt + 1) * tq]
            q_ref[h, tt, LANES:, :] = rope[:, tt * tq:(tt + 1) * tq]


def _mla_q(cq, w_uq_pad_t, tabs_t, scale, t, tm=1024, heads_per_tile=8):
    s, r = cq.shape
    n = w_uq_pad_t.shape[0]
    tn = heads_per_tile * QK_PAD_DIM
    col = pl.BlockSpec((LANES, tm), lambda j, i: (0, i))
    return pl.pallas_call(
        functools.partial(_mla_q_kernel, heads=heads_per_tile, scale=scale),
        grid=(n // tn, s // tm),
        in_specs=[pl.BlockSpec((tn, r), lambda j, i: (j, 0)), pl.BlockSpec((tm, r), lambda j, i: (i, 0)),
                  col, col, col],
        out_specs=pl.BlockSpec((heads_per_tile, tm // t, QK_PAD_DIM, t), lambda j, i: (j, i, 0, 0)),
        out_shape=jax.ShapeDtypeStruct((n // QK_PAD_DIM, s // t, QK_PAD_DIM, t), BF16),
        compiler_params=_params(2),
        name="mla_q",
    )(w_uq_pad_t, cq, *tabs_t)


V_ROWS = V_HEAD_DIM + 16


def _mla_kv_kernel(ckv_ref, wk_ref, wv_ref, kr_ref, k_ref, v_ref, *, heads, t):
    ckv = ckv_ref[...]
    kn = _dot(ckv, wk_ref[...])
    vt = _dot_nt(wv_ref[...], ckv)
    kr = kr_ref[...]
    ones = jnp.ones((V_ROWS - V_HEAD_DIM, t), BF16)
    for h in range(heads):
        k_ref[:, h * QK_PAD_DIM:h * QK_PAD_DIM + LANES] = kn[:, h * LANES:(h + 1) * LANES].astype(BF16)
        k_ref[:, h * QK_PAD_DIM + LANES:(h + 1) * QK_PAD_DIM] = kr
        for tt in range(ckv.shape[0] // t):
            v_ref[h, tt, :V_HEAD_DIM, :] = vt[h * V_HEAD_DIM:(h + 1) * V_HEAD_DIM, tt * t:(tt + 1) * t].astype(BF16)
            v_ref[h, tt, V_HEAD_DIM:, :] = ones


def _mla_kv(ckv, w_uk, w_uv_t, kr, t, tm=1024, heads_per_tile=8):
    s, r = ckv.shape
    n = w_uk.shape[1]
    heads = n // LANES
    tn = heads_per_tile * LANES
    return pl.pallas_call(
        functools.partial(_mla_kv_kernel, heads=heads_per_tile, t=t),
        grid=(s // tm, n // tn),
        in_specs=[pl.BlockSpec((tm, r), lambda i, j: (i, 0)),
                  pl.BlockSpec((r, tn), lambda i, j: (0, j)),
                  pl.BlockSpec((tn, r), lambda i, j: (j, 0)),
                  pl.BlockSpec((tm, LANES), lambda i, j: (i, 0))],
        out_specs=[pl.BlockSpec((tm, 2 * tn), lambda i, j: (i, j)),
                   pl.BlockSpec((heads_per_tile, tm // t, V_ROWS, t), lambda i, j: (j, i, 0, 0))],
        out_shape=[jax.ShapeDtypeStruct((s, 2 * n), BF16),
                   jax.ShapeDtypeStruct((heads, s // t, V_ROWS, t), BF16)],
        compiler_params=_params(2),
        name="mla_kv",
    )(ckv, w_uk, w_uv_t, kr)


def _mla_attn_kernel(q_ref, k_ref, v_ref, *rest, t, unrolls, n_cast):
    w_refs, o_ref, w_bf16_refs = rest[:n_cast], rest[n_cast], rest[n_cast + 1:2 * n_cast + 1]
    m_sc, acc_sc, s_even, s_odd = rest[2 * n_cast + 1:]
    for w_ref, w_bf16_ref in zip(w_refs, w_bf16_refs):
        w_bf16_ref[...] = w_ref[...].astype(BF16)

    def q_tile(qi, carry):
        _mla_attn_q_tile(pl.program_id(1) * q_ref.shape[0] + qi, q_ref[qi], k_ref, v_ref,
                         o_ref.at[pl.ds(pl.multiple_of(qi * t, t), t), :],
                         m_sc, acc_sc, s_even, s_odd, t=t, unrolls=unrolls)
        return carry

    lax.fori_loop(0, q_ref.shape[0], q_tile, 0)


def _mla_attn_q_tile(i, q, k_ref, v_ref, o_ref, m_sc, acc_sc, s_even, s_odd, *, t, unrolls):
    m_sc[...] = jnp.full(m_sc.shape, -jnp.inf, F32)
    acc_sc[...] = jnp.zeros(acc_sc.shape, F32)

    def scores(j):
        return _dot(k_ref[pl.ds(pl.multiple_of(j * t, t), t), :], q)

    def softmax_pv(s_ref, j, diagonal):
        def load():
            s = s_ref[...]
            if diagonal:
                key = lax.broadcasted_iota(jnp.int32, (t, t), 0)
                query = lax.broadcasted_iota(jnp.int32, (t, t), 1)
                s = jnp.where(key <= query, s, -jnp.inf)
            return s
        m_prev = m_sc[...]
        m_next = jnp.maximum(m_prev, jnp.max(load(), axis=0, keepdims=True))
        m_sc[...] = m_next
        p = jnp.exp2(load() - m_sc[...]).astype(BF16)
        alpha = jnp.exp2(m_prev - m_next)
        acc_sc[...] = alpha * acc_sc[...] + _dot(v_ref[j], p)

    s_even[...] = scores(0)

    def pairs(first, n_pairs):
        for d in range(n_pairs):
            j = first + 2 * d
            s_odd[...] = scores(j + 1)
            softmax_pv(s_even, j, False)
            s_even[...] = scores(j + 2)
            softmax_pv(s_odd, j + 1, False)

    done = 0
    for u in unrolls:
        n_groups = (i // 2 - done) // u

        def group(g, carry, u=u, done=done):
            pairs(2 * (done + g * u), u)
            return carry

        lax.fori_loop(0, n_groups, group, 0)
        done = done + n_groups * u

    @pl.when(i % 2 == 0)
    def _():
        softmax_pv(s_even, i, True)

    @pl.when(i % 2 == 1)
    def _():
        s_odd[...] = scores(i)
        softmax_pv(s_even, i - 1, False)
        softmax_pv(s_odd, i, True)

    out_t = acc_sc[:V_HEAD_DIM, :] / acc_sc[V_HEAD_DIM:V_HEAD_DIM + 1, :]
    o_ref[...] = out_t.T.astype(o_ref.dtype)


BF16_SUBLANES = 16


def _mla_attn(q_t, k, v_t, t, cast_weights=(), q_tiles_per_step=4, unrolls=(8, 4, 2, 1)):
    s = k.shape[0]
    h = MLA_HEADS
    g = q_tiles_per_step
    n_i = s // (t * g)
    n_steps = h * n_i
    assert unrolls[-1] == 1
    cast_specs = []
    for w in cast_weights:
        rows, cols = w.shape
        share = next(sh for sh in (1, 2, 4, 8) if rows % (n_steps // sh * BF16_SUBLANES) == 0)
        cast_specs.append(pl.BlockSpec((rows // (n_steps // share), cols),
                                       lambda hh, i, share=share: ((hh * n_i + i) // share, 0)))
    outs = pl.pallas_call(
        functools.partial(_mla_attn_kernel, t=t, unrolls=unrolls, n_cast=len(cast_weights)),
        grid=(h, n_i),
        in_specs=[pl.BlockSpec((None, g, QK_PAD_DIM, t), lambda hh, i: (hh, i, 0, 0)),
                  pl.BlockSpec((s, QK_PAD_DIM), lambda hh, i: (0, hh)),
                  pl.BlockSpec((None, s // t, V_ROWS, t), lambda hh, i: (hh, 0, 0, 0))] + cast_specs,
        out_specs=[pl.BlockSpec((g * t, V_HEAD_DIM), lambda hh, i: (i, hh))] + cast_specs,
        out_shape=[jax.ShapeDtypeStruct((s, h * V_HEAD_DIM), BF16)]
        + [jax.ShapeDtypeStruct(w.shape, BF16) for w in cast_weights],
        scratch_shapes=[pltpu.VMEM((1, t), F32),
                        pltpu.VMEM((V_ROWS, t), F32),
                        pltpu.VMEM((t, t), F32), pltpu.VMEM((t, t), F32)],
        compiler_params=_params(2),
        name="mla_attn",
    )(q_t, k, v_t, *cast_weights)
    return outs[0], outs[1:]


def _sb_attn_kernel(q_ref, k_ref, v_ref, o_ref, tri_sc, *bufs, t):
    n_tiles = q_ref.shape[0] // t
    w = 2 * t
    z_sc, lom_sc, lb_sc, a_sc, wsum_sc, carry_sc, acc_sc = [bufs[2 * kind:2 * kind + 2] for kind in range(7)]
    tri_sc[...] = jnp.where(lax.broadcasted_iota(jnp.int32, (w, w), 0)
                            > lax.broadcasted_iota(jnp.int32, (w, w), 1), -1.0, 0.0).astype(BF16)

    def log_probs(z):
        neg_log_om = jnp.maximum(z, 0.0) + jnp.log2(1.0 + jnp.exp2(-jnp.abs(z)))
        return neg_log_om, z - neg_log_om

    def tile_rows(i):
        return pl.ds(pl.multiple_of(i * t, t), t)

    def window_start(i):
        return pl.multiple_of(jnp.maximum(i - 1, 0) * t, t)

    def window_scores(slot, i):
        i = jnp.minimum(i, n_tiles - 1)
        z_sc[slot][...] = _dot_nt(q_ref[tile_rows(i), :], k_ref[pl.ds(window_start(i), w), :])

    def window_log_probs(slot, first_tile=False):
        neg_log_om, log_beta = log_probs(z_sc[slot][...])
        lom, lb, wsum = lom_sc[slot], lb_sc[slot], wsum_sc[slot]
        strict = (lax.broadcasted_iota(jnp.int32, (t, t), 1)
                  < lax.broadcasted_iota(jnp.int32, (t, t), 0))
        if first_tile:
            halves = ((strict, neg_log_om[:, :t], log_beta[:, :t]), (None, None, None))
        else:
            halves = ((True, neg_log_om[:, :t], log_beta[:, :t]), (strict, neg_log_om[:, t:], log_beta[:, t:]))
        total = jnp.zeros((t, 1), F32)
        for half, (mask, nlo, lbeta) in enumerate(halves):
            cols = slice(half * t, (half + 1) * t)
            if mask is None:
                lom[:, cols] = jnp.zeros((t, t), BF16)
                lb[:, cols] = jnp.full((t, t), -jnp.inf, F32)
                continue
            if mask is not True:
                nlo = jnp.where(mask, nlo, 0.0)
                lbeta = jnp.where(mask, lbeta, -jnp.inf)
            lom[:, cols] = nlo.astype(BF16)
            lb[:, cols] = lbeta
            total = total + jnp.sum(nlo, axis=1, keepdims=True)
        wsum[...] = jnp.broadcast_to(-total, (t, LANES))

    def window_weights(slot):
        lom, lb, a = lom_sc[slot], lb_sc[slot], a_sc[slot]
        a[:, :t] = jnp.exp2(lb[:, :t] + _dot(lom[...], tri_sc[:, :t])).astype(BF16)
        a[:, t:] = jnp.exp2(lb[:, t:] + _dot(lom[:, t:], tri_sc[t:, t:])).astype(BF16)
        carry_sc[slot][...] = wsum_sc[slot][...]

    def window_values(slot, i):
        acc_sc[slot][...] = _dot(a_sc[slot][...], v_ref[pl.ds(window_start(i), w), :])

    def earlier_tiles(slot, i):
        carry, acc = carry_sc[slot], acc_sc[slot]
        q = q_ref[tile_rows(i), :]

        def more(state):
            j, carry_max = state
            return (j >= 0) & (carry_max > SB_LOG2_WEIGHT_FLOOR)

        def body(state):
            j, _ = state
            neg_log_om, log_beta = log_probs(_dot_nt(q, k_ref[tile_rows(j), :]))
            after = _dot(neg_log_om.astype(BF16), tri_sc[:t, :t]) + jnp.tile(carry[...], (1, t // LANES))
            a = jnp.exp2(log_beta + after)
            acc[...] += _dot(a.astype(BF16), v_ref[tile_rows(j), :])
            carry[...] -= jnp.sum(neg_log_om, axis=1, keepdims=True)
            return j - 1, jnp.max(carry[...])

        lax.while_loop(more, body, (i - 2, jnp.max(carry[...])))
        o_ref[tile_rows(i), :] = acc[...].astype(o_ref.dtype)

    def step(s, slot):
        window_scores(slot, s + 2)
        window_log_probs(1 - slot)
        window_weights(slot)
        window_values(1 - slot, s - 1)
        earlier_tiles(1 - slot, s - 1)

    def step_pair(p, _):
        step(2 * p + 1, 1)
        step(2 * p + 2, 0)
        return 0

    window_scores(0, 0)
    window_log_probs(0, first_tile=True)
    window_scores(1, 1)
    window_weights(0)
    window_log_probs(1)
    window_scores(0, 2)
    lax.fori_loop(0, n_tiles // 2, step_pair, 0)


def _sb_attn(qkv, t=256):
    s = qkv.shape[0]
    h, d = SB_HEADS, SB_HEAD_DIM
    assert s % (2 * t) == 0 and s >= 3 * t
    return pl.pallas_call(
        functools.partial(_sb_attn_kernel, t=t),
        grid=(h,),
        in_specs=[pl.BlockSpec((s, d), lambda hh: (0, hh)),
                  pl.BlockSpec((s, d), lambda hh: (0, h + hh)),
                  pl.BlockSpec((s, d), lambda hh: (0, 2 * h + hh))],
        out_specs=pl.BlockSpec((s, d), lambda hh: (0, hh)),
        out_shape=jax.ShapeDtypeStruct((s, h * d), BF16),
        scratch_shapes=[pltpu.VMEM((2 * t, 2 * t), BF16)]
        + 2 * [pltpu.VMEM((t, 2 * t), F32)]
        + 2 * [pltpu.VMEM((t, 2 * t), BF16)]
        + 2 * [pltpu.VMEM((t, 2 * t), F32)]
        + 2 * [pltpu.VMEM((t, 2 * t), BF16)]
        + 2 * [pltpu.VMEM((t, LANES), F32)]
        + 2 * [pltpu.VMEM((t, LANES), F32)]
        + 2 * [pltpu.VMEM((t, d), F32)],
        compiler_params=_params(1),
        name="sb_attn",
    )(qkv, qkv, qkv)


def _normalize_rows_once(h_ref, g_ref, a_sc):
    @pl.when(pl.program_id(1) == 0)
    def _():
        a_sc[...] = _rms(h_ref[...], g_ref[...]).astype(BF16)


def _norm_matmul_scale_kernel(h_ref, g_ref, w_ref, cs_ref, o_ref, a_sc):
    _normalize_rows_once(h_ref, g_ref, a_sc)
    o_ref[...] = (_dot(a_sc[...], w_ref[...].astype(BF16)) * cs_ref[...]).astype(o_ref.dtype)


def _norm_matmul_scale(h, norm_g, w, layer, col_scale, tm=1024, tn=1536):
    s, k = h.shape
    n = w.shape[2]
    return pl.pallas_call(
        _norm_matmul_scale_kernel,
        grid=(s // tm, n // tn),
        in_specs=[pl.BlockSpec((tm, k), lambda i, j: (i, 0)), pl.BlockSpec((1, k), lambda i, j: (0, 0)),
                  pl.BlockSpec((None, k, tn), lambda i, j: (layer, 0, j)),
                  pl.BlockSpec((1, tn), lambda i, j: (0, j))],
        out_specs=pl.BlockSpec((tm, tn), lambda i, j: (i, j)),
        out_shape=jax.ShapeDtypeStruct((s, n), BF16),
        scratch_shapes=[pltpu.VMEM((tm, k), BF16)],
        compiler_params=_params(2),
        name="norm_matmul_scale",
    )(h, norm_g[None, :], w, col_scale[None, :])


def _matmul_residual_kernel(a_ref, w_ref, r_ref, o_ref):
    o_ref[...] = r_ref[...] + _dot(a_ref[...], w_ref[...].astype(BF16))


def _matmul_residual(a, w, layer, res, tm, tn):
    s, k = a.shape
    n = w.shape[2]
    return pl.pallas_call(
        _matmul_residual_kernel,
        grid=(s // tm, n // tn),
        in_specs=[pl.BlockSpec((tm, k), lambda i, j: (i, 0)),
                  pl.BlockSpec((None, k, tn), lambda i, j: (layer, 0, j)),
                  pl.BlockSpec((tm, tn), lambda i, j: (i, j))],
        out_specs=pl.BlockSpec((tm, tn), lambda i, j: (i, j)),
        out_shape=jax.ShapeDtypeStruct((s, n), F32),
        compiler_params=_params(2),
        name="matmul_residual",
    )(a, w, res)


def _matmul_residual_norm_kernel(a_ref, w_ref, r_ref, g_ref, o_ref):
    kk = pl.program_id(1)
    chunk = o_ref.shape[0] // 4

    def partial_sums(base_ref):
        for r in range(4):
            rows = slice(r * chunk, (r + 1) * chunk)
            yield rows, base_ref[rows, :] + _dot(a_ref[rows, :], w_ref[...].astype(BF16))

    @pl.when(kk == 0)
    def _():
        for rows, x in partial_sums(r_ref):
            o_ref[rows, :] = x

    @pl.when(kk == 1)
    def _():
        for rows, x in partial_sums(o_ref):
            o_ref[rows, :] = _rms(x, g_ref[...])


def _matmul_residual_norm(a, w, layer, res, norm_g, tm=512, tk=2816):
    s, k = a.shape
    n = w.shape[2]
    assert k == 2 * tk
    return pl.pallas_call(
        _matmul_residual_norm_kernel,
        grid=(s // tm, k // tk),
        in_specs=[pl.BlockSpec((tm, tk), lambda i, kk: (i, kk)),
                  pl.BlockSpec((None, tk, n), lambda i, kk: (layer, kk, 0)),
                  pl.BlockSpec((tm, n), lambda i, kk: (i, 0)),
                  pl.BlockSpec((1, n), lambda i, kk: (0, 0))],
        out_specs=pl.BlockSpec((tm, n), lambda i, kk: (i, 0)),
        out_shape=jax.ShapeDtypeStruct((s, n), F32),
        compiler_params=_params(2),
        name="matmul_residual_norm",
    )(a, w, res, norm_g[None, :])


def _ffn_up_kernel(h_ref, g_ref, wg_ref, wu_ref, o_ref, a_sc):
    _normalize_rows_once(h_ref, g_ref, a_sc)
    a = a_sc[...]
    g = _dot(a, wg_ref[...].astype(BF16))
    u = _dot(a, wu_ref[...].astype(BF16))
    o_ref[...] = (g * jax.nn.sigmoid(g) * u).astype(o_ref.dtype)


def _ffn_up(h, norm_g, w_gate_up, layer, tm=1024, tn=512):
    s, k = h.shape
    d_ff = w_gate_up.shape[2] // 2
    nj = d_ff // tn
    return pl.pallas_call(
        _ffn_up_kernel,
        grid=(s // tm, nj),
        in_specs=[pl.BlockSpec((tm, k), lambda i, j: (i, 0)), pl.BlockSpec((1, k), lambda i, j: (0, 0)),
                  pl.BlockSpec((None, k, tn), lambda i, j: (layer, 0, j)),
                  pl.BlockSpec((None, k, tn), lambda i, j: (layer, 0, nj + j))],
        out_specs=pl.BlockSpec((tm, tn), lambda i, j: (i, j)),
        out_shape=jax.ShapeDtypeStruct((s, d_ff), BF16),
        scratch_shapes=[pltpu.VMEM((tm, k), BF16)],
        compiler_params=_params(2),
        name="ffn_up",
    )(h, norm_g[None, :], w_gate_up, w_gate_up)


def _ffn(h, norm_g, w_gate_up, w_down, layer, out_norm_g=None):
    act = _ffn_up(h, norm_g, w_gate_up, layer)
    if out_norm_g is not None:
        return _matmul_residual_norm(act, w_down, layer, h, out_norm_g)
    return _matmul_residual(act, w_down, layer, h, tm=1024, tn=512)


def kernel(x, positions, attn_norm, mla_w_in, mla_q_norm, mla_kv_norm, mla_w_uq, mla_w_ukv, mla_w_o,
           sb_w_in, sb_w_o, ffn_norm, ffn_w_gate_up, ffn_w_down, final_norm):
    b, s, d = x.shape
    assert b == 1 and d == D_MODEL
    h = x.reshape(s, d)

    attn_tile = 512
    tabs = _rope_tables(positions.reshape(s, 1))
    tabs_t = _rope_tables_t(positions.reshape(1, s))
    w_in_pad = jnp.pad(mla_w_in[0], ((0, 0), (0, LANES - QK_ROPE_DIM))).astype(BF16)
    qk_dim = QK_NOPE_DIM + QK_ROPE_DIM
    w_uq_pad = jnp.pad(mla_w_uq[0].reshape(Q_LORA_RANK, MLA_HEADS, qk_dim),
                       ((0, 0), (0, 0), (0, QK_PAD_DIM - qk_dim)))
    w_uq_pad_t = w_uq_pad.reshape(Q_LORA_RANK, MLA_HEADS * QK_PAD_DIM).T.astype(BF16)
    w_ukv = mla_w_ukv[0].reshape(KV_LORA_RANK, MLA_HEADS, QK_NOPE_DIM + V_HEAD_DIM)
    w_uk = w_ukv[:, :, :QK_NOPE_DIM].reshape(KV_LORA_RANK, MLA_HEADS * QK_NOPE_DIM).astype(BF16)
    w_uv_t = w_ukv[:, :, QK_NOPE_DIM:].reshape(KV_LORA_RANK, MLA_HEADS * V_HEAD_DIM).T.astype(BF16)

    cq, ckv, kr = _mla_in(h, attn_norm[0], w_in_pad, mla_q_norm[0], mla_kv_norm[0], tabs)
    q_t = _mla_q(cq, w_uq_pad_t, tabs_t, qk_dim ** -0.5 * LOG2_E, attn_tile)
    k, v_t = _mla_kv(ckv, w_uk, w_uv_t, kr, attn_tile)
    to_cast = (mla_w_o, sb_w_in, sb_w_o, ffn_w_gate_up, ffn_w_down)
    o, cast = _mla_attn(q_t, k, v_t, attn_tile, cast_weights=[w.reshape(-1, w.shape[2]) for w in to_cast])
    w_o, w_sb_in, w_sb_o, w_gate_up, w_down = [c.reshape(w.shape) for c, w in zip(cast, to_cast)]
    h = _matmul_residual(o, w_o, 0, h, tm=1024, tn=1024)
    h = _ffn(h, ffn_norm[0], w_gate_up, w_down, 0)

    n_q = SB_HEADS * SB_HEAD_DIM
    col_scale = jnp.concatenate([jnp.full((n_q,), SB_HEAD_DIM ** -0.5 * LOG2_E, F32), jnp.ones((2 * n_q,), F32)])
    qkv = _norm_matmul_scale(h, attn_norm[1], w_sb_in, 0, col_scale)
    o = _sb_attn(qkv)
    h = _matmul_residual(o, w_sb_o, 0, h, tm=1024, tn=1024)
    return _ffn(h, ffn_norm[1], w_gate_up, w_down, 1, out_norm_g=final_norm).reshape(b, s, d)
```

```python
import functools

import jax
import jax.numpy as jnp
from jax import lax
from jax.experimental import pallas as pl
from jax.experimental.pallas import tpu as pltpu

D_MODEL = 2048
MLA_HEADS = 16
Q_LORA_RANK = 512
KV_LORA_RANK = 512
QK_NOPE_DIM = 128
QK_ROPE_DIM = 64
V_HEAD_DIM = 128
ROPE_THETA = 10000.0
SB_HEADS = 16
SB_HEAD_DIM = 128
RMS_EPS = 1e-6

LANES = 128
QK_PAD_DIM = 2 * LANES
VMEM_LIMIT_BYTES = 56 * 1024 * 1024

LOG2_E = 1.4426950408889634
SB_LOG2_WEIGHT_FLOOR = -120.0 * LOG2_E

F32 = jnp.float32
BF16 = jnp.bfloat16


def _params(n_axes):
    return pltpu.CompilerParams(
        dimension_semantics=("arbitrary",) * n_axes, vmem_limit_bytes=VMEM_LIMIT_BYTES)


def _dot(a, b):
    return jnp.dot(a, b, preferred_element_type=F32)


def _dot_nt(a, b):
    return lax.dot_general(a, b, (((1,), (1,)), ((), ())), preferred_element_type=F32)


def _rms(x, g):
    return x * lax.rsqrt(jnp.mean(x * x, axis=-1, keepdims=True) + RMS_EPS) * g


def _rope(t, c, s1, s2):
    half = QK_ROPE_DIM // 2
    return t * c + pltpu.roll(t, half, 1) * s1 + pltpu.roll(t, LANES - half, 1) * s2


def _rope_tables_t_kernel(pos_ref, freq_ref, c_ref, s1_ref, s2_ref):
    half = QK_ROPE_DIM // 2
    ang = freq_ref[...] * pos_ref[...].astype(F32)
    r = lax.broadcasted_iota(jnp.int32, ang.shape, 0)
    cos, sin = jnp.cos(ang), jnp.sin(ang)
    c_ref[...] = jnp.where(r < 2 * half, cos, 0.0)
    s1_ref[...] = jnp.where((r >= half) & (r < 2 * half), sin, 0.0)
    s2_ref[...] = jnp.where(r < half, -sin, 0.0)


def _rope_tables_t(positions, tm=2048):
    s = positions.shape[1]
    inv_freq = ROPE_THETA ** (-jnp.arange(0, QK_ROPE_DIM, 2, dtype=F32) / QK_ROPE_DIM)
    freq = jnp.concatenate([inv_freq, inv_freq, jnp.zeros((LANES - QK_ROPE_DIM,), F32)])[:, None]
    out = jax.ShapeDtypeStruct((LANES, s), F32)
    col = pl.BlockSpec((LANES, tm), lambda i: (0, i))
    return pl.pallas_call(
        _rope_tables_t_kernel,
        grid=(s // tm,),
        in_specs=[pl.BlockSpec((1, tm), lambda i: (0, i)), pl.BlockSpec((LANES, 1), lambda i: (0, 0))],
        out_specs=[col, col, col],
        out_shape=[out, out, out],
        compiler_params=_params(1),
        name="rope_tables_t",
    )(positions, freq)


def _mla_in_kernel(h_ref, g_ref, w_ref, qn_ref, kvn_ref, pos_ref, freq_ref, cq_ref, ckv_ref, kr_ref):
    chunk = h_ref.shape[0] // 4
    half = QK_ROPE_DIM // 2
    lane = lax.broadcasted_iota(jnp.int32, (chunk, LANES), 1)
    for r in range(4):
        rows = slice(r * chunk, (r + 1) * chunk)
        a = _rms(h_ref[rows, :], g_ref[...]).astype(BF16)
        proj = _dot(a, w_ref[...])
        cq_ref[rows, :] = _rms(proj[:, :Q_LORA_RANK], qn_ref[...]).astype(BF16)
        ckv_ref[rows, :] = _rms(proj[:, Q_LORA_RANK:Q_LORA_RANK + KV_LORA_RANK], kvn_ref[...]).astype(BF16)
        kr = proj[:, Q_LORA_RANK + KV_LORA_RANK:]
        ang = pos_ref[rows, :].astype(F32) * freq_ref[...]
        cos, sin = jnp.cos(ang), jnp.sin(ang)
        c = jnp.where(lane < 2 * half, cos, 0.0)
        s1 = jnp.where((lane >= half) & (lane < 2 * half), sin, 0.0)
        s2 = jnp.where(lane < half, -sin, 0.0)
        kr_ref[rows, :] = _rope(kr, c, s1, s2).astype(BF16)


def _mla_in(h, norm_g, w_pad, q_norm, kv_norm, positions, tm=1024):
    s, d = h.shape
    n = w_pad.shape[1]
    inv_freq = ROPE_THETA ** (-jnp.arange(0, QK_ROPE_DIM, 2, dtype=F32) / QK_ROPE_DIM)
    freq = jnp.concatenate([inv_freq, inv_freq, jnp.zeros((LANES - QK_ROPE_DIM,), F32)])[None, :]
    row = lambda w: pl.BlockSpec((tm, w), lambda i: (i, 0))
    const = lambda r, w: pl.BlockSpec((r, w), lambda i: (0, 0))
    return pl.pallas_call(
        _mla_in_kernel,
        grid=(s // tm,),
        in_specs=[row(d), const(1, d), const(d, n), const(1, Q_LORA_RANK), const(1, KV_LORA_RANK),
                  row(1), const(1, LANES)],
        out_specs=[row(Q_LORA_RANK), row(KV_LORA_RANK), row(LANES)],
        out_shape=[jax.ShapeDtypeStruct((s, Q_LORA_RANK), BF16),
                   jax.ShapeDtypeStruct((s, KV_LORA_RANK), BF16),
                   jax.ShapeDtypeStruct((s, LANES), BF16)],
        compiler_params=_params(1),
        name="mla_in",
    )(h, norm_g[None, :], w_pad, q_norm[None, :], kv_norm[None, :], positions, freq)


def _mla_q_kernel(w_ref, cq_ref, c_ref, s1_ref, s2_ref, q_ref, *, heads, scale):
    c, s1, s2 = c_ref[...], s1_ref[...], s2_ref[...]
    cq = cq_ref[...]
    half = QK_ROPE_DIM // 2
    tq = q_ref.shape[3]
    for h in range(heads):
        acc = _dot_nt(w_ref[h * QK_PAD_DIM:(h + 1) * QK_PAD_DIM, :], cq)
        nope = (acc[:LANES, :] * scale).astype(BF16)
        t = acc[LANES:, :]
        down = jnp.concatenate([t[LANES - half:], t[:LANES - half]], axis=0)
        up = jnp.concatenate([t[half:], t[:half]], axis=0)
        rope = ((t * c + down * s1 + up * s2) * scale).astype(BF16)
        for tt in range(acc.shape[1] // tq):
            q_ref[h, tt, :LANES, :] = nope[:, tt * tq:(tt + 1) * tq]
            q_ref[h, tt, LANES:, :] = rope[:, tt * tq:(tt + 1) * tq]


def _mla_q(cq, w_uq_pad_t, tabs_t, scale, t, tm=1024, heads_per_tile=8):
    s, r = cq.shape
    n = w_uq_pad_t.shape[0]
    tn = heads_per_tile * QK_PAD_DIM
    col = pl.BlockSpec((LANES, tm), lambda j, i: (0, i))
    return pl.pallas_call(
        functools.partial(_mla_q_kernel, heads=heads_per_tile, scale=scale),
        grid=(n // tn, s // tm),
        in_specs=[pl.BlockSpec((tn, r), lambda j, i: (j, 0)), pl.BlockSpec((tm, r), lambda j, i: (i, 0)),
                  col, col, col],
        out_specs=pl.BlockSpec((heads_per_tile, tm // t, QK_PAD_DIM, t), lambda j, i: (j, i, 0, 0)),
        out_shape=jax.ShapeDtypeStruct((n // QK_PAD_DIM, s // t, QK_PAD_DIM, t), BF16),
        compiler_params=_params(2),
        name="mla_q",
    )(w_uq_pad_t, cq, *tabs_t)


V_ROWS = V_HEAD_DIM + 16


def _mla_kv_kernel(ckv_ref, wk_ref, wv_ref, kr_ref, k_ref, v_ref, *, heads, t):
    ckv = ckv_ref[...]
    kn = _dot(ckv, wk_ref[...])
    vt = _dot_nt(wv_ref[...], ckv)
    kr = kr_ref[...]
    ones = jnp.ones((V_ROWS - V_HEAD_DIM, t), BF16)
    for h in range(heads):
        k_ref[:, h * QK_PAD_DIM:h * QK_PAD_DIM + LANES] = kn[:, h * LANES:(h + 1) * LANES].astype(BF16)
        k_ref[:, h * QK_PAD_DIM + LANES:(h + 1) * QK_PAD_DIM] = kr
        for tt in range(ckv.shape[0] // t):
            v_ref[h, tt, :V_HEAD_DIM, :] = vt[h * V_HEAD_DIM:(h + 1) * V_HEAD_DIM, tt * t:(tt + 1) * t].astype(BF16)
            v_ref[h, tt, V_HEAD_DIM:, :] = ones


def _mla_kv(ckv, w_uk, w_uv_t, kr, t, tm=1024, heads_per_tile=8):
    s, r = ckv.shape
    n = w_uk.shape[1]
    heads = n // LANES
    tn = heads_per_tile * LANES
    return pl.pallas_call(
        functools.partial(_mla_kv_kernel, heads=heads_per_tile, t=t),
        grid=(s // tm, n // tn),
        in_specs=[pl.BlockSpec((tm, r), lambda i, j: (i, 0)),
                  pl.BlockSpec((r, tn), lambda i, j: (0, j)),
                  pl.BlockSpec((tn, r), lambda i, j: (j, 0)),
                  pl.BlockSpec((tm, LANES), lambda i, j: (i, 0))],
        out_specs=[pl.BlockSpec((tm, 2 * tn), lambda i, j: (i, j)),
                   pl.BlockSpec((heads_per_tile, tm // t, V_ROWS, t), lambda i, j: (j, i, 0, 0))],
        out_shape=[jax.ShapeDtypeStruct((s, 2 * n), BF16),
                   jax.ShapeDtypeStruct((heads, s // t, V_ROWS, t), BF16)],
        compiler_params=_params(2),
        name="mla_kv",
    )(ckv, w_uk, w_uv_t, kr)


def _mla_attn_kernel(q_ref, k_ref, v_ref, *rest, t, unrolls, n_cast):
    w_refs, o_ref, w_bf16_refs = rest[:n_cast], rest[n_cast], rest[n_cast + 1:2 * n_cast + 1]
    m_sc, acc_sc, s_even, s_odd = rest[2 * n_cast + 1:]
    for w_ref, w_bf16_ref in zip(w_refs, w_bf16_refs):
        w_bf16_ref[...] = w_ref[...].astype(BF16)

    def q_tile(qi, carry):
        _mla_attn_q_tile(pl.program_id(1) * q_ref.shape[0] + qi, q_ref[qi], k_ref, v_ref,
                         o_ref.at[pl.ds(pl.multiple_of(qi * t, t), t), :],
                         m_sc, acc_sc, s_even, s_odd, t=t, unrolls=unrolls)
        return carry

    lax.fori_loop(0, q_ref.shape[0], q_tile, 0)


def _mla_attn_q_tile(i, q, k_ref, v_ref, o_ref, m_sc, acc_sc, s_even, s_odd, *, t, unrolls):
    m_sc[...] = jnp.full(m_sc.shape, -jnp.inf, F32)
    acc_sc[...] = jnp.zeros(acc_sc.shape, F32)

    def scores(j):
        return _dot(k_ref[pl.ds(pl.multiple_of(j * t, t), t), :], q)

    def softmax_pv(s_ref, j, diagonal):
        def load():
            s = s_ref[...]
            if diagonal:
                key = lax.broadcasted_iota(jnp.int32, (t, t), 0)
                query = lax.broadcasted_iota(jnp.int32, (t, t), 1)
                s = jnp.where(key <= query, s, -jnp.inf)
            return s
        m_prev = m_sc[...]
        m_next = jnp.maximum(m_prev, jnp.max(load(), axis=0, keepdims=True))
        m_sc[...] = m_next
        p = jnp.exp2(load() - m_sc[...]).astype(BF16)
        alpha = jnp.exp2(m_prev - m_next)
        acc_sc[...] = alpha * acc_sc[...] + _dot(v_ref[j], p)

    s_even[...] = scores(0)

    def pairs(first, n_pairs):
        for d in range(n_pairs):
            j = first + 2 * d
            s_odd[...] = scores(j + 1)
            softmax_pv(s_even, j, False)
            s_even[...] = scores(j + 2)
            softmax_pv(s_odd, j + 1, False)

    done = 0
    for u in unrolls:
        n_groups = (i // 2 - done) // u

        def group(g, carry, u=u, done=done):
            pairs(2 * (done + g * u), u)
            return carry

        lax.fori_loop(0, n_groups, group, 0)
        done = done + n_groups * u

    @pl.when(i % 2 == 0)
    def _():
        softmax_pv(s_even, i, True)

    @pl.when(i % 2 == 1)
    def _():
        s_odd[...] = scores(i)
        softmax_pv(s_even, i - 1, False)
        softmax_pv(s_odd, i, True)

    out_t = acc_sc[:V_HEAD_DIM, :] / acc_sc[V_HEAD_DIM:V_HEAD_DIM + 1, :]
    o_ref[...] = out_t.T.astype(o_ref.dtype)


BF16_SUBLANES = 16


def _mla_attn(q_t, k, v_t, t, cast_weights=(), q_tiles_per_step=4, unrolls=(8, 4, 2, 1)):
    s = k.shape[0]
    h = MLA_HEADS
    g = q_tiles_per_step
    n_i = s // (t * g)
    n_steps = h * n_i
    assert unrolls[-1] == 1
    cast_specs = []
    for w in cast_weights:
        rows, cols = w.shape
        share = next(sh for sh in (1, 2, 4, 8) if rows % (n_steps // sh * BF16_SUBLANES) == 0)
        cast_specs.append(pl.BlockSpec((rows // (n_steps // share), cols),
                                       lambda hh, i, share=share: ((hh * n_i + i) // share, 0)))
    outs = pl.pallas_call(
        functools.partial(_mla_attn_kernel, t=t, unrolls=unrolls, n_cast=len(cast_weights)),
        grid=(h, n_i),
        in_specs=[pl.BlockSpec((None, g, QK_PAD_DIM, t), lambda hh, i: (hh, i, 0, 0)),
                  pl.BlockSpec((s, QK_PAD_DIM), lambda hh, i: (0, hh)),
                  pl.BlockSpec((None, s // t, V_ROWS, t), lambda hh, i: (hh, 0, 0, 0))] + cast_specs,
        out_specs=[pl.BlockSpec((g * t, V_HEAD_DIM), lambda hh, i: (i, hh))] + cast_specs,
        out_shape=[jax.ShapeDtypeStruct((s, h * V_HEAD_DIM), BF16)]
        + [jax.ShapeDtypeStruct(w.shape, BF16) for w in cast_weights],
        scratch_shapes=[pltpu.VMEM((1, t), F32),
                        pltpu.VMEM((V_ROWS, t), F32),
                        pltpu.VMEM((t, t), F32), pltpu.VMEM((t, t), F32)],
        compiler_params=_params(2),
        name="mla_attn",
    )(q_t, k, v_t, *cast_weights)
    return outs[0], outs[1:]


def _sb_attn_kernel(q_ref, k_ref, v_ref, o_ref, tri_sc, *bufs, t):
    n_tiles = q_ref.shape[0] // t
    w = 2 * t
    z_sc, lom_sc, lb_sc, a_sc, wsum_sc, carry_sc, acc_sc = [bufs[2 * kind:2 * kind + 2] for kind in range(7)]
    tri_sc[...] = jnp.where(lax.broadcasted_iota(jnp.int32, (w, w), 0)
                            > lax.broadcasted_iota(jnp.int32, (w, w), 1), -1.0, 0.0).astype(BF16)

    def log_probs(z):
        neg_log_om = jnp.maximum(z, 0.0) + jnp.log2(1.0 + jnp.exp2(-jnp.abs(z)))
        return neg_log_om, z - neg_log_om

    def tile_rows(i):
        return pl.ds(pl.multiple_of(i * t, t), t)

    def window_start(i):
        return pl.multiple_of(jnp.maximum(i - 1, 0) * t, t)

    def window_scores(slot, i):
        i = jnp.minimum(i, n_tiles - 1)
        z_sc[slot][...] = _dot_nt(q_ref[tile_rows(i), :], k_ref[pl.ds(window_start(i), w), :])

    def window_log_probs(slot, first_tile=False):
        neg_log_om, log_beta = log_probs(z_sc[slot][...])
        lom, lb, wsum = lom_sc[slot], lb_sc[slot], wsum_sc[slot]
        strict = (lax.broadcasted_iota(jnp.int32, (t, t), 1)
                  < lax.broadcasted_iota(jnp.int32, (t, t), 0))
        if first_tile:
            halves = ((strict, neg_log_om[:, :t], log_beta[:, :t]), (None, None, None))
        else:
            halves = ((True, neg_log_om[:, :t], log_beta[:, :t]), (strict, neg_log_om[:, t:], log_beta[:, t:]))
        total = jnp.zeros((t, 1), F32)
        for half, (mask, nlo, lbeta) in enumerate(halves):
            cols = slice(half * t, (half + 1) * t)
            if mask is None:
                lom[:, cols] = jnp.zeros((t, t), BF16)
                lb[:, cols] = jnp.full((t, t), -jnp.inf, F32)
                continue
            if mask is not True:
                nlo = jnp.where(mask, nlo, 0.0)
                lbeta = jnp.where(mask, lbeta, -jnp.inf)
            lom[:, cols] = nlo.astype(BF16)
            lb[:, cols] = lbeta
            total = total + jnp.sum(nlo, axis=1, keepdims=True)
        wsum[...] = jnp.broadcast_to(-total, (t, LANES))

    def window_weights(slot):
        lom, lb, a = lom_sc[slot], lb_sc[slot], a_sc[slot]
        a[:, :t] = jnp.exp2(lb[:, :t] + _dot(lom[...], tri_sc[:, :t])).astype(BF16)
        a[:, t:] = jnp.exp2(lb[:, t:] + _dot(lom[:, t:], tri_sc[t:, t:])).astype(BF16)
        carry_sc[slot][...] = wsum_sc[slot][...]

    def window_values(slot, i):
        acc_sc[slot][...] = _dot(a_sc[slot][...], v_ref[pl.ds(window_start(i), w), :])

    def earlier_tiles(slot, i):
        carry, acc = carry_sc[slot], acc_sc[slot]
        q = q_ref[tile_rows(i), :]

        def more(state):
            j, carry_max = state
            return (j >= 0) & (carry_max > SB_LOG2_WEIGHT_FLOOR)

        def body(state):
            j, _ = state
            neg_log_om, log_beta = log_probs(_dot_nt(q, k_ref[tile_rows(j), :]))
            after = _dot(neg_log_om.astype(BF16), tri_sc[:t, :t]) + jnp.tile(carry[...], (1, t // LANES))
            a = jnp.exp2(log_beta + after)
            acc[...] += _dot(a.astype(BF16), v_ref[tile_rows(j), :])
            carry[...] -= jnp.sum(neg_log_om, axis=1, keepdims=True)
            return j - 1, jnp.max(carry[...])

        lax.while_loop(more, body, (i - 2, jnp.max(carry[...])))
        o_ref[tile_rows(i), :] = acc[...].astype(o_ref.dtype)

    def step(s, slot):
        window_scores(slot, s + 2)
        window_log_probs(1 - slot)
        window_weights(slot)
        window_values(1 - slot, s - 1)
        earlier_tiles(1 - slot, s - 1)

    def step_pair(p, _):
        step(2 * p + 1, 1)
        step(2 * p + 2, 0)
        return 0

    window_scores(0, 0)
    window_log_probs(0, first_tile=True)
    window_scores(1, 1)
    window_weights(0)
    window_log_probs(1)
    window_scores(0, 2)
    lax.fori_loop(0, n_tiles // 2, step_pair, 0)


def _sb_attn(qkv, t=256):
    s = qkv.shape[0]
    h, d = SB_HEADS, SB_HEAD_DIM
    assert s % (2 * t) == 0 and s >= 3 * t
    return pl.pallas_call(
        functools.partial(_sb_attn_kernel, t=t),
        grid=(h,),
        in_specs=[pl.BlockSpec((s, d), lambda hh: (0, hh)),
                  pl.BlockSpec((s, d), lambda hh: (0, h + hh)),
                  pl.BlockSpec((s, d), lambda hh: (0, 2 * h + hh))],
        out_specs=pl.BlockSpec((s, d), lambda hh: (0, hh)),
        out_shape=jax.ShapeDtypeStruct((s, h * d), BF16),
        scratch_shapes=[pltpu.VMEM((2 * t, 2 * t), BF16)]
        + 2 * [pltpu.VMEM((t, 2 * t), F32)]
        + 2 * [pltpu.VMEM((t, 2 * t), BF16)]
        + 2 * [pltpu.VMEM((t, 2 * t), F32)]
        + 2 * [pltpu.VMEM((t, 2 * t), BF16)]
        + 2 * [pltpu.VMEM((t, LANES), F32)]
        + 2 * [pltpu.VMEM((t, LANES), F32)]
        + 2 * [pltpu.VMEM((t, d), F32)],
        compiler_params=_params(1),
        name="sb_attn",
    )(qkv, qkv, qkv)


def _normalize_rows_once(h_ref, g_ref, a_sc):
    @pl.when(pl.program_id(1) == 0)
    def _():
        a_sc[...] = _rms(h_ref[...], g_ref[...]).astype(BF16)


def _norm_matmul_scale_kernel(h_ref, g_ref, w_ref, cs_ref, o_ref, a_sc):
    _normalize_rows_once(h_ref, g_ref, a_sc)
    o_ref[...] = (_dot(a_sc[...], w_ref[...].astype(BF16)) * cs_ref[...]).astype(o_ref.dtype)


def _norm_matmul_scale(h, norm_g, w, layer, col_scale, tm=1024, tn=1536):
    s, k = h.shape
    n = w.shape[2]
    return pl.pallas_call(
        _norm_matmul_scale_kernel,
        grid=(s // tm, n // tn),
        in_specs=[pl.BlockSpec((tm, k), lambda i, j: (i, 0)), pl.BlockSpec((1, k), lambda i, j: (0, 0)),
                  pl.BlockSpec((None, k, tn), lambda i, j: (layer, 0, j)),
                  pl.BlockSpec((1, tn), lambda i, j: (0, j))],
        out_specs=pl.BlockSpec((tm, tn), lambda i, j: (i, j)),
        out_shape=jax.ShapeDtypeStruct((s, n), BF16),
        scratch_shapes=[pltpu.VMEM((tm, k), BF16)],
        compiler_params=_params(2),
        name="norm_matmul_scale",
    )(h, norm_g[None, :], w, col_scale[None, :])


def _matmul_residual_kernel(a_ref, w_ref, r_ref, o_ref):
    o_ref[...] = r_ref[...] + _dot(a_ref[...], w_ref[...].astype(BF16))


def _matmul_residual(a, w, layer, res, tm, tn):
    s, k = a.shape
    n = w.shape[2]
    return pl.pallas_call(
        _matmul_residual_kernel,
        grid=(s // tm, n // tn),
        in_specs=[pl.BlockSpec((tm, k), lambda i, j: (i, 0)),
                  pl.BlockSpec((None, k, tn), lambda i, j: (layer, 0, j)),
                  pl.BlockSpec((tm, tn), lambda i, j: (i, j))],
        out_specs=pl.BlockSpec((tm, tn), lambda i, j: (i, j)),
        out_shape=jax.ShapeDtypeStruct((s, n), F32),
        compiler_params=_params(2),
        name="matmul_residual",
    )(a, w, res)


def _matmul_residual_norm_kernel(a_ref, w_ref, r_ref, g_ref, o_ref):
    kk = pl.program_id(1)

    @pl.when(kk == 0)
    def _():
        o_ref[...] = r_ref[...]

    o_ref[...] += _dot(a_ref[...], w_ref[...].astype(BF16))

    @pl.when(kk == pl.num_programs(1) - 1)
    def _():
        o_ref[...] = _rms(o_ref[...], g_ref[...])


def _matmul_residual_norm(a, w, layer, res, norm_g, tm=512, tk=2816):
    s, k = a.shape
    n = w.shape[2]
    return pl.pallas_call(
        _matmul_residual_norm_kernel,
        grid=(s // tm, k // tk),
        in_specs=[pl.BlockSpec((tm, tk), lambda i, kk: (i, kk)),
                  pl.BlockSpec((None, tk, n), lambda i, kk: (layer, kk, 0)),
                  pl.BlockSpec((tm, n), lambda i, kk: (i, 0)),
                  pl.BlockSpec((1, n), lambda i, kk: (0, 0))],
        out_specs=pl.BlockSpec((tm, n), lambda i, kk: (i, 0)),
        out_shape=jax.ShapeDtypeStruct((s, n), F32),
        compiler_params=_params(2),
        name="matmul_residual_norm",
    )(a, w, res, norm_g[None, :])


def _ffn_up_kernel(h_ref, g_ref, wg_ref, wu_ref, o_ref, a_sc):
    _normalize_rows_once(h_ref, g_ref, a_sc)
    a = a_sc[...]
    g = _dot(a, wg_ref[...].astype(BF16))
    u = _dot(a, wu_ref[...].astype(BF16))
    o_ref[...] = (g * jax.nn.sigmoid(g) * u).astype(o_ref.dtype)


def _ffn_up(h, norm_g, w_gate_up, layer, tm=1024, tn=512):
    s, k = h.shape
    d_ff = w_gate_up.shape[2] // 2
    nj = d_ff // tn
    return pl.pallas_call(
        _ffn_up_kernel,
        grid=(s // tm, nj),
        in_specs=[pl.BlockSpec((tm, k), lambda i, j: (i, 0)), pl.BlockSpec((1, k), lambda i, j: (0, 0)),
                  pl.BlockSpec((None, k, tn), lambda i, j: (layer, 0, j)),
                  pl.BlockSpec((None, k, tn), lambda i, j: (layer, 0, nj + j))],
        out_specs=pl.BlockSpec((tm, tn), lambda i, j: (i, j)),
        out_shape=jax.ShapeDtypeStruct((s, d_ff), BF16),
        scratch_shapes=[pltpu.VMEM((tm, k), BF16)],
        compiler_params=_params(2),
        name="ffn_up",
    )(h, norm_g[None, :], w_gate_up, w_gate_up)


def _ffn(h, norm_g, w_gate_up, w_down, layer, out_norm_g=None):
    act = _ffn_up(h, norm_g, w_gate_up, layer)
    if out_norm_g is not None:
        return _matmul_residual_norm(act, w_down, layer, h, out_norm_g)
    return _matmul_residual(act, w_down, layer, h, tm=1024, tn=512)


def kernel(x, positions, attn_norm, mla_w_in, mla_q_norm, mla_kv_norm, mla_w_uq, mla_w_ukv, mla_w_o,
           sb_w_in, sb_w_o, ffn_norm, ffn_w_gate_up, ffn_w_down, final_norm):
    b, s, d = x.shape
    assert b == 1 and d == D_MODEL
    h = x.reshape(s, d)

    attn_tile = 512
    tabs_t = _rope_tables_t(positions.reshape(1, s))
    w_in_pad = jnp.pad(mla_w_in[0], ((0, 0), (0, LANES - QK_ROPE_DIM))).astype(BF16)
    qk_dim = QK_NOPE_DIM + QK_ROPE_DIM
    w_uq_pad = jnp.pad(mla_w_uq[0].reshape(Q_LORA_RANK, MLA_HEADS, qk_dim),
                       ((0, 0), (0, 0), (0, QK_PAD_DIM - qk_dim)))
    w_uq_pad_t = w_uq_pad.reshape(Q_LORA_RANK, MLA_HEADS * QK_PAD_DIM).T.astype(BF16)
    w_ukv = mla_w_ukv[0].reshape(KV_LORA_RANK, MLA_HEADS, QK_NOPE_DIM + V_HEAD_DIM)
    w_uk = w_ukv[:, :, :QK_NOPE_DIM].reshape(KV_LORA_RANK, MLA_HEADS * QK_NOPE_DIM).astype(BF16)
    w_uv_t = w_ukv[:, :, QK_NOPE_DIM:].reshape(KV_LORA_RANK, MLA_HEADS * V_HEAD_DIM).T.astype(BF16)

    cq, ckv, kr = _mla_in(h, attn_norm[0], w_in_pad, mla_q_norm[0], mla_kv_norm[0], positions.reshape(s, 1))
    q_t = _mla_q(cq, w_uq_pad_t, tabs_t, qk_dim ** -0.5 * LOG2_E, attn_tile)
    k, v_t = _mla_kv(ckv, w_uk, w_uv_t, kr, attn_tile)
    to_cast = (mla_w_o, sb_w_in, sb_w_o, ffn_w_gate_up, ffn_w_down)
    o, cast = _mla_attn(q_t, k, v_t, attn_tile, cast_weights=[w.reshape(-1, w.shape[2]) for w in to_cast])
    w_o, w_sb_in, w_sb_o, w_gate_up, w_down = [c.reshape(w.shape) for c, w in zip(cast, to_cast)]
    h = _matmul_residual(o, w_o, 0, h, tm=1024, tn=1024)
    h = _ffn(h, ffn_norm[0], w_gate_up, w_down, 0)

    n_q = SB_HEADS * SB_HEAD_DIM
    col_scale = jnp.concatenate([jnp.full((n_q,), SB_HEAD_DIM ** -0.5 * LOG2_E, F32), jnp.ones((2 * n_q,), F32)])
    qkv = _norm_matmul_scale(h, attn_norm[1], w_sb_in, 0, col_scale)
    o = _sb_attn(qkv)
    h = _matmul_residual(o, w_sb_o, 0, h, tm=1024, tn=1024)
    return _ffn(h, ffn_norm[1], w_gate_up, w_down, 1, out_norm_g=final_norm).reshape(b, s, d)
```

```python
import functools

import jax
import jax.numpy as jnp
from jax import lax
from jax.experimental import pallas as pl
from jax.experimental.pallas import tpu as pltpu

D_MODEL = 2048
MLA_HEADS = 16
Q_LORA_RANK = 512
KV_LORA_RANK = 512
QK_NOPE_DIM = 128
QK_ROPE_DIM = 64
V_HEAD_DIM = 128
ROPE_THETA = 10000.0
SB_HEADS = 16
SB_HEAD_DIM = 128
RMS_EPS = 1e-6

LANES = 128
QK_PAD_DIM = 2 * LANES
VMEM_LIMIT_BYTES = 56 * 1024 * 1024

LOG2_E = 1.4426950408889634
SB_LOG2_WEIGHT_FLOOR = -120.0 * LOG2_E

F32 = jnp.float32
BF16 = jnp.bfloat16


def _params(n_axes, fuse_inputs=None):
    return pltpu.CompilerParams(
        dimension_semantics=("arbitrary",) * n_axes, vmem_limit_bytes=VMEM_LIMIT_BYTES,
        allow_input_fusion=fuse_inputs)


def _dot(a, b):
    return jnp.dot(a, b, preferred_element_type=F32)


def _dot_nt(a, b):
    return lax.dot_general(a, b, (((1,), (1,)), ((), ())), preferred_element_type=F32)


def _rms(x, g):
    return x * lax.rsqrt(jnp.mean(x * x, axis=-1, keepdims=True) + RMS_EPS) * g


def _rope(t, c, s1, s2):
    half = QK_ROPE_DIM // 2
    return t * c + pltpu.roll(t, half, 1) * s1 + pltpu.roll(t, LANES - half, 1) * s2


def _rope_tables_t_kernel(pos_ref, freq_ref, c_ref, s1_ref, s2_ref):
    half = QK_ROPE_DIM // 2
    ang = freq_ref[...] * pos_ref[...].astype(F32)
    r = lax.broadcasted_iota(jnp.int32, ang.shape, 0)
    cos, sin = jnp.cos(ang), jnp.sin(ang)
    c_ref[...] = jnp.where(r < 2 * half, cos, 0.0)
    s1_ref[...] = jnp.where((r >= half) & (r < 2 * half), sin, 0.0)
    s2_ref[...] = jnp.where(r < half, -sin, 0.0)


def _rope_tables_t(positions, tm=2048):
    s = positions.shape[1]
    inv_freq = ROPE_THETA ** (-jnp.arange(0, QK_ROPE_DIM, 2, dtype=F32) / QK_ROPE_DIM)
    freq = jnp.concatenate([inv_freq, inv_freq, jnp.zeros((LANES - QK_ROPE_DIM,), F32)])[:, None]
    out = jax.ShapeDtypeStruct((LANES, s), F32)
    col = pl.BlockSpec((LANES, tm), lambda i: (0, i))
    return pl.pallas_call(
        _rope_tables_t_kernel,
        grid=(s // tm,),
        in_specs=[pl.BlockSpec((1, tm), lambda i: (0, i)), pl.BlockSpec((LANES, 1), lambda i: (0, 0))],
        out_specs=[col, col, col],
        out_shape=[out, out, out],
        compiler_params=_params(1),
        name="rope_tables_t",
    )(positions, freq)


def _mla_in_kernel(h_ref, g_ref, w_ref, qn_ref, kvn_ref, pos_ref, freq_ref, cq_ref, ckv_ref, kr_ref):
    chunk = h_ref.shape[0] // 4
    half = QK_ROPE_DIM // 2
    lane = lax.broadcasted_iota(jnp.int32, (chunk, LANES), 1)
    for r in range(4):
        rows = slice(r * chunk, (r + 1) * chunk)
        a = _rms(h_ref[rows, :], g_ref[...]).astype(BF16)
        proj = _dot(a, w_ref[...])
        cq_ref[rows, :] = _rms(proj[:, :Q_LORA_RANK], qn_ref[...]).astype(BF16)
        ckv_ref[rows, :] = _rms(proj[:, Q_LORA_RANK:Q_LORA_RANK + KV_LORA_RANK], kvn_ref[...]).astype(BF16)
        kr = proj[:, Q_LORA_RANK + KV_LORA_RANK:]
        ang = pos_ref[rows, :].astype(F32) * freq_ref[...]
        cos, sin = jnp.cos(ang), jnp.sin(ang)
        c = jnp.where(lane < 2 * half, cos, 0.0)
        s1 = jnp.where((lane >= half) & (lane < 2 * half), sin, 0.0)
        s2 = jnp.where(lane < half, -sin, 0.0)
        kr_ref[rows, :] = _rope(kr, c, s1, s2).astype(BF16)


def _mla_in(h, norm_g, w_pad, q_norm, kv_norm, positions, tm=1024):
    s, d = h.shape
    n = w_pad.shape[1]
    inv_freq = ROPE_THETA ** (-jnp.arange(0, QK_ROPE_DIM, 2, dtype=F32) / QK_ROPE_DIM)
    freq = jnp.concatenate([inv_freq, inv_freq, jnp.zeros((LANES - QK_ROPE_DIM,), F32)])[None, :]
    row = lambda w: pl.BlockSpec((tm, w), lambda i: (i, 0))
    const = lambda r, w: pl.BlockSpec((r, w), lambda i: (0, 0))
    return pl.pallas_call(
        _mla_in_kernel,
        grid=(s // tm,),
        in_specs=[row(d), const(1, d), const(d, n), const(1, Q_LORA_RANK), const(1, KV_LORA_RANK),
                  row(1), const(1, LANES)],
        out_specs=[row(Q_LORA_RANK), row(KV_LORA_RANK), row(LANES)],
        out_shape=[jax.ShapeDtypeStruct((s, Q_LORA_RANK), BF16),
                   jax.ShapeDtypeStruct((s, KV_LORA_RANK), BF16),
                   jax.ShapeDtypeStruct((s, LANES), BF16)],
        compiler_params=_params(1, [False, False, True, False, False, False, False]),
        name="mla_in",
    )(h, norm_g[None, :], w_pad, q_norm[None, :], kv_norm[None, :], positions, freq)


def _mla_q_kernel(w_ref, cq_ref, c_ref, s1_ref, s2_ref, q_ref, *, heads, scale):
    c, s1, s2 = c_ref[...], s1_ref[...], s2_ref[...]
    cq = cq_ref[...]
    half = QK_ROPE_DIM // 2
    tq = q_ref.shape[3]
    for h in range(heads):
        acc = _dot_nt(w_ref[h * QK_PAD_DIM:(h + 1) * QK_PAD_DIM, :], cq)
        nope = (acc[:LANES, :] * scale).astype(BF16)
        t = acc[LANES:, :]
        down = jnp.concatenate([t[LANES - half:], t[:LANES - half]], axis=0)
        up = jnp.concatenate([t[half:], t[:half]], axis=0)
        rope = ((t * c + down * s1 + up * s2) * scale).astype(BF16)
        for tt in range(acc.shape[1] // tq):
            q_ref[h, tt, :LANES, :] = nope[:, tt * tq:(tt + 1) * tq]
            q_ref[h, tt, LANES:, :] = rope[:, tt * tq:(tt + 1) * tq]


def _mla_q(cq, w_uq_pad_t, tabs_t, scale, t, tm=1024, heads_per_tile=8):
    s, r = cq.shape
    n = w_uq_pad_t.shape[0]
    tn = heads_per_tile * QK_PAD_DIM
    col = pl.BlockSpec((LANES, tm), lambda j, i: (0, i))
    return pl.pallas_call(
        functools.partial(_mla_q_kernel, heads=heads_per_tile, scale=scale),
        grid=(n // tn, s // tm),
        in_specs=[pl.BlockSpec((tn, r), lambda j, i: (j, 0)), pl.BlockSpec((tm, r), lambda j, i: (i, 0)),
                  col, col, col],
        out_specs=pl.BlockSpec((heads_per_tile, tm // t, QK_PAD_DIM, t), lambda j, i: (j, i, 0, 0)),
        out_shape=jax.ShapeDtypeStruct((n // QK_PAD_DIM, s // t, QK_PAD_DIM, t), BF16),
        compiler_params=_params(2, [True, False, False, False, False]),
        name="mla_q",
    )(w_uq_pad_t, cq, *tabs_t)


V_ROWS = V_HEAD_DIM + 16


def _mla_kv_kernel(ckv_ref, wk_ref, wv_ref, kr_ref, k_ref, v_ref, *, heads, t):
    ckv = ckv_ref[...]
    kn = _dot(ckv, wk_ref[...])
    vt = _dot_nt(wv_ref[...], ckv)
    kr = kr_ref[...]
    ones = jnp.ones((V_ROWS - V_HEAD_DIM, t), BF16)
    for h in range(heads):
        k_ref[:, h * QK_PAD_DIM:h * QK_PAD_DIM + LANES] = kn[:, h * LANES:(h + 1) * LANES].astype(BF16)
        k_ref[:, h * QK_PAD_DIM + LANES:(h + 1) * QK_PAD_DIM] = kr
        for tt in range(ckv.shape[0] // t):
            v_ref[h, tt, :V_HEAD_DIM, :] = vt[h * V_HEAD_DIM:(h + 1) * V_HEAD_DIM, tt * t:(tt + 1) * t].astype(BF16)
            v_ref[h, tt, V_HEAD_DIM:, :] = ones


def _mla_kv(ckv, w_uk, w_uv_t, kr, t, tm=1024, heads_per_tile=8):
    s, r = ckv.shape
    n = w_uk.shape[1]
    heads = n // LANES
    tn = heads_per_tile * LANES
    return pl.pallas_call(
        functools.partial(_mla_kv_kernel, heads=heads_per_tile, t=t),
        grid=(s // tm, n // tn),
        in_specs=[pl.BlockSpec((tm, r), lambda i, j: (i, 0)),
                  pl.BlockSpec((r, tn), lambda i, j: (0, j)),
                  pl.BlockSpec((tn, r), lambda i, j: (j, 0)),
                  pl.BlockSpec((tm, LANES), lambda i, j: (i, 0))],
        out_specs=[pl.BlockSpec((tm, 2 * tn), lambda i, j: (i, j)),
                   pl.BlockSpec((heads_per_tile, tm // t, V_ROWS, t), lambda i, j: (j, i, 0, 0))],
        out_shape=[jax.ShapeDtypeStruct((s, 2 * n), BF16),
                   jax.ShapeDtypeStruct((heads, s // t, V_ROWS, t), BF16)],
        compiler_params=_params(2, [False, True, True, False]),
        name="mla_kv",
    )(ckv, w_uk, w_uv_t, kr)


def _mla_attn_kernel(q_ref, k_ref, v_ref, *rest, t, unrolls, n_cast):
    w_refs, o_ref, w_bf16_refs = rest[:n_cast], rest[n_cast], rest[n_cast + 1:2 * n_cast + 1]
    m_sc, acc_sc, s_even, s_odd = rest[2 * n_cast + 1:]
    for w_ref, w_bf16_ref in zip(w_refs, w_bf16_refs):
        w_bf16_ref[...] = w_ref[...].astype(BF16)

    def q_tile(qi, carry):
        _mla_attn_q_tile(pl.program_id(1) * q_ref.shape[0] + qi, q_ref[qi], k_ref, v_ref,
                         o_ref.at[pl.ds(pl.multiple_of(qi * t, t), t), :],
                         m_sc, acc_sc, s_even, s_odd, t=t, unrolls=unrolls)
        return carry

    lax.fori_loop(0, q_ref.shape[0], q_tile, 0)


def _mla_attn_q_tile(i, q, k_ref, v_ref, o_ref, m_sc, acc_sc, s_even, s_odd, *, t, unrolls):
    m_sc[...] = jnp.full(m_sc.shape, -jnp.inf, F32)
    acc_sc[...] = jnp.zeros(acc_sc.shape, F32)

    def scores(j):
        return _dot(k_ref[pl.ds(pl.multiple_of(j * t, t), t), :], q)

    def softmax_pv(s_ref, j, diagonal):
        def load():
            s = s_ref[...]
            if diagonal:
                key = lax.broadcasted_iota(jnp.int32, (t, t), 0)
                query = lax.broadcasted_iota(jnp.int32, (t, t), 1)
                s = jnp.where(key <= query, s, -jnp.inf)
            return s
        m_prev = m_sc[...]
        m_next = jnp.maximum(m_prev, jnp.max(load(), axis=0, keepdims=True))
        m_sc[...] = m_next
        p = jnp.exp2(load() - m_sc[...]).astype(BF16)
        alpha = jnp.exp2(m_prev - m_next)
        acc_sc[...] = alpha * acc_sc[...] + _dot(v_ref[j], p)

    s_even[...] = scores(0)

    def pairs(first, n_pairs):
        for d in range(n_pairs):
            j = first + 2 * d
            s_odd[...] = scores(j + 1)
            softmax_pv(s_even, j, False)
            s_even[...] = scores(j + 2)
            softmax_pv(s_odd, j + 1, False)

    done = 0
    for u in unrolls:
        n_groups = (i // 2 - done) // u

        def group(g, carry, u=u, done=done):
            pairs(2 * (done + g * u), u)
            return carry

        lax.fori_loop(0, n_groups, group, 0)
        done = done + n_groups * u

    @pl.when(i % 2 == 0)
    def _():
        softmax_pv(s_even, i, True)

    @pl.when(i % 2 == 1)
    def _():
        s_odd[...] = scores(i)
        softmax_pv(s_even, i - 1, False)
        softmax_pv(s_odd, i, True)

    out_t = acc_sc[:V_HEAD_DIM, :] / acc_sc[V_HEAD_DIM:V_HEAD_DIM + 1, :]
    o_ref[...] = out_t.T.astype(o_ref.dtype)


BF16_SUBLANES = 16


def _mla_attn(q_t, k, v_t, t, cast_weights=(), q_tiles_per_step=4, unrolls=(8, 4, 2, 1)):
    s = k.shape[0]
    h = MLA_HEADS
    g = q_tiles_per_step
    n_i = s // (t * g)
    n_steps = h * n_i
    assert unrolls[-1] == 1
    cast_specs = []
    for w in cast_weights:
        rows, cols = w.shape
        share = next(sh for sh in (1, 2, 4, 8) if rows % (n_steps // sh * BF16_SUBLANES) == 0)
        cast_specs.append(pl.BlockSpec((rows // (n_steps // share), cols),
                                       lambda hh, i, share=share: ((hh * n_i + i) // share, 0)))
    outs = pl.pallas_call(
        functools.partial(_mla_attn_kernel, t=t, unrolls=unrolls, n_cast=len(cast_weights)),
        grid=(h, n_i),
        in_specs=[pl.BlockSpec((None, g, QK_PAD_DIM, t), lambda hh, i: (hh, i, 0, 0)),
                  pl.BlockSpec((s, QK_PAD_DIM), lambda hh, i: (0, hh)),
                  pl.BlockSpec((None, s // t, V_ROWS, t), lambda hh, i: (hh, 0, 0, 0))] + cast_specs,
        out_specs=[pl.BlockSpec((g * t, V_HEAD_DIM), lambda hh, i: (i, hh))] + cast_specs,
        out_shape=[jax.ShapeDtypeStruct((s, h * V_HEAD_DIM), BF16)]
        + [jax.ShapeDtypeStruct(w.shape, BF16) for w in cast_weights],
        scratch_shapes=[pltpu.VMEM((1, t), F32),
                        pltpu.VMEM((V_ROWS, t), F32),
                        pltpu.VMEM((t, t), F32), pltpu.VMEM((t, t), F32)],
        compiler_params=_params(2),
        name="mla_attn",
    )(q_t, k, v_t, *cast_weights)
    return outs[0], outs[1:]


def _sb_attn_kernel(q_ref, k_ref, v_ref, o_ref, tri_sc, *bufs, t):
    n_tiles = q_ref.shape[0] // t
    w = 2 * t
    z_sc, lom_sc, lb_sc, a_sc, wsum_sc, carry_sc, acc_sc = [bufs[2 * kind:2 * kind + 2] for kind in range(7)]
    tri_sc[...] = jnp.where(lax.broadcasted_iota(jnp.int32, (w, w), 0)
                            > lax.broadcasted_iota(jnp.int32, (w, w), 1), -1.0, 0.0).astype(BF16)

    def log_probs(z):
        neg_log_om = jnp.maximum(z, 0.0) + jnp.log2(1.0 + jnp.exp2(-jnp.abs(z)))
        return neg_log_om, z - neg_log_om

    def tile_rows(i):
        return pl.ds(pl.multiple_of(i * t, t), t)

    def window_start(i):
        return pl.multiple_of(jnp.maximum(i - 1, 0) * t, t)

    def window_scores(slot, i):
        i = jnp.minimum(i, n_tiles - 1)
        z_sc[slot][...] = _dot_nt(q_ref[tile_rows(i), :], k_ref[pl.ds(window_start(i), w), :])

    def window_log_probs(slot, first_tile=False):
        neg_log_om, log_beta = log_probs(z_sc[slot][...])
        lom, lb, wsum = lom_sc[slot], lb_sc[slot], wsum_sc[slot]
        strict = (lax.broadcasted_iota(jnp.int32, (t, t), 1)
                  < lax.broadcasted_iota(jnp.int32, (t, t), 0))
        if first_tile:
            halves = ((strict, neg_log_om[:, :t], log_beta[:, :t]), (None, None, None))
        else:
            halves = ((True, neg_log_om[:, :t], log_beta[:, :t]), (strict, neg_log_om[:, t:], log_beta[:, t:]))
        total = jnp.zeros((t, 1), F32)
        for half, (mask, nlo, lbeta) in enumerate(halves):
            cols = slice(half * t, (half + 1) * t)
            if mask is None:
                lom[:, cols] = jnp.zeros((t, t), BF16)
                lb[:, cols] = jnp.full((t, t), -jnp.inf, F32)
                continue
            if mask is not True:
                nlo = jnp.where(mask, nlo, 0.0)
                lbeta = jnp.where(mask, lbeta, -jnp.inf)
            lom[:, cols] = nlo.astype(BF16)
            lb[:, cols] = lbeta
            total = total + jnp.sum(nlo, axis=1, keepdims=True)
        wsum[...] = jnp.broadcast_to(-total, (t, LANES))

    def window_weights(slot):
        lom, lb, a = lom_sc[slot], lb_sc[slot], a_sc[slot]
        a[:, :t] = jnp.exp2(lb[:, :t] + _dot(lom[...], tri_sc[:, :t])).astype(BF16)
        a[:, t:] = jnp.exp2(lb[:, t:] + _dot(lom[:, t:], tri_sc[t:, t:])).astype(BF16)
        carry_sc[slot][...] = wsum_sc[slot][...]

    def window_values(slot, i):
        acc_sc[slot][...] = _dot(a_sc[slot][...], v_ref[pl.ds(window_start(i), w), :])

    def earlier_tiles(slot, i):
        carry, acc = carry_sc[slot], acc_sc[slot]
        q = q_ref[tile_rows(i), :]

        def more(state):
            j, carry_max = state
            return (j >= 0) & (carry_max > SB_LOG2_WEIGHT_FLOOR)

        def body(state):
            j, _ = state
            neg_log_om, log_beta = log_probs(_dot_nt(q, k_ref[tile_rows(j), :]))
            after = _dot(neg_log_om.astype(BF16), tri_sc[:t, :t]) + jnp.tile(carry[...], (1, t // LANES))
            a = jnp.exp2(log_beta + after)
            acc[...] += _dot(a.astype(BF16), v_ref[tile_rows(j), :])
            carry[...] -= jnp.sum(neg_log_om, axis=1, keepdims=True)
            return j - 1, jnp.max(carry[...])

        lax.while_loop(more, body, (i - 2, jnp.max(carry[...])))
        o_ref[tile_rows(i), :] = acc[...].astype(o_ref.dtype)

    def step(s, slot):
        window_scores(slot, s + 2)
        window_log_probs(1 - slot)
        window_weights(slot)
        window_values(1 - slot, s - 1)
        earlier_tiles(1 - slot, s - 1)

    def step_pair(p, _):
        step(2 * p + 1, 1)
        step(2 * p + 2, 0)
        return 0

    window_scores(0, 0)
    window_log_probs(0, first_tile=True)
    window_scores(1, 1)
    window_weights(0)
    window_log_probs(1)
    window_scores(0, 2)
    lax.fori_loop(0, n_tiles // 2, step_pair, 0)


def _sb_attn(qkv, t=256):
    s = qkv.shape[0]
    h, d = SB_HEADS, SB_HEAD_DIM
    assert s % (2 * t) == 0 and s >= 3 * t
    return pl.pallas_call(
        functools.partial(_sb_attn_kernel, t=t),
        grid=(h,),
        in_specs=[pl.BlockSpec((s, d), lambda hh: (0, hh)),
                  pl.BlockSpec((s, d), lambda hh: (0, h + hh)),
                  pl.BlockSpec((s, d), lambda hh: (0, 2 * h + hh))],
        out_specs=pl.BlockSpec((s, d), lambda hh: (0, hh)),
        out_shape=jax.ShapeDtypeStruct((s, h * d), BF16),
        scratch_shapes=[pltpu.VMEM((2 * t, 2 * t), BF16)]
        + 2 * [pltpu.VMEM((t, 2 * t), F32)]
        + 2 * [pltpu.VMEM((t, 2 * t), BF16)]
        + 2 * [pltpu.VMEM((t, 2 * t), F32)]
        + 2 * [pltpu.VMEM((t, 2 * t), BF16)]
        + 2 * [pltpu.VMEM((t, LANES), F32)]
        + 2 * [pltpu.VMEM((t, LANES), F32)]
        + 2 * [pltpu.VMEM((t, d), F32)],
        compiler_params=_params(1),
        name="sb_attn",
    )(qkv, qkv, qkv)


def _normalize_rows_once(h_ref, g_ref, a_sc):
    @pl.when(pl.program_id(1) == 0)
    def _():
        a_sc[...] = _rms(h_ref[...], g_ref[...]).astype(BF16)


def _norm_matmul_scale_kernel(h_ref, g_ref, w_ref, cs_ref, o_ref, a_sc):
    _normalize_rows_once(h_ref, g_ref, a_sc)
    o_ref[...] = (_dot(a_sc[...], w_ref[...].astype(BF16)) * cs_ref[...]).astype(o_ref.dtype)


def _norm_matmul_scale(h, norm_g, w, layer, col_scale, tm=1024, tn=1536):
    s, k = h.shape
    n = w.shape[2]
    return pl.pallas_call(
        _norm_matmul_scale_kernel,
        grid=(s // tm, n // tn),
        in_specs=[pl.BlockSpec((tm, k), lambda i, j: (i, 0)), pl.BlockSpec((1, k), lambda i, j: (0, 0)),
                  pl.BlockSpec((None, k, tn), lambda i, j: (layer, 0, j)),
                  pl.BlockSpec((1, tn), lambda i, j: (0, j))],
        out_specs=pl.BlockSpec((tm, tn), lambda i, j: (i, j)),
        out_shape=jax.ShapeDtypeStruct((s, n), BF16),
        scratch_shapes=[pltpu.VMEM((tm, k), BF16)],
        compiler_params=_params(2),
        name="norm_matmul_scale",
    )(h, norm_g[None, :], w, col_scale[None, :])


def _matmul_residual_kernel(a_ref, w_ref, r_ref, o_ref):
    o_ref[...] = r_ref[...] + _dot(a_ref[...], w_ref[...].astype(BF16))


def _matmul_residual(a, w, layer, res, tm, tn):
    s, k = a.shape
    n = w.shape[2]
    return pl.pallas_call(
        _matmul_residual_kernel,
        grid=(s // tm, n // tn),
        in_specs=[pl.BlockSpec((tm, k), lambda i, j: (i, 0)),
                  pl.BlockSpec((None, k, tn), lambda i, j: (layer, 0, j)),
                  pl.BlockSpec((tm, tn), lambda i, j: (i, j))],
        out_specs=pl.BlockSpec((tm, tn), lambda i, j: (i, j)),
        out_shape=jax.ShapeDtypeStruct((s, n), F32),
        compiler_params=_params(2),
        name="matmul_residual",
    )(a, w, res)


def _matmul_residual_norm_kernel(a_ref, w_ref, r_ref, g_ref, o_ref):
    kk = pl.program_id(1)

    @pl.when(kk == 0)
    def _():
        o_ref[...] = r_ref[...]

    o_ref[...] += _dot(a_ref[...], w_ref[...].astype(BF16))

    @pl.when(kk == pl.num_programs(1) - 1)
    def _():
        o_ref[...] = _rms(o_ref[...], g_ref[...])


def _matmul_residual_norm(a, w, layer, res, norm_g, tm=512, tk=2816):
    s, k = a.shape
    n = w.shape[2]
    return pl.pallas_call(
        _matmul_residual_norm_kernel,
        grid=(s // tm, k // tk),
        in_specs=[pl.BlockSpec((tm, tk), lambda i, kk: (i, kk)),
                  pl.BlockSpec((None, tk, n), lambda i, kk: (layer, kk, 0)),
                  pl.BlockSpec((tm, n), lambda i, kk: (i, 0)),
                  pl.BlockSpec((1, n), lambda i, kk: (0, 0))],
        out_specs=pl.BlockSpec((tm, n), lambda i, kk: (i, 0)),
        out_shape=jax.ShapeDtypeStruct((s, n), F32),
        compiler_params=_params(2),
        name="matmul_residual_norm",
    )(a, w, res, norm_g[None, :])


def _ffn_up_kernel(h_ref, g_ref, wg_ref, wu_ref, o_ref, a_sc):
    _normalize_rows_once(h_ref, g_ref, a_sc)
    a = a_sc[...]
    g = _dot(a, wg_ref[...].astype(BF16))
    u = _dot(a, wu_ref[...].astype(BF16))
    o_ref[...] = (g * jax.nn.sigmoid(g) * u).astype(o_ref.dtype)


def _ffn_up(h, norm_g, w_gate_up, layer, tm=1024, tn=512):
    s, k = h.shape
    d_ff = w_gate_up.shape[2] // 2
    nj = d_ff // tn
    return pl.pallas_call(
        _ffn_up_kernel,
        grid=(s // tm, nj),
        in_specs=[pl.BlockSpec((tm, k), lambda i, j: (i, 0)), pl.BlockSpec((1, k), lambda i, j: (0, 0)),
                  pl.BlockSpec((None, k, tn), lambda i, j: (layer, 0, j)),
                  pl.BlockSpec((None, k, tn), lambda i, j: (layer, 0, nj + j))],
        out_specs=pl.BlockSpec((tm, tn), lambda i, j: (i, j)),
        out_shape=jax.ShapeDtypeStruct((s, d_ff), BF16),
        scratch_shapes=[pltpu.VMEM((tm, k), BF16)],
        compiler_params=_params(2),
        name="ffn_up",
    )(h, norm_g[None, :], w_gate_up, w_gate_up)


def _ffn(h, norm_g, w_gate_up, w_down, layer, out_norm_g=None):
    act = _ffn_up(h, norm_g, w_gate_up, layer)
    if out_norm_g is not None:
        return _matmul_residual_norm(act, w_down, layer, h, out_norm_g)
    return _matmul_residual(act, w_down, layer, h, tm=1024, tn=512)


def kernel(x, positions, attn_norm, mla_w_in, mla_q_norm, mla_kv_norm, mla_w_uq, mla_w_ukv, mla_w_o,
           sb_w_in, sb_w_o, ffn_norm, ffn_w_gate_up, ffn_w_down, final_norm):
    b, s, d = x.shape
    assert b == 1 and d == D_MODEL
    h = x.reshape(s, d)

    attn_tile = 512
    tabs_t = _rope_tables_t(positions.reshape(1, s))
    w_in_pad = jnp.pad(mla_w_in[0], ((0, 0), (0, LANES - QK_ROPE_DIM))).astype(BF16)
    qk_dim = QK_NOPE_DIM + QK_ROPE_DIM
    w_uq_pad = jnp.pad(mla_w_uq[0].reshape(Q_LORA_RANK, MLA_HEADS, qk_dim),
                       ((0, 0), (0, 0), (0, QK_PAD_DIM - qk_dim)))
    w_uq_pad_t = w_uq_pad.reshape(Q_LORA_RANK, MLA_HEADS * QK_PAD_DIM).T.astype(BF16)
    w_ukv = mla_w_ukv[0].reshape(KV_LORA_RANK, MLA_HEADS, QK_NOPE_DIM + V_HEAD_DIM)
    w_uk = w_ukv[:, :, :QK_NOPE_DIM].reshape(KV_LORA_RANK, MLA_HEADS * QK_NOPE_DIM).astype(BF16)
    w_uv_t = w_ukv[:, :, QK_NOPE_DIM:].reshape(KV_LORA_RANK, MLA_HEADS * V_HEAD_DIM).T.astype(BF16)

    cq, ckv, kr = _mla_in(h, attn_norm[0], w_in_pad, mla_q_norm[0], mla_kv_norm[0], positions.reshape(s, 1))
    q_t = _mla_q(cq, w_uq_pad_t, tabs_t, qk_dim ** -0.5 * LOG2_E, attn_tile)
    k, v_t = _mla_kv(ckv, w_uk, w_uv_t, kr, attn_tile)
    to_cast = (mla_w_o, sb_w_in, sb_w_o, ffn_w_gate_up, ffn_w_down)
    o, cast = _mla_attn(q_t, k, v_t, attn_tile, cast_weights=[w.reshape(-1, w.shape[2]) for w in to_cast])
    w_o, w_sb_in, w_sb_o, w_gate_up, w_down = [c.reshape(w.shape) for c, w in zip(cast, to_cast)]
    h = _matmul_residual(o, w_o, 0, h, tm=1024, tn=1024)
    h = _ffn(h, ffn_norm[0], w_gate_up, w_down, 0)

    n_q = SB_HEADS * SB_HEAD_DIM
    col_scale = jnp.concatenate([jnp.full((n_q,), SB_HEAD_DIM ** -0.5 * LOG2_E, F32), jnp.ones((2 * n_q,), F32)])
    qkv = _norm_matmul_scale(h, attn_norm[1], w_sb_in, 0, col_scale)
    o = _sb_attn(qkv)
    h = _matmul_residual(o, w_sb_o, 0, h, tm=1024, tn=1024)
    return _ffn(h, ffn_norm[1], w_gate_up, w_down, 1, out_norm_g=final_norm).reshape(b, s, d)
```
